```python
import jax
import jax.numpy as jnp
from jax import lax
import numpy as np

D_MODEL = 1024
BATCH = 16
SEQ = 2048
DEPTH = 4

GRID_W = 64
CTX_LEN = 256
QBLK = 128
ROPE_THETA = 10000.0
EPS = 1e-6
NEG = -1e30
DH = 64
N_BRANCH = 4
BR_WIDTH = 8 * DH
H_A = 8
Q_LORA = 256
KV_LORA = 128
NOPE_A = 64
ROPE_A = 32
V_A = 64
H_B = 8
KV_B = 2
WINDOW = 128
H_C = 8
KH_MAX = 8
KW = 16
H_D = 8
KV_D = 2
D_FF = 256 * (-(-8 * D_MODEL // (3 * 256)))
MOD_CHUNKS = 6

IN_SPLITS = (
    ('a_cq', Q_LORA), ('a_ckv', KV_LORA), ('a_kr', ROPE_A),
    ('b_q', H_B * DH), ('b_k', KV_B * DH), ('b_v', KV_B * DH),
    ('c_q', H_C * DH), ('c_k', H_C * DH), ('c_v', H_C * DH),
    ('d_q', H_D * DH), ('d_k', KV_D * DH), ('d_v', KV_D * DH),
    ('gate', N_BRANCH * D_MODEL),
)
IN_COLS = sum(w for _, w in IN_SPLITS)

kernel_name = 'hybrid_gated_mixer_dit'


def rmsnorm(x, g):
    xf = x.astype(jnp.float32)
    y = xf * lax.rsqrt(jnp.mean(xf * xf, axis=-1, keepdims=True) + EPS)
    return (y * g.astype(jnp.float32)).astype(x.dtype)


def split_cols(p):
    out = {}
    off = 0
    for name, w in IN_SPLITS:
        out[name] = p[..., off:off + w]
        off += w
    return out


def heads(t, h):
    return t.reshape(t.shape[0], t.shape[1], h, t.shape[-1] // h)


def rope_1d(x, pos):
    d = x.shape[-1]
    inv = ROPE_THETA ** (-jnp.arange(d // 2, dtype=jnp.float32) * 2.0 / d)
    ang = pos.astype(jnp.float32)[:, None] * inv[None, :]
    cos = jnp.cos(ang)[:, None, :]
    sin = jnp.sin(ang)[:, None, :]
    xf = x.astype(jnp.float32)
    x1, x2 = xf[..., :d // 2], xf[..., d // 2:]
    return jnp.concatenate([x1 * cos - x2 * sin, x2 * cos + x1 * sin], axis=-1).astype(x.dtype)


def axial_rope(x, rows_pos, cols_pos):
    h = x.shape[-1] // 2
    return jnp.concatenate([rope_1d(x[..., :h], rows_pos), rope_1d(x[..., h:], cols_pos)], axis=-1)


def attend_dense(q, k, v, scale, sink=None):
    B, Tq, Hkv, G, dk = q.shape
    nb = Tq // QBLK
    qb = jnp.moveaxis(q.reshape(B, nb, QBLK, Hkv, G, dk), 1, 0)

    def block(qi):
        s = jnp.einsum('bqhgd,bkhd->bhgqk', qi, k, preferred_element_type=jnp.float32) * scale
        if sink is not None:
            sk = jnp.broadcast_to(sink.astype(jnp.float32)[None, :, :, None, None], s.shape[:-1] + (1,))
            p = jax.nn.softmax(jnp.concatenate([s, sk], axis=-1), axis=-1)[..., :-1]
        else:
            p = jax.nn.softmax(s, axis=-1)
        return jnp.einsum('bhgqk,bkhd->bqhgd', p.astype(v.dtype), v)

    o = lax.map(block, qb)
    return jnp.moveaxis(o, 0, 1).reshape(B, Tq, Hkv * G * v.shape[-1])


def attend_window(q, k, v, ck, cv, sink, scale):
    B, T, Hkv, G, d = q.shape
    L = ck.shape[1]
    nb = T // QBLK
    pad = ((0, 0), (QBLK, QBLK), (0, 0), (0, 0))

    def band(t):
        tp = jnp.pad(t, pad).reshape(B, nb + 2, QBLK, Hkv, t.shape[-1])
        tb = jnp.concatenate([tp[:, :-2], tp[:, 1:-1], tp[:, 2:]], axis=2)
        return jnp.moveaxis(tb, 1, 0)

    kb, vb = band(k), band(v)
    qb = jnp.moveaxis(q.reshape(B, nb, QBLK, Hkv, G, d), 1, 0)
    qi = jnp.arange(QBLK)[:, None] + QBLK
    kj = jnp.arange(3 * QBLK)[None, :]
    near = jnp.abs(qi - kj) <= WINDOW
    nloc = 3 * QBLK

    def block(args):
        n, qn, kn, vn = args
        kabs = n * QBLK - QBLK + kj
        valid = near & (kabs >= 0) & (kabs < T)
        s_loc = jnp.einsum('bqhgd,bkhd->bhgqk', qn, kn, preferred_element_type=jnp.float32) * scale
        s_loc = jnp.where(valid, s_loc, NEG)
        s_ctx = jnp.einsum('bqhgd,bkhd->bhgqk', qn, ck, preferred_element_type=jnp.float32) * scale
        s_snk = jnp.broadcast_to(sink.astype(jnp.float32)[None, :, :, None, None], s_loc.shape[:-1] + (1,))
        p = jax.nn.softmax(jnp.concatenate([s_loc, s_ctx, s_snk], axis=-1), axis=-1)
        p_loc = p[..., :nloc].astype(v.dtype)
        p_ctx = p[..., nloc:nloc + L].astype(v.dtype)
        return jnp.einsum('bhgqk,bkhd->bqhgd', p_loc, vn) + jnp.einsum('bhgqk,bkhd->bqhgd', p_ctx, cv)

    o = lax.map(block, (jnp.arange(nb), qb, kb, vb))
    return jnp.moveaxis(o, 0, 1).reshape(B, T, Hkv * G * d)


def attend_neighbourhood(q, k, v, ck, cv, rpb, scale):
    B, T, H, d = q.shape
    rows = T // GRID_W
    kh = min(KH_MAX, rows)
    nk = kh * GRID_W
    kg = k.reshape(B, rows, GRID_W, H, d)
    vg = v.reshape(B, rows, GRID_W, H, d)
    qg = jnp.moveaxis(q.reshape(B, rows, GRID_W, H, d), 1, 0)
    qc = jnp.arange(GRID_W)
    c0 = jnp.clip(qc - KW // 2, 0, GRID_W - KW)
    kc = jnp.arange(nk) % GRID_W
    kr_off = jnp.arange(nk) // GRID_W
    col_ok = (kc[None, :] >= c0[:, None]) & (kc[None, :] < c0[:, None] + KW)
    dc_idx = jnp.clip(kc[None, :] - qc[:, None], -(KW - 1), KW - 1) + (KW - 1)
    L = ck.shape[1]

    def row(args):
        r, qr = args
        r0 = jnp.clip(r - kh // 2, 0, rows - kh)
        kr = lax.dynamic_slice_in_dim(kg, r0, kh, axis=1).reshape(B, nk, H, d)
        vr = lax.dynamic_slice_in_dim(vg, r0, kh, axis=1).reshape(B, nk, H, d)
        dr_idx = jnp.clip(r0 + kr_off - r, -(KH_MAX - 1), KH_MAX - 1) + (KH_MAX - 1)
        bias = rpb[:, dr_idx[None, :], dc_idx].astype(jnp.float32)
        s_loc = jnp.einsum('bqhd,bkhd->bhqk', qr, kr, preferred_element_type=jnp.float32) * scale + bias
        s_loc = jnp.where(col_ok, s_loc, NEG)
        s_ctx = jnp.einsum('bqhd,bkhd->bhqk', qr, ck, preferred_element_type=jnp.float32) * scale
        p = jax.nn.softmax(jnp.concatenate([s_loc, s_ctx], axis=-1), axis=-1)
        p_loc = p[..., :nk].astype(v.dtype)
        p_ctx = p[..., nk:nk + L].astype(v.dtype)
        return jnp.einsum('bhqk,bkhd->bqhd', p_loc, vr) + jnp.einsum('bhqk,bkhd->bqhd', p_ctx, cv)

    o = lax.map(row, (jnp.arange(rows), qg))
    return jnp.moveaxis(o, 0, 1).reshape(B, T, H * d)


def gated_merge(gate_pre, branches, w_branch, w_out):
    o = jnp.stack(branches, axis=2)
    up = jnp.einsum('btnw,nwd->btnd', o, w_branch)
    g = jax.nn.sigmoid(gate_pre.reshape(up.shape))
    return jnp.sum(g * up, axis=2) @ w_out


def swiglu(u, w1, w2):
    a, b = jnp.split(u @ w1, 2, axis=-1)
    return (jax.nn.silu(a) * b) @ w2


def token_mixers(ux, uc, rows_pos, cols_pos, ctx_out, w_in, g_a_q, g_a_kv, w_a_uq, w_a_ukv,
                 sink_b, rpb_c, g_d_q, g_d_k, w_branch, w_out):
    B, T, _ = ux.shape
    L = uc.shape[1]
    px = split_cols(ux @ w_in)
    pc = split_cols(uc @ w_in)
    rope = lambda t: axial_rope(t, rows_pos, cols_pos)
    g_b = H_B // KV_B
    g_d = H_D // KV_D
    s_a = (NOPE_A + ROPE_A) ** -0.5
    s_h = DH ** -0.5

    def mla_q(p):
        return heads(rmsnorm(p['a_cq'], g_a_q) @ w_a_uq, H_A)

    def mla_kv(p, rotate):
        kv = heads(rmsnorm(p['a_ckv'], g_a_kv) @ w_a_ukv, H_A)
        kr = p['a_kr'][:, :, None, :]
        if rotate:
            kr = rope(kr)
        k = jnp.concatenate([kv[..., :NOPE_A], jnp.broadcast_to(kr, kv.shape[:-1] + (ROPE_A,))], axis=-1)
        return k, kv[..., NOPE_A:]

    qa = mla_q(px)
    qa = jnp.concatenate([qa[..., :NOPE_A], rope(qa[..., NOPE_A:])], axis=-1)
    ka, va = mla_kv(px, True)
    kca, vca = mla_kv(pc, False)
    oa = attend_dense(qa[:, :, :, None], jnp.concatenate([ka, kca], axis=1),
                      jnp.concatenate([va, vca], axis=1), s_a)

    qb = rope(heads(px['b_q'], H_B)).reshape(B, T, KV_B, g_b, DH)
    kb = rope(heads(px['b_k'], KV_B))
    vb = heads(px['b_v'], KV_B)
    kcb = heads(pc['b_k'], KV_B)
    vcb = heads(pc['b_v'], KV_B)
    sink = sink_b.reshape(KV_B, g_b)
    ob = attend_window(qb, kb, vb, kcb, vcb, sink, s_h)

    kcc = heads(pc['c_k'], H_C)
    vcc = heads(pc['c_v'], H_C)
    oc = attend_neighbourhood(heads(px['c_q'], H_C), heads(px['c_k'], H_C), heads(px['c_v'], H_C),
                              kcc, vcc, rpb_c, s_h)

    qd = rope(rmsnorm(heads(px['d_q'], H_D), g_d_q)).reshape(B, T, KV_D, g_d, DH)
    kd = rope(rmsnorm(heads(px['d_k'], KV_D), g_d_k))
    vd = heads(px['d_v'], KV_D)
    kcd = rmsnorm(heads(pc['d_k'], KV_D), g_d_k)
    vcd = heads(pc['d_v'], KV_D)
    od = attend_dense(qd, jnp.concatenate([kd, kcd], axis=1), jnp.concatenate([vd, vcd], axis=1), s_h)

    yx = gated_merge(px['gate'], (oa, ob, oc, od), w_branch, w_out)
    if not ctx_out:
        return yx, None

    oca = attend_dense(mla_q(pc)[:, :, :, None], kca, vca, s_a)
    ocb = attend_dense(heads(pc['b_q'], H_B).reshape(B, L, KV_B, g_b, DH), kcb, vcb, s_h, sink)
    occ = attend_dense(heads(pc['c_q'], H_C)[:, :, :, None], kcc, vcc, s_h)
    ocd = attend_dense(rmsnorm(heads(pc['d_q'], H_D), g_d_q).reshape(B, L, KV_D, g_d, DH), kcd, vcd, s_h)
    yc = gated_merge(pc['gate'], (oca, ocb, occ, ocd), w_branch, w_out)
    return yx, yc


def setup_inputs(seed: int = 0) -> dict:
    key = jax.random.key(seed)
    ks = jax.random.split(key, 23)
    f32 = jnp.float32

    def nrm(k, shape, std):
        return jax.random.normal(k, shape, f32) * std

    def gain(k, n):
        return 1.0 + 0.05 * jax.random.normal(k, (DEPTH, n), f32)

    L = DEPTH
    return {
        'x': nrm(ks[0], (BATCH, SEQ, D_MODEL), 1.0),
        'c': nrm(ks[1], (BATCH, D_MODEL), 1.0),
        'ctx': nrm(ks[2], (BATCH, CTX_LEN, D_MODEL), 1.0),
        'c_ctx': nrm(ks[3], (D_MODEL,), 1.0),
        'w_mod': nrm(ks[4], (L, D_MODEL, MOD_CHUNKS * D_MODEL), 0.5 * D_MODEL ** -0.5),
        'b_mod': nrm(ks[5], (L, MOD_CHUNKS * D_MODEL), 0.02),
        'g_pre_mix': gain(ks[6], D_MODEL),
        'g_post_mix': gain(ks[7], D_MODEL),
        'g_pre_ffn': gain(ks[8], D_MODEL),
        'g_post_ffn': gain(ks[9], D_MODEL),
        'w_in': nrm(ks[10], (L, D_MODEL, IN_COLS), D_MODEL ** -0.5),
        'g_a_q': gain(ks[11], Q_LORA),
        'g_a_kv': gain(ks[12], KV_LORA),
        'w_a_uq': nrm(ks[13], (L, Q_LORA, H_A * (NOPE_A + ROPE_A)), Q_LORA ** -0.5),
        'w_a_ukv': nrm(ks[14], (L, KV_LORA, H_A * (NOPE_A + V_A)), KV_LORA ** -0.5),
        'sink_b': nrm(ks[15], (L, H_B), 0.5),
        'rpb_c': nrm(ks[16], (L, H_C, 2 * KH_MAX - 1, 2 * KW - 1), 0.2),
        'g_d_q': gain(ks[17], DH),
        'g_d_k': gain(ks[18], DH),
        'w_branch': nrm(ks[19], (L, N_BRANCH, BR_WIDTH, D_MODEL), BR_WIDTH ** -0.5),
        'w_out': nrm(ks[20], (L, D_MODEL, D_MODEL), D_MODEL ** -0.5),
        'w_ffn_in': nrm(ks[21], (L, D_MODEL, 2 * D_FF), D_MODEL ** -0.5),
        'w_ffn_out': nrm(ks[22], (L, D_FF, D_MODEL), D_FF ** -0.5),
    }


def reference(x, c, ctx, c_ctx, w_mod, b_mod, g_pre_mix, g_post_mix, g_pre_ffn, g_post_ffn,
              w_in, g_a_q, g_a_kv, w_a_uq, w_a_ukv, sink_b, rpb_c, g_d_q, g_d_k,
              w_branch, w_out, w_ffn_in, w_ffn_out):
    T = x.shape[1]
    t = jnp.arange(T)
    rows_pos = t // GRID_W
    cols_pos = t % GRID_W
    sc = jax.nn.silu(c)
    scc = jax.nn.silu(c_ctx)
    h, hc = x, ctx
    for l in range(DEPTH):
        last = l == DEPTH - 1
        sh1x, sc1x, ga1x, sh2x, sc2x, ga2x = jnp.split((sc @ w_mod[l] + b_mod[l])[:, None, :], MOD_CHUNKS, axis=-1)
        sh1c, sc1c, ga1c, sh2c, sc2c, ga2c = jnp.split(scc @ w_mod[l] + b_mod[l], MOD_CHUNKS, axis=-1)
        ux = rmsnorm(h, g_pre_mix[l]) * (1 + sc1x) + sh1x
        uc = rmsnorm(hc, g_pre_mix[l]) * (1 + sc1c) + sh1c
        yx, yc = token_mixers(ux, uc, rows_pos, cols_pos, not last, w_in[l], g_a_q[l], g_a_kv[l],
                              w_a_uq[l], w_a_ukv[l], sink_b[l], rpb_c[l], g_d_q[l], g_d_k[l],
                              w_branch[l], w_out[l])
        h = h + ga1x * rmsnorm(yx, g_post_mix[l])
        ux = rmsnorm(h, g_pre_ffn[l]) * (1 + sc2x) + sh2x
        h = h + ga2x * rmsnorm(swiglu(ux, w_ffn_in[l], w_ffn_out[l]), g_post_ffn[l])
        if not last:
            hc = hc + ga1c * rmsnorm(yc, g_post_mix[l])
            uc = rmsnorm(hc, g_pre_ffn[l]) * (1 + sc2c) + sh2c
            hc = hc + ga2c * rmsnorm(swiglu(uc, w_ffn_in[l], w_ffn_out[l]), g_post_ffn[l])
    return h
```

```python
import functools

import jax
import jax.numpy as jnp
from jax import lax
from jax.experimental import pallas as pl
from jax.experimental.pallas import tpu as pltpu

F32 = jnp.float32
BF16 = jnp.bfloat16

GRID_W = 64
ROPE_THETA = 10000.0
EPS = 1e-6
NEG = -1e30
DH = 64
N_HEADS = 8
N_BRANCH = 4
Q_LORA = 256
KV_LORA = 128
NOPE_A = 64
ROPE_A = 32
KV_B = 2
WINDOW = 128
KH = 8
KW = 16
KV_D = 2
MOD_CHUNKS = 6

LANES = 128
F32_SUBLANES = 8
VMEM_LIMIT_BYTES = 56 * 1024 * 1024

TM = 256
HEAD_PAD_A = 128
S_A = (NOPE_A + ROPE_A) ** -0.5
S_H = DH ** -0.5


def _cparams(n_axes):
    return pltpu.CompilerParams(
        dimension_semantics=("arbitrary",) * n_axes, vmem_limit_bytes=VMEM_LIMIT_BYTES)


def _resident(shape):
    nd = len(shape)
    return pl.BlockSpec(shape, lambda *_: (0,) * nd, pipeline_mode=pl.Buffered(1))


def _dot(a, b):
    return jnp.dot(a, b, preferred_element_type=F32)


def _dot_nt(a, b):
    return lax.dot_general(a, b, (((1,), (1,)), ((), ())), preferred_element_type=F32)


def _rms(x, g):
    ms = jnp.mean(x * x, axis=-1, keepdims=True)
    return x * lax.rsqrt(ms + EPS) * g


def _sigmoid(x):
    return 1.0 / (1.0 + jnp.exp(-x))


def _rope(x, tab_ref, shift):
    n = x.shape[-1]
    return (x * tab_ref[0] + pltpu.roll(x, n - shift, 1) * tab_ref[1]
            + pltpu.roll(x, shift, 1) * tab_ref[2])


def _head_rms(x, g, bd):
    x2 = x * x
    hi = x2.astype(BF16)
    lo = (x2 - hi.astype(F32)).astype(BF16)
    ss = _dot(hi, bd) + _dot(lo, bd)
    return x * lax.rsqrt(ss * (1.0 / DH) + EPS) * g


def _softmax_pv(parts, sink=None):
    m = None
    for s, _ in parts:
        mi = jnp.max(s, axis=1, keepdims=True)
        m = mi if m is None else jnp.maximum(m, mi)
    if sink is not None:
        m = jnp.maximum(m, sink)
    l = None
    acc = None
    for s, v in parts:
        p = jnp.exp(s - m)
        li = jnp.sum(p, axis=1, keepdims=True)
        ai = _dot(p.astype(BF16), v)
        l = li if l is None else l + li
        acc = ai if acc is None else acc + ai
    if sink is not None:
        l = l + jnp.exp(sink - m)
    return acc / l


def _mod_kernel(c_ref, w_ref, b_ref, o_ref):
    c = c_ref[...]
    sc = (c * _sigmoid(c)).astype(BF16)
    o_ref[...] = _dot(sc, w_ref[...].astype(BF16)) + b_ref[...]


def _modulation(cc, w_mod, b_mod):
    n_layers, d, n = w_mod.shape
    r = cc.shape[0]
    tn = n // 4
    return pl.pallas_call(
        _mod_kernel,
        grid=(n_layers, n // tn),
        in_specs=[
            pl.BlockSpec((r, d), lambda l, j: (0, 0)),
            pl.BlockSpec((None, d, tn), lambda l, j: (l, 0, j)),
            pl.BlockSpec((None, 1, tn), lambda l, j: (l, 0, j)),
        ],
        out_specs=pl.BlockSpec((None, r, tn), lambda l, j: (l, 0, j)),
        out_shape=jax.ShapeDtypeStruct((n_layers, r, n), F32),
        compiler_params=_cparams(2),
        name="modulation",
    )(cc, w_mod, b_mod.reshape(n_layers, 1, n))


def _nbr_bias_kernel(rpb_ref, o_ref):
    h = pl.program_id(0)
    n_dr = 2 * KH - 1
    n_dc = 2 * KW - 1
    qc = lax.broadcasted_iota(jnp.int32, (GRID_W, 2 * GRID_W), 0)
    lane = lax.broadcasted_iota(jnp.int32, (GRID_W, 2 * GRID_W), 1)
    second = lane >= GRID_W
    kc = jnp.where(second, lane - GRID_W, lane)
    c0 = jnp.clip(qc - KW // 2, 0, GRID_W - KW)
    ok = (kc >= c0) & (kc < c0 + KW)
    dc = kc - qc + (KW - 1)

    def d2_body(d2, carry):
        def d_body(d, acc):
            v0 = rpb_ref[(h * n_dr + d2) * n_dc + d]
            v1 = rpb_ref[(h * n_dr + d2 + 1) * n_dc + d]
            return jnp.where(dc == d, jnp.where(second, v1, v0), acc)

        acc = lax.fori_loop(0, n_dc, d_body, jnp.zeros((GRID_W, 2 * GRID_W), F32))
        o_ref[d2] = jnp.where(ok, acc, NEG)
        return carry

    lax.fori_loop(0, n_dr - 1, d2_body, 0)


def _nbr_bias(rpb):
    h = rpb.shape[0]
    n_pairs = 2 * KH - 2
    return pl.pallas_call(
        _nbr_bias_kernel,
        grid=(h,),
        in_specs=[pl.BlockSpec(memory_space=pltpu.SMEM)],
        out_specs=pl.BlockSpec((None, n_pairs, GRID_W, 2 * GRID_W), lambda i: (i, 0, 0, 0)),
        out_shape=jax.ShapeDtypeStruct((h, n_pairs, GRID_W, 2 * GRID_W), F32),
        compiler_params=_cparams(1),
        name="nbr_bias",
    )(rpb.reshape(-1))


def _proj_kernel(h_ref, mod_ref, g_ref, wa_ref, wb_ref, wc_ref, wd_ref, wg_ref, wuq_ref, wk_ref, wv_ref,
                 gaq_ref, gakv_ref, gdq_ref, gdk_ref, ropea_ref, rope64_ref,
                 pa_ref, pb_ref, pc_ref, pd_ref, gate_ref):
    d_model = h_ref.shape[1]
    hq = N_HEADS * DH
    sh1 = mod_ref[0:1, :]
    sc1 = mod_ref[1:2, :]
    u = (_rms(h_ref[...], g_ref[...]) * (1.0 + sc1) + sh1).astype(BF16)

    rb = jnp.where(lax.broadcasted_iota(jnp.int32, (LANES, LANES), 0) < DH, 0, 1)
    cb = jnp.where(lax.broadcasted_iota(jnp.int32, (LANES, LANES), 1) < DH, 0, 1)
    bd = jnp.where(rb == cb, 1.0, 0.0).astype(BF16)

    ya = _dot(u, wa_ref[...])
    cq = _rms(ya[:, :Q_LORA], gaq_ref[...]).astype(BF16)
    ckv = _rms(ya[:, Q_LORA:Q_LORA + KV_LORA], gakv_ref[...]).astype(BF16)
    kr = _rope(ya[:, Q_LORA + KV_LORA:], ropea_ref, ROPE_A // 4)
    qa = _dot(cq, wuq_ref[...])
    ka = _dot(ckv, wk_ref[...])
    wq_a = N_HEADS * HEAD_PAD_A
    for hh in range(N_HEADS):
        lo, hi = hh * HEAD_PAD_A, (hh + 1) * HEAD_PAD_A
        pa_ref[:, lo:hi] = (_rope(qa[:, lo:hi], ropea_ref, ROPE_A // 4) * S_A).astype(BF16)
        pa_ref[:, wq_a + lo:wq_a + hi] = (ka[:, lo:hi] + kr).astype(BF16)
    pa_ref[:, 2 * wq_a:] = _dot(ckv, wv_ref[...]).astype(BF16)

    yb = _dot(u, wb_ref[...])
    for c in range(hq // LANES):
        lo, hi = c * LANES, (c + 1) * LANES
        pb_ref[:, lo:hi] = (_rope(yb[:, lo:hi], rope64_ref, DH // 4) * S_H).astype(BF16)
    pb_ref[:, hq:hq + LANES] = _rope(yb[:, hq:hq + LANES], rope64_ref, DH // 4).astype(BF16)
    pb_ref[:, hq + LANES:] = yb[:, hq + LANES:].astype(BF16)

    yc = _dot(u, wc_ref[...])
    pc_ref[:, :hq] = (yc[:, :hq] * S_H).astype(BF16)
    pc_ref[:, hq:] = yc[:, hq:].astype(BF16)

    yd = _dot(u, wd_ref[...])
    for c in range(hq // LANES):
        lo, hi = c * LANES, (c + 1) * LANES
        qn = _head_rms(yd[:, lo:hi], gdq_ref[...], bd)
        pd_ref[:, lo:hi] = (_rope(qn, rope64_ref, DH // 4) * S_H).astype(BF16)
    kn = _head_rms(yd[:, hq:hq + LANES], gdk_ref[...], bd)
    pd_ref[:, hq:hq + LANES] = _rope(kn, rope64_ref, DH // 4).astype(BF16)
    pd_ref[:, hq + LANES:] = yd[:, hq + LANES:].astype(BF16)

    cw = 512
    for c in range(N_BRANCH * d_model // cw):
        gate_ref[:, c * cw:(c + 1) * cw] = _dot(u, wg_ref[:, c * cw:(c + 1) * cw]).astype(BF16)


def _project(h, mod, g_pre, w, tabs, tiles_per_batch):
    m, d = h.shape
    hq = N_HEADS * DH
    wa_cols = N_HEADS * HEAD_PAD_A * 2 + hq
    wb_cols = hq + 2 * LANES
    row = lambda i: (i, 0)
    tile_pos = lambda i: (0, i % tiles_per_batch, 0)
    in_specs = [
        pl.BlockSpec((TM, d), row),
        pl.BlockSpec((None, None, MOD_CHUNKS, d),
                     lambda i: (i // tiles_per_batch, jnp.minimum(i % tiles_per_batch, 1), 0, 0)),
        _resident((1, d)),
        _resident(w["wa"].shape), _resident(w["wb"].shape), _resident(w["wc"].shape),
        _resident(w["wd"].shape), _resident(w["wg"].shape), _resident(w["wuq"].shape),
        _resident(w["wk"].shape), _resident(w["wv"].shape),
        _resident((1, Q_LORA)), _resident((1, KV_LORA)), _resident((1, LANES)), _resident((1, LANES)),
        pl.BlockSpec((3, TM, LANES), tile_pos),
        pl.BlockSpec((3, TM, LANES), tile_pos),
    ]
    out_cols = (wa_cols, wb_cols, 3 * hq, wb_cols, N_BRANCH * d)
    return pl.pallas_call(
        _proj_kernel,
        grid=(m // TM,),
        in_specs=in_specs,
        out_specs=[pl.BlockSpec((TM, c), row) for c in out_cols],
        out_shape=[jax.ShapeDtypeStruct((m, c), BF16) for c in out_cols],
        compiler_params=_cparams(1),
        name="proj",
    )(h, mod, g_pre, w["wa"], w["wb"], w["wc"], w["wd"], w["wg"], w["wuq"], w["wk"], w["wv"],
      w["gaq"], w["gakv"], w["gdq"], w["gdk"], tabs["rope_a"], tabs["rope_64"])


def _ctx_attend(q_ref, k_ref, v_ref, o_ref, n_ctx, dk, group, sink_ref):
    for h in range(N_HEADS):
        g = h // group
        q = q_ref[:, h * dk:(h + 1) * dk]
        k = k_ref[0:n_ctx, g * dk:(g + 1) * dk]
        v = v_ref[0:n_ctx, g * DH:(g + 1) * DH]
        sink = None if sink_ref is None else sink_ref[h]
        o = _softmax_pv([(_dot_nt(q, k), v)], sink)
        o_ref[:, h * DH:(h + 1) * DH] = o.astype(o_ref.dtype)


def _dense_attn_kernel(q_ref, k_ref, v_ref, o_ref, *, n_ctx, dk, group):
    qi = pl.program_id(1)
    n_keys = k_ref.shape[0]

    @pl.when(qi < n_ctx // TM)
    def _():
        _ctx_attend(q_ref, k_ref, v_ref, o_ref, n_ctx, dk, group, None)

    @pl.when(qi >= n_ctx // TM)
    def _():
        for h in range(N_HEADS):
            g = h // group
            q = q_ref[:, h * dk:(h + 1) * dk]
            k = k_ref[:, g * dk:(g + 1) * dk]
            v = v_ref[:, g * DH:(g + 1) * DH]
            o = _softmax_pv([(_dot_nt(q, k), v)])
            o_ref[:, h * DH:(h + 1) * DH] = o.astype(o_ref.dtype)
    del n_keys


def _window_attn_kernel(sink_ref, q_ref, k_ref, v_ref, o_ref, *, n_ctx):
    qi = pl.program_id(1)
    n_lat = k_ref.shape[0] - n_ctx
    span = TM + 2 * WINDOW
    group = N_HEADS // KV_B

    @pl.when(qi < n_ctx // TM)
    def _():
        _ctx_attend(q_ref, k_ref, v_ref, o_ref, n_ctx, DH, group, sink_ref)

    @pl.when(qi >= n_ctx // TM)
    def _():
        q0 = (qi - n_ctx // TM) * TM
        start = pl.multiple_of(jnp.clip(q0 - WINDOW, 0, n_lat - span), WINDOW)
        dist = (lax.broadcasted_iota(jnp.int32, (TM, span), 0)
                - lax.broadcasted_iota(jnp.int32, (TM, span), 1) + (q0 - start))
        bias = jnp.where(jnp.abs(dist) <= WINDOW, 0.0, NEG)
        rows = pl.ds(n_ctx + start, span)
        for h in range(N_HEADS):
            g = h // group
            cols = slice(g * DH, (g + 1) * DH)
            q = q_ref[:, h * DH:(h + 1) * DH]
            s_win = _dot_nt(q, k_ref[rows, cols]) + bias
            s_ctx = _dot_nt(q, k_ref[0:n_ctx, cols])
            o = _softmax_pv([(s_win, v_ref[rows, cols]), (s_ctx, v_ref[0:n_ctx, cols])], sink_ref[h])
            o_ref[:, h * DH:(h + 1) * DH] = o.astype(o_ref.dtype)


def _nbr_attn_kernel(q_ref, k_ref, v_ref, tb_ref, o_ref, *, n_ctx):
    qi = pl.program_id(1)
    n_lat = k_ref.shape[0] - n_ctx
    grid_rows = n_lat // GRID_W
    rows_per_tile = TM // GRID_W
    span = KH * GRID_W

    @pl.when(qi < n_ctx // TM)
    def _():
        _ctx_attend(q_ref, k_ref, v_ref, o_ref, n_ctx, DH, 1, None)

    @pl.when(qi >= n_ctx // TM)
    def _():
        def row_body(rr, carry):
            r = (qi - n_ctx // TM) * rows_per_tile + rr
            r0 = jnp.clip(r - KH // 2, 0, grid_rows - KH)
            delta = r - r0
            krows = pl.ds(pl.multiple_of(n_ctx + r0 * GRID_W, GRID_W), span)
            qrows = pl.ds(pl.multiple_of(rr * GRID_W, GRID_W), GRID_W)
            for h in range(N_HEADS):
                cols = slice(h * DH, (h + 1) * DH)
                q = q_ref[qrows, cols]
                bias = jnp.concatenate(
                    [tb_ref[h, 2 * j - delta + (KH - 1)] for j in range(KH // 2)], axis=1)
                s_win = _dot_nt(q, k_ref[krows, cols]) + bias
                s_ctx = _dot_nt(q, k_ref[0:n_ctx, cols])
                o = _softmax_pv([(s_win, v_ref[krows, cols]), (s_ctx, v_ref[0:n_ctx, cols])])
                o_ref[qrows, cols] = o.astype(o_ref.dtype)
            return carry

        lax.fori_loop(0, rows_per_tile, row_body, 0)


def _attention(kind, p, n_ctx, *, q_w, k_w, v_w, k_blk, v_blk, sink=None, bias_tab=None):
    b, s, _ = p.shape
    hq = N_HEADS * DH
    q_spec = pl.BlockSpec((None, TM, q_w), lambda bi, qi: (bi, qi, 0))
    k_spec = pl.BlockSpec((None, s, k_w), lambda bi, qi: (bi, 0, k_blk))
    v_spec = pl.BlockSpec((None, s, v_w), lambda bi, qi: (bi, 0, v_blk))
    in_specs = [q_spec, k_spec, v_spec]
    args = [p, p, p]
    if kind == "dense":
        body = functools.partial(_dense_attn_kernel, n_ctx=n_ctx, dk=q_w // N_HEADS,
                                 group=N_HEADS * DH // v_w)
    elif kind == "window":
        body = functools.partial(_window_attn_kernel, n_ctx=n_ctx)
        in_specs = [pl.BlockSpec(memory_space=pltpu.SMEM)] + in_specs
        args = [sink] + args
    else:
        body = functools.partial(_nbr_attn_kernel, n_ctx=n_ctx)
        in_specs = in_specs + [_resident(bias_tab.shape)]
        args = args + [bias_tab]
    return pl.pallas_call(
        body,
        grid=(b, s // TM),
        in_specs=in_specs,
        out_specs=pl.BlockSpec((None, TM, hq), lambda bi, qi: (bi, qi, 0)),
        out_shape=jax.ShapeDtypeStruct((b, s, hq), BF16),
        compiler_params=_cparams(2),
        name="attn_" + kind,
    )(*args)


def _merge_kernel(oa_ref, ob_ref, oc_ref, od_ref, gate_ref, h_ref, mod_ref, g_ref, wbr_ref, wout_ref,
                  o_ref, acc_ref):
    d_model = h_ref.shape[1]
    cw = 256
    branches = (oa_ref, ob_ref, oc_ref, od_ref)
    for c in range(d_model // cw):
        acc = None
        for n, br in enumerate(branches):
            up = _dot(br[...], wbr_ref[n, :, c * cw:(c + 1) * cw])
            gt = gate_ref[:, n * d_model + c * cw:n * d_model + (c + 1) * cw].astype(F32)
            term = _sigmoid(gt) * up
            acc = term if acc is None else acc + term
        acc_ref[:, c * cw:(c + 1) * cw] = acc.astype(BF16)
    y = _dot(acc_ref[...], wout_ref[...])
    ga1 = mod_ref[2:3, :]
    o_ref[...] = h_ref[...] + ga1 * _rms(y, g_ref[...])


def _merge(o_a, o_b, o_c, o_d, gate, h, mod, g_post, wbr, wout, tiles_per_batch):
    m, d = h.shape
    hq = N_HEADS * DH
    row = lambda i: (i, 0)
    return pl.pallas_call(
        _merge_kernel,
        grid=(m // TM,),
        in_specs=[pl.BlockSpec((TM, hq), row)] * N_BRANCH + [
            pl.BlockSpec((TM, N_BRANCH * d), row),
            pl.BlockSpec((TM, d), row),
            pl.BlockSpec((None, None, MOD_CHUNKS, d),
                         lambda i: (i // tiles_per_batch, jnp.minimum(i % tiles_per_batch, 1), 0, 0)),
            _resident((1, d)), _resident(wbr.shape), _resident(wout.shape),
        ],
        out_specs=pl.BlockSpec((TM, d), row),
        out_shape=jax.ShapeDtypeStruct((m, d), F32),
        scratch_shapes=[pltpu.VMEM((TM, d), BF16)],
        input_output_aliases={5: 0},
        compiler_params=_cparams(1),
        name="merge",
    )(o_a, o_b, o_c, o_d, gate, h, mod, g_post, wbr, wout)


def _ffn_kernel(h_ref, mod_ref, gpre_ref, gpost_ref, w1_ref, w2_ref, o_ref):
    d_ff = w2_ref.shape[0]
    cw = 256
    h = h_ref[...]
    sh2 = mod_ref[3:4, :]
    sc2 = mod_ref[4:5, :]
    ga2 = mod_ref[5:6, :]
    u = (_rms(h, gpre_ref[...]) * (1.0 + sc2) + sh2).astype(BF16)
    y = None
    for c in range(d_ff // cw):
        a = _dot(u, w1_ref[:, c * cw:(c + 1) * cw])
        b = _dot(u, w1_ref[:, d_ff + c * cw:d_ff + (c + 1) * cw])
        act = (a * _sigmoid(a) * b).astype(BF16)
        t = _dot(act, w2_ref[c * cw:(c + 1) * cw, :])
        y = t if y is None else y + t
    o_ref[...] = h + ga2 * _rms(y, gpost_ref[...])


def _ffn(h, mod, g_pre, g_post, w1, w2, tiles_per_batch):
    m, d = h.shape
    row = lambda i: (i, 0)
    return pl.pallas_call(
        _ffn_kernel,
        grid=(m // TM,),
        in_specs=[
            pl.BlockSpec((TM, d), row),
            pl.BlockSpec((None, None, MOD_CHUNKS, d),
                         lambda i: (i // tiles_per_batch, jnp.minimum(i % tiles_per_batch, 1), 0, 0)),
            _resident((1, d)), _resident((1, d)), _resident(w1.shape), _resident(w2.shape),
        ],
        out_specs=pl.BlockSpec((TM, d), row),
        out_shape=jax.ShapeDtypeStruct((m, d), F32),
        input_output_aliases={0: 0},
        compiler_params=_cparams(1),
        name="ffn",
    )(h, mod, g_pre, g_post, w1, w2)


def _rope_tables(n_ctx, n_lat):
    t = jnp.arange(n_lat)
    rows_pos = (t // GRID_W).astype(F32)
    cols_pos = (t % GRID_W).astype(F32)

    def block(d):
        q = d // 4
        inv = ROPE_THETA ** (-jnp.arange(q, dtype=F32) * 2.0 / (d // 2))
        ar = rows_pos[:, None] * inv[None, :]
        ac = cols_pos[:, None] * inv[None, :]
        z = jnp.zeros_like(ar)
        cos = jnp.concatenate([jnp.cos(ar), jnp.cos(ar), jnp.cos(ac), jnp.cos(ac)], axis=1)
        up = jnp.concatenate([-jnp.sin(ar), z, -jnp.sin(ac), z], axis=1)
        dn = jnp.concatenate([z, jnp.sin(ar), z, jnp.sin(ac)], axis=1)
        return cos, up, dn

    def with_ctx(tab, fill):
        return jnp.concatenate([jnp.full((n_ctx, tab.shape[1]), fill, F32), tab], axis=0)

    c64, u64, d64 = block(DH)
    rope_64 = jnp.stack([with_ctx(jnp.tile(c64, (1, LANES // DH)), 1.0),
                         with_ctx(jnp.tile(u64, (1, LANES // DH)), 0.0),
                         with_ctx(jnp.tile(d64, (1, LANES // DH)), 0.0)])
    ca, ua, da = block(ROPE_A)
    pad_lo = NOPE_A
    pad_hi = HEAD_PAD_A - NOPE_A - ROPE_A

    def widen(tab, fill):
        return jnp.concatenate([jnp.full((n_lat, pad_lo), fill, F32), tab,
                                jnp.full((n_lat, pad_hi), fill, F32)], axis=1)

    rope_a = jnp.stack([with_ctx(widen(ca, 1.0), 1.0), with_ctx(widen(ua, 0.0), 0.0),
                        with_ctx(widen(da, 0.0), 0.0)])
    return {"rope_a": rope_a, "rope_64": rope_64}


def _layer_weights(l, w_in, g_a_q, g_a_kv, w_a_uq, w_a_ukv, g_d_q, g_d_k):
    d = w_in.shape[1]
    hq = N_HEADS * DH
    wl = w_in[l]
    o = 0
    a_cq = wl[:, o:o + Q_LORA]; o += Q_LORA
    a_ckv = wl[:, o:o + KV_LORA]; o += KV_LORA
    a_kr = wl[:, o:o + ROPE_A]; o += ROPE_A
    w_b = wl[:, o:o + hq + 2 * KV_B * DH]; o += hq + 2 * KV_B * DH
    w_c = wl[:, o:o + 3 * hq]; o += 3 * hq
    w_d = wl[:, o:o + hq + 2 * KV_D * DH]; o += hq + 2 * KV_D * DH
    w_g = wl[:, o:]
    zeros = lambda n: jnp.zeros((d, n), wl.dtype)
    wa = jnp.concatenate([a_cq, a_ckv, zeros(NOPE_A), a_kr, zeros(HEAD_PAD_A - NOPE_A - ROPE_A)], axis=1)
    uq = w_a_uq[l].reshape(Q_LORA, N_HEADS, NOPE_A + ROPE_A)
    uq = jnp.pad(uq, ((0, 0), (0, 0), (0, HEAD_PAD_A - NOPE_A - ROPE_A))).reshape(Q_LORA, N_HEADS * HEAD_PAD_A)
    ukv = w_a_ukv[l].reshape(KV_LORA, N_HEADS, NOPE_A + DH)
    wk = jnp.pad(ukv[:, :, :NOPE_A], ((0, 0), (0, 0), (0, HEAD_PAD_A - NOPE_A))).reshape(KV_LORA, N_HEADS * HEAD_PAD_A)
    wv = ukv[:, :, NOPE_A:].reshape(KV_LORA, N_HEADS * DH)
    cast = lambda x: x.astype(BF16)
    return {
        "wa": cast(wa), "wb": cast(w_b), "wc": cast(w_c), "wd": cast(w_d), "wg": cast(w_g),
        "wuq": cast(uq), "wk": cast(wk), "wv": cast(wv),
        "gaq": g_a_q[l][None, :], "gakv": g_a_kv[l][None, :],
        "gdq": jnp.tile(g_d_q[l], LANES // DH)[None, :], "gdk": jnp.tile(g_d_k[l], LANES // DH)[None, :],
    }


def kernel(x, c, ctx, c_ctx, w_mod, b_mod, g_pre_mix, g_post_mix, g_pre_ffn, g_post_ffn,
           w_in, g_a_q, g_a_kv, w_a_uq, w_a_ukv, sink_b, rpb_c, g_d_q, g_d_k,
           w_branch, w_out, w_ffn_in, w_ffn_out):
    b, n_lat, d = x.shape
    n_ctx = ctx.shape[1]
    s = n_ctx + n_lat
    depth = w_mod.shape[0]
    hq = N_HEADS * DH
    assert n_ctx == TM and n_lat % TM == 0 and n_lat >= TM + 2 * WINDOW
    assert n_lat % GRID_W == 0 and n_lat // GRID_W >= KH
    tiles_per_batch = s // TM

    n_rows = -(-(b + 1) // F32_SUBLANES) * F32_SUBLANES
    cc = jnp.zeros((n_rows, d), F32).at[:b].set(c).at[b].set(c_ctx)
    mods = _modulation(cc, w_mod, b_mod)
    mod_x = mods[:, :b].reshape(depth, b, 1, MOD_CHUNKS, d)
    mod_c = jnp.broadcast_to(mods[:, b].reshape(depth, 1, 1, MOD_CHUNKS, d), mod_x.shape)
    mod_tab = jnp.concatenate([mod_c, mod_x], axis=2)

    tabs = _rope_tables(n_ctx, n_lat)
    h = jnp.concatenate([ctx, x], axis=1).reshape(b * s, d)
    wq_a = N_HEADS * HEAD_PAD_A

    for l in range(depth):
        w = _layer_weights(l, w_in, g_a_q, g_a_kv, w_a_uq, w_a_ukv, g_d_q, g_d_k)
        mod = mod_tab[l]
        pa, pb, pc, pd, gate = _project(h, mod, g_pre_mix[l][None, :], w, tabs, tiles_per_batch)
        pa = pa.reshape(b, s, -1)
        pb = pb.reshape(b, s, -1)
        pc = pc.reshape(b, s, -1)
        pd = pd.reshape(b, s, -1)
        o_a = _attention("dense", pa, n_ctx, q_w=wq_a, k_w=wq_a, v_w=hq, k_blk=1, v_blk=2 * wq_a // hq)
        o_b = _attention("window", pb, n_ctx, q_w=hq, k_w=LANES, v_w=LANES, k_blk=hq // LANES,
                         v_blk=hq // LANES + 1, sink=sink_b[l])
        o_c = _attention("nbr", pc, n_ctx, q_w=hq, k_w=hq, v_w=hq, k_blk=1, v_blk=2,
                         bias_tab=_nbr_bias(rpb_c[l]))
        o_d = _attention("dense", pd, n_ctx, q_w=hq, k_w=LANES, v_w=LANES, k_blk=hq // LANES,
                         v_blk=hq // LANES + 1)
        flat = lambda t: t.reshape(b * s, hq)
        h = _merge(flat(o_a), flat(o_b), flat(o_c), flat(o_d), gate, h, mod, g_post_mix[l][None, :],
                   w_branch[l].astype(BF16), w_out[l].astype(BF16), tiles_per_batch)
        h = _ffn(h, mod, g_pre_ffn[l][None, :], g_post_ffn[l][None, :],
                 w_ffn_in[l].astype(BF16), w_ffn_out[l].astype(BF16), tiles_per_batch)
    return h.reshape(b, s, d)[:, n_ctx:, :]
```

```python
import functools

import jax
import jax.numpy as jnp
from jax import lax
from jax.experimental import pallas as pl
from jax.experimental.pallas import tpu as pltpu

F32 = jnp.float32
BF16 = jnp.bfloat16

GRID_W = 64
ROPE_THETA = 10000.0
EPS = 1e-6
NEG = -1e30
DH = 64
N_HEADS = 8
N_BRANCH = 4
Q_LORA = 256
KV_LORA = 128
NOPE_A = 64
ROPE_A = 32
KV_B = 2
WINDOW = 128
KH = 8
KW = 16
KV_D = 2
MOD_CHUNKS = 6

LANES = 128
F32_SUBLANES = 8
VMEM_LIMIT_BYTES = 56 * 1024 * 1024

TM = 256
HEAD_PAD_A = 128
HQ = N_HEADS * DH
S_A = (NOPE_A + ROPE_A) ** -0.5
S_H = DH ** -0.5
LOG2E = 1.4426950408889634
KEY_CHUNK = 768
GQA_HEAD_ORDER = tuple(h for j in range(N_HEADS // 2) for h in (j, j + N_HEADS // 2))


def _cparams(n_axes):
    return pltpu.CompilerParams(
        dimension_semantics=("arbitrary",) * n_axes, vmem_limit_bytes=VMEM_LIMIT_BYTES)


def _resident(shape):
    nd = len(shape)
    return pl.BlockSpec(shape, lambda *_: (0,) * nd, pipeline_mode=pl.Buffered(1))


def _mod_spec(d, tiles_per_batch):
    return pl.BlockSpec((None, None, MOD_CHUNKS, d),
                        lambda i: (i // tiles_per_batch, jnp.minimum(i % tiles_per_batch, 1), 0, 0))


def _dot(a, b):
    return jnp.dot(a, b, preferred_element_type=F32)


def _dot_nt(a, b):
    return lax.dot_general(a, b, (((1,), (1,)), ((), ())), preferred_element_type=F32)


def _rms(x, g):
    ms = jnp.mean(x * x, axis=-1, keepdims=True)
    return x * lax.rsqrt(ms + EPS) * g


def _sigmoid(x):
    return 1.0 / (1.0 + jnp.exp(-x))


def _rope(x, tab_ref, shift):
    n = x.shape[-1]
    return (x * tab_ref[0] + pltpu.roll(x, n - shift, 1) * tab_ref[1]
            + pltpu.roll(x, shift, 1) * tab_ref[2])


def _head_rms(x, g, bd):
    x2 = x * x
    hi = x2.astype(BF16)
    lo = (x2 - hi.astype(F32)).astype(BF16)
    ss = _dot(hi, bd) + _dot(lo, bd)
    return x * lax.rsqrt(ss * (1.0 / DH) + EPS) * g


def _softmax_pv(parts, sink=None):
    m = None
    for s, _ in parts:
        mi = jnp.max(s, axis=1, keepdims=True)
        m = mi if m is None else jnp.maximum(m, mi)
    if sink is not None:
        m = jnp.maximum(m, sink)
    l = None
    acc = None
    for s, v in parts:
        p = jnp.exp(s - m)
        li = jnp.sum(p, axis=1, keepdims=True)
        ai = _dot(p.astype(BF16), v)
        l = li if l is None else l + li
        acc = ai if acc is None else acc + ai
    if sink is not None:
        l = l + jnp.exp(sink - m)
    return acc / l


def _pair_attend(q_blk, parts, sinks=None):
    m = q_blk.shape[0]
    lane = lax.broadcasted_iota(jnp.int32, (m, LANES), 1)
    zero = jnp.zeros_like(q_blk)
    qs = jnp.concatenate([jnp.where(lane < DH, q_blk, zero), jnp.where(lane >= DH, q_blk, zero)], axis=0)
    scored = []
    for k, v, bias in parts:
        s = _dot_nt(qs, k)
        scored.append((s if bias is None else s + bias, v))
    sink = None
    if sinks is not None:
        row = lax.broadcasted_iota(jnp.int32, (2 * m, 1), 0)
        sink = jnp.where(row < m, sinks[0], sinks[1])
    o = _softmax_pv(scored, sink)
    return jnp.where(lane < DH, o[:m], o[m:])


def _mod_kernel(c_ref, w_ref, b_ref, o_ref):
    c = c_ref[...]
    sc = (c * _sigmoid(c)).astype(BF16)
    o_ref[...] = _dot(sc, w_ref[...].astype(BF16)) + b_ref[...]


def _modulation(cc, w_mod, b_mod):
    n_layers, d, n = w_mod.shape
    r = cc.shape[0]
    tn = n // 4
    return pl.pallas_call(
        _mod_kernel,
        grid=(n_layers, n // tn),
        in_specs=[
            pl.BlockSpec((r, d), lambda l, j: (0, 0)),
            pl.BlockSpec((None, d, tn), lambda l, j: (l, 0, j)),
            pl.BlockSpec((None, 1, tn), lambda l, j: (l, 0, j)),
        ],
        out_specs=pl.BlockSpec((None, r, tn), lambda l, j: (l, 0, j)),
        out_shape=jax.ShapeDtypeStruct((n_layers, r, n), F32),
        compiler_params=_cparams(2),
        name="modulation",
    )(cc, w_mod, b_mod.reshape(n_layers, 1, n))


def _nbr_bias_kernel(rpb_ref, o_ref):
    pair = pl.program_id(0)
    n_dr = 2 * KH - 1
    n_dc = 2 * KW - 1
    qc = lax.broadcasted_iota(jnp.int32, (GRID_W, 2 * GRID_W), 0)
    lane = lax.broadcasted_iota(jnp.int32, (GRID_W, 2 * GRID_W), 1)
    second = lane >= GRID_W
    kc = jnp.where(second, lane - GRID_W, lane)
    c0 = jnp.clip(qc - KW // 2, 0, GRID_W - KW)
    ok = (kc >= c0) & (kc < c0 + KW)
    dc = kc - qc + (KW - 1)

    for e in range(2):
        h = 2 * pair + e

        def d2_body(d2, carry, h=h, e=e):
            def d_body(d, acc):
                v0 = rpb_ref[(h * n_dr + d2) * n_dc + d]
                v1 = rpb_ref[(h * n_dr + d2 + 1) * n_dc + d]
                return jnp.where(dc == d, jnp.where(second, v1, v0), acc)

            acc = lax.fori_loop(0, n_dc, d_body, jnp.zeros((GRID_W, 2 * GRID_W), F32))
            o_ref[d2, e * GRID_W:(e + 1) * GRID_W, :] = jnp.where(ok, acc, NEG)
            return carry

        lax.fori_loop(0, n_dr - 1, d2_body, 0)


def _nbr_bias(rpb):
    h = rpb.shape[0]
    n_pairs = 2 * KH - 2
    return pl.pallas_call(
        _nbr_bias_kernel,
        grid=(h // 2,),
        in_specs=[pl.BlockSpec(memory_space=pltpu.SMEM)],
        out_specs=pl.BlockSpec((None, n_pairs, 2 * GRID_W, 2 * GRID_W), lambda i: (i, 0, 0, 0)),
        out_shape=jax.ShapeDtypeStruct((h // 2, n_pairs, 2 * GRID_W, 2 * GRID_W), F32),
        compiler_params=_cparams(1),
        name="nbr_bias",
    )(rpb.reshape(-1))


def _proj_kernel(h_ref, mod_ref, g_ref, wa_ref, wb_ref, wc_ref, wd_ref, wuq_ref, wk_ref, wv_ref,
                 gaq_ref, gakv_ref, gdq_ref, gdk_ref, ropea_ref, rope64_ref,
                 ka_ref, qat_ref, vat_ref, pb_ref, pc_ref, kd_ref, qdt_ref, vdt_ref):
    sh1 = mod_ref[0:1, :]
    sc1 = mod_ref[1:2, :]
    u = (_rms(h_ref[...], g_ref[...]) * (1.0 + sc1) + sh1).astype(BF16)

    rb = jnp.where(lax.broadcasted_iota(jnp.int32, (LANES, LANES), 0) < DH, 0, 1)
    cb = jnp.where(lax.broadcasted_iota(jnp.int32, (LANES, LANES), 1) < DH, 0, 1)
    bd = jnp.where(rb == cb, 1.0, 0.0).astype(BF16)

    ya = _dot(u, wa_ref[...])
    cq = _rms(ya[:, :Q_LORA], gaq_ref[...]).astype(BF16)
    ckv = _rms(ya[:, Q_LORA:Q_LORA + KV_LORA], gakv_ref[...]).astype(BF16)
    kr = _rope(ya[:, Q_LORA + KV_LORA:], ropea_ref, ROPE_A // 4)
    qa = _dot(cq, wuq_ref[...])
    ka = _dot(ckv, wk_ref[...])
    va = _dot(ckv, wv_ref[...])
    for hh in range(N_HEADS):
        lo, hi = hh * HEAD_PAD_A, (hh + 1) * HEAD_PAD_A
        q = _rope(qa[:, lo:hi], ropea_ref, ROPE_A // 4) * (S_A * LOG2E)
        qat_ref[lo:hi, :] = q.T.astype(BF16)
        ka_ref[:, lo:hi] = (ka[:, lo:hi] + kr).astype(BF16)
    for c in range(HQ // LANES):
        lo, hi = c * LANES, (c + 1) * LANES
        vat_ref[lo:hi, :] = va[:, lo:hi].T.astype(BF16)

    yb = _dot(u, wb_ref[...])
    for c in range(HQ // LANES):
        lo, hi = c * LANES, (c + 1) * LANES
        pb_ref[:, lo:hi] = (_rope(yb[:, lo:hi], rope64_ref, DH // 4) * S_H).astype(BF16)
    pb_ref[:, HQ:HQ + LANES] = _rope(yb[:, HQ:HQ + LANES], rope64_ref, DH // 4).astype(BF16)
    pb_ref[:, HQ + LANES:] = yb[:, HQ + LANES:].astype(BF16)

    yc = _dot(u, wc_ref[...])
    pc_ref[:, :HQ] = (yc[:, :HQ] * S_H).astype(BF16)
    pc_ref[:, HQ:] = yc[:, HQ:].astype(BF16)

    yd = _dot(u, wd_ref[...])
    for c in range(HQ // LANES):
        lo, hi = c * LANES, (c + 1) * LANES
        qn = _head_rms(yd[:, lo:hi], gdq_ref[...], bd)
        qdt_ref[lo:hi, :] = (_rope(qn, rope64_ref, DH // 4) * (S_H * LOG2E)).T.astype(BF16)
    kn = _head_rms(yd[:, HQ:HQ + LANES], gdk_ref[...], bd)
    kd_ref[...] = _rope(kn, rope64_ref, DH // 4).astype(BF16)
    vdt_ref[...] = yd[:, HQ + LANES:].T.astype(BF16)


def _project(h, mod, g_pre, w, tabs, n_batch, tiles_per_batch):
    m, d = h.shape
    s = tiles_per_batch * TM
    wq_a = N_HEADS * HEAD_PAD_A
    row = lambda i: (i, 0)
    col = lambda i: (i // tiles_per_batch, 0, i % tiles_per_batch)
    tile_pos = lambda i: (0, i % tiles_per_batch, 0)
    in_specs = [
        pl.BlockSpec((TM, d), row),
        _mod_spec(d, tiles_per_batch),
        _resident((1, d)),
        _resident(w["wa"].shape), _resident(w["wb"].shape), _resident(w["wc"].shape),
        _resident(w["wd"].shape), _resident(w["wuq"].shape),
        _resident(w["wk"].shape), _resident(w["wv"].shape),
        _resident((1, Q_LORA)), _resident((1, KV_LORA)), _resident((1, LANES)), _resident((1, LANES)),
        pl.BlockSpec((3, TM, LANES), tile_pos),
        pl.BlockSpec((3, TM, LANES), tile_pos),
    ]

    def rows_out(c):
        return pl.BlockSpec((TM, c), row), jax.ShapeDtypeStruct((m, c), BF16)

    def cols_out(r):
        return pl.BlockSpec((None, r, TM), col), jax.ShapeDtypeStruct((n_batch, r, s), BF16)

    outs = [rows_out(wq_a), cols_out(wq_a), cols_out(HQ),
            rows_out(HQ + 2 * LANES), rows_out(3 * HQ),
            rows_out(LANES), cols_out(HQ), cols_out(LANES)]
    return pl.pallas_call(
        _proj_kernel,
        grid=(m // TM,),
        in_specs=in_specs,
        out_specs=[o[0] for o in outs],
        out_shape=[o[1] for o in outs],
        compiler_params=_cparams(1),
        name="proj",
    )(h, mod, g_pre, w["wa"], w["wb"], w["wc"], w["wd"], w["wuq"], w["wk"], w["wv"],
      w["gaq"], w["gakv"], w["gdq"], w["gdk"], tabs["rope_a"], tabs["rope_64"])


def _dense_attn_kernel(k_ref, qt_ref, vt_ref, o_ref, ot_ref, *, n_ctx, dk):
    qi = pl.program_id(1)
    group = 1 if dk == LANES else N_HEADS // (LANES // DH)

    def attend(n_keys):
        n_chunks = -(-n_keys // KEY_CHUNK)
        bounds = [(i * n_keys // n_chunks, (i + 1) * n_keys // n_chunks) for i in range(n_chunks)]

        def q_operand(h):
            if dk == LANES:
                return qt_ref[h * LANES:(h + 1) * LANES, :]
            qh = qt_ref[h * DH:(h + 1) * DH, :]
            z = jnp.zeros_like(qh)
            return jnp.concatenate([qh, z] if h // group == 0 else [z, qh], axis=0)

        def scores(h, qt, i):
            lanes = slice(h * LANES, (h + 1) * LANES) if dk == LANES else slice(0, LANES)
            return _dot(k_ref[bounds[i][0]:bounds[i][1], lanes], qt)

        def colmax(parts):
            m = None
            for st in parts:
                mi = jnp.max(st, axis=0, keepdims=True)
                m = mi if m is None else jnp.maximum(m, mi)
            return m

        qt = q_operand(0)
        cur = [scores(0, qt, i) for i in range(n_chunks)]
        for h in range(N_HEADS):
            g = h // group
            m = colmax(cur)
            qt = q_operand(h + 1) if h + 1 < N_HEADS else None
            nxt = []
            l = acc = None
            for i in range(n_chunks):
                if qt is not None:
                    nxt.append(scores(h + 1, qt, i))
                p = jnp.exp2(cur[i] - m)
                li = jnp.sum(p, axis=0, keepdims=True)
                ai = _dot(vt_ref[g * DH:(g + 1) * DH, bounds[i][0]:bounds[i][1]], p.astype(BF16))
                l = li if l is None else l + li
                acc = ai if acc is None else acc + ai
            ot_ref[h * DH:(h + 1) * DH, :] = acc / l
            cur = nxt
        o_ref[...] = ot_ref[...].T.astype(o_ref.dtype)

    @pl.when(qi < n_ctx // TM)
    def _():
        attend(n_ctx)

    @pl.when(qi >= n_ctx // TM)
    def _():
        attend(k_ref.shape[0])


def _dense_attention(k, qt, vt, n_ctx, dk):
    b, s, kw = k.shape
    return pl.pallas_call(
        functools.partial(_dense_attn_kernel, n_ctx=n_ctx, dk=dk),
        grid=(b, s // TM),
        in_specs=[
            pl.BlockSpec((None, s, kw), lambda bi, qi: (bi, 0, 0)),
            pl.BlockSpec((None, qt.shape[1], TM), lambda bi, qi: (bi, 0, qi)),
            pl.BlockSpec((None, vt.shape[1], s), lambda bi, qi: (bi, 0, 0)),
        ],
        out_specs=pl.BlockSpec((None, TM, HQ), lambda bi, qi: (bi, qi, 0)),
        out_shape=jax.ShapeDtypeStruct((b, s, HQ), BF16),
        scratch_shapes=[pltpu.VMEM((HQ, TM), F32)],
        compiler_params=_cparams(2),
        name="attn_dense",
    )(k, qt, vt)


def _window_attn_kernel(sink_ref, q_ref, k_ref, v_ref, o_ref, *, n_ctx):
    qi = pl.program_id(1)
    n_lat = k_ref.shape[0] - n_ctx
    span = TM + 2 * WINDOW
    n_blk = HQ // LANES

    @pl.when(qi < n_ctx // TM)
    def _():
        for j in range(n_blk):
            blk = slice(j * LANES, (j + 1) * LANES)
            o = _pair_attend(q_ref[:, blk], [(k_ref[0:n_ctx, :], v_ref[0:n_ctx, :], None)],
                             (sink_ref[GQA_HEAD_ORDER[2 * j]], sink_ref[GQA_HEAD_ORDER[2 * j + 1]]))
            o_ref[:, blk] = o.astype(o_ref.dtype)

    @pl.when(qi >= n_ctx // TM)
    def _():
        q0 = (qi - n_ctx // TM) * TM
        start = pl.multiple_of(jnp.clip(q0 - WINDOW, 0, n_lat - span), WINDOW)
        dist = (lax.broadcasted_iota(jnp.int32, (2 * TM, span), 0)
                - lax.broadcasted_iota(jnp.int32, (2 * TM, span), 1) + (q0 - start))
        dist = jnp.where(lax.broadcasted_iota(jnp.int32, (2 * TM, span), 0) >= TM, dist - TM, dist)
        bias = jnp.where(jnp.abs(dist) <= WINDOW, 0.0, NEG)
        rows = pl.ds(n_ctx + start, span)
        for j in range(n_blk):
            blk = slice(j * LANES, (j + 1) * LANES)
            o = _pair_attend(q_ref[:, blk],
                             [(k_ref[rows, :], v_ref[rows, :], bias),
                              (k_ref[0:n_ctx, :], v_ref[0:n_ctx, :], None)],
                             (sink_ref[GQA_HEAD_ORDER[2 * j]], sink_ref[GQA_HEAD_ORDER[2 * j + 1]]))
            o_ref[:, blk] = o.astype(o_ref.dtype)


def _nbr_attn_kernel(q_ref, k_ref, v_ref, tb_ref, o_ref, *, n_ctx):
    qi = pl.program_id(1)
    n_lat = k_ref.shape[0] - n_ctx
    grid_rows = n_lat // GRID_W
    rows_per_tile = TM // GRID_W
    span = KH * GRID_W
    n_blk = HQ // LANES

    @pl.when(qi < n_ctx // TM)
    def _():
        for j in range(n_blk):
            blk = slice(j * LANES, (j + 1) * LANES)
            o = _pair_attend(q_ref[:, blk], [(k_ref[0:n_ctx, blk], v_ref[0:n_ctx, blk], None)])
            o_ref[:, blk] = o.astype(o_ref.dtype)

    @pl.when(qi >= n_ctx // TM)
    def _():
        def row_body(rr, carry):
            r = (qi - n_ctx // TM) * rows_per_tile + rr
            r0 = jnp.clip(r - KH // 2, 0, grid_rows - KH)
            delta = r - r0
            krows = pl.ds(pl.multiple_of(n_ctx + r0 * GRID_W, GRID_W), span)
            qrows = pl.ds(pl.multiple_of(rr * GRID_W, GRID_W), GRID_W)
            for j in range(n_blk):
                blk = slice(j * LANES, (j + 1) * LANES)
                bias = jnp.concatenate(
                    [tb_ref[j, 2 * jj - delta + (KH - 1)] for jj in range(KH // 2)], axis=1)
                o = _pair_attend(q_ref[qrows, blk],
                                 [(k_ref[krows, blk], v_ref[krows, blk], bias),
                                  (k_ref[0:n_ctx, blk], v_ref[0:n_ctx, blk], None)])
                o_ref[qrows, blk] = o.astype(o_ref.dtype)
            return carry

        lax.fori_loop(0, rows_per_tile, row_body, 0)


def _pair_attention(kind, p, n_ctx, *, k_w, k_blk, sink=None, bias_tab=None):
    b, s, _ = p.shape
    in_specs = [
        pl.BlockSpec((None, TM, HQ), lambda bi, qi: (bi, qi, 0)),
        pl.BlockSpec((None, s, k_w), lambda bi, qi: (bi, 0, k_blk)),
        pl.BlockSpec((None, s, k_w), lambda bi, qi: (bi, 0, k_blk + 1)),
    ]
    args = [p, p, p]
    if kind == "window":
        body = functools.partial(_window_attn_kernel, n_ctx=n_ctx)
        in_specs = [pl.BlockSpec(memory_space=pltpu.SMEM)] + in_specs
        args = [sink] + args
    else:
        body = functools.partial(_nbr_attn_kernel, n_ctx=n_ctx)
        in_specs = in_specs + [_resident(bias_tab.shape)]
        args = args + [bias_tab]
    return pl.pallas_call(
        body,
        grid=(b, s // TM),
        in_specs=in_specs,
        out_specs=pl.BlockSpec((None, TM, HQ), lambda bi, qi: (bi, qi, 0)),
        out_shape=jax.ShapeDtypeStruct((b, s, HQ), BF16),
        compiler_params=_cparams(2),
        name="attn_" + kind,
    )(*args)


def _merge_kernel(oa_ref, ob_ref, oc_ref, od_ref, h_ref, mod_ref, gpre_ref, gpost_ref,
                  wg_ref, wbr_ref, wout_ref, o_ref, acc_ref):
    d_model = h_ref.shape[1]
    cw = 256
    h = h_ref[...]
    sh1 = mod_ref[0:1, :]
    sc1 = mod_ref[1:2, :]
    ga1 = mod_ref[2:3, :]
    u = (_rms(h, gpre_ref[...]) * (1.0 + sc1) + sh1).astype(BF16)
    branches = (oa_ref, ob_ref, oc_ref, od_ref)
    for c in range(d_model // cw):
        acc = None
        for n, br in enumerate(branches):
            lo = n * d_model + c * cw
            gate = _dot(u, wg_ref[:, lo:lo + cw])
            up = _dot(br[...], wbr_ref[n, :, c * cw:(c + 1) * cw])
            term = _sigmoid(gate) * up
            acc = term if acc is None else acc + term
        acc_ref[:, c * cw:(c + 1) * cw] = acc.astype(BF16)
    y = _dot(acc_ref[...], wout_ref[...])
    o_ref[...] = h + ga1 * _rms(y, gpost_ref[...])


def _merge(o_a, o_b, o_c, o_d, h, mod, g_pre, g_post, wg, wbr, wout, tiles_per_batch):
    m, d = h.shape
    row = lambda i: (i, 0)
    return pl.pallas_call(
        _merge_kernel,
        grid=(m // TM,),
        in_specs=[pl.BlockSpec((TM, HQ), row)] * N_BRANCH + [
            pl.BlockSpec((TM, d), row),
            _mod_spec(d, tiles_per_batch),
            _resident((1, d)), _resident((1, d)),
            _resident(wg.shape), _resident(wbr.shape), _resident(wout.shape),
        ],
        out_specs=pl.BlockSpec((TM, d), row),
        out_shape=jax.ShapeDtypeStruct((m, d), F32),
        scratch_shapes=[pltpu.VMEM((TM, d), BF16)],
        input_output_aliases={4: 0},
        compiler_params=_cparams(1),
        name="merge",
    )(o_a, o_b, o_c, o_d, h, mod, g_pre, g_post, wg, wbr, wout)


def _ffn_kernel(h_ref, mod_ref, gpre_ref, gpost_ref, w1_ref, w2_ref, o_ref, act_ref):
    d_ff = w2_ref.shape[0]
    cw = 256
    h = h_ref[...]
    sh2 = mod_ref[3:4, :]
    sc2 = mod_ref[4:5, :]
    ga2 = mod_ref[5:6, :]
    u = (_rms(h, gpre_ref[...]) * (1.0 + sc2) + sh2).astype(BF16)
    for c in range(d_ff // cw):
        a = _dot(u, w1_ref[:, c * cw:(c + 1) * cw])
        b = _dot(u, w1_ref[:, d_ff + c * cw:d_ff + (c + 1) * cw])
        act_ref[:, c * cw:(c + 1) * cw] = (a * _sigmoid(a) * b).astype(BF16)
    y = _dot(act_ref[...], w2_ref[...])
    o_ref[...] = h + ga2 * _rms(y, gpost_ref[...])


def _ffn(h, mod, g_pre, g_post, w1, w2, tiles_per_batch):
    m, d = h.shape
    row = lambda i: (i, 0)
    return pl.pallas_call(
        _ffn_kernel,
        grid=(m // TM,),
        in_specs=[
            pl.BlockSpec((TM, d), row),
            _mod_spec(d, tiles_per_batch),
            _resident((1, d)), _resident((1, d)), _resident(w1.shape), _resident(w2.shape),
        ],
        out_specs=pl.BlockSpec((TM, d), row),
        out_shape=jax.ShapeDtypeStruct((m, d), F32),
        scratch_shapes=[pltpu.VMEM((TM, w2.shape[0]), BF16)],
        input_output_aliases={0: 0},
        compiler_params=_cparams(1),
        name="ffn",
    )(h, mod, g_pre, g_post, w1, w2)


def _rope_tables(n_ctx, n_lat):
    t = jnp.arange(n_lat)
    rows_pos = (t // GRID_W).astype(F32)
    cols_pos = (t % GRID_W).astype(F32)

    def block(d):
        q = d // 4
        inv = ROPE_THETA ** (-jnp.arange(q, dtype=F32) * 2.0 / (d // 2))
        ar = rows_pos[:, None] * inv[None, :]
        ac = cols_pos[:, None] * inv[None, :]
        z = jnp.zeros_like(ar)
        cos = jnp.concatenate([jnp.cos(ar), jnp.cos(ar), jnp.cos(ac), jnp.cos(ac)], axis=1)
        up = jnp.concatenate([-jnp.sin(ar), z, -jnp.sin(ac), z], axis=1)
        dn = jnp.concatenate([z, jnp.sin(ar), z, jnp.sin(ac)], axis=1)
        return cos, up, dn

    def with_ctx(tab, fill):
        return jnp.concatenate([jnp.full((n_ctx, tab.shape[1]), fill, F32), tab], axis=0)

    c64, u64, d64 = block(DH)
    rope_64 = jnp.stack([with_ctx(jnp.tile(c64, (1, LANES // DH)), 1.0),
                         with_ctx(jnp.tile(u64, (1, LANES // DH)), 0.0),
                         with_ctx(jnp.tile(d64, (1, LANES // DH)), 0.0)])
    ca, ua, da = block(ROPE_A)
    pad_lo = NOPE_A
    pad_hi = HEAD_PAD_A - NOPE_A - ROPE_A

    def widen(tab, fill):
        return jnp.concatenate([jnp.full((n_lat, pad_lo), fill, F32), tab,
                                jnp.full((n_lat, pad_hi), fill, F32)], axis=1)

    rope_a = jnp.stack([with_ctx(widen(ca, 1.0), 1.0), with_ctx(widen(ua, 0.0), 0.0),
                        with_ctx(widen(da, 0.0), 0.0)])
    return {"rope_a": rope_a, "rope_64": rope_64}


def _permute_heads(w, axis, order):
    shape = w.shape
    w = w.reshape(shape[:axis] + (N_HEADS, DH) + shape[axis + 1:])
    w = jnp.take(w, jnp.asarray(order), axis=axis)
    return w.reshape(shape)


def _layer_weights(l, w_in, g_a_q, g_a_kv, w_a_uq, w_a_ukv, g_d_q, g_d_k, w_branch):
    d = w_in.shape[1]
    wl = w_in[l]
    o = 0
    a_cq = wl[:, o:o + Q_LORA]; o += Q_LORA
    a_ckv = wl[:, o:o + KV_LORA]; o += KV_LORA
    a_kr = wl[:, o:o + ROPE_A]; o += ROPE_A
    w_bq = wl[:, o:o + HQ]; o += HQ
    w_bkv = wl[:, o:o + 2 * KV_B * DH]; o += 2 * KV_B * DH
    w_c = wl[:, o:o + 3 * HQ]; o += 3 * HQ
    w_d = wl[:, o:o + HQ + 2 * KV_D * DH]; o += HQ + 2 * KV_D * DH
    w_g = wl[:, o:]
    zeros = lambda n: jnp.zeros((d, n), wl.dtype)
    wa = jnp.concatenate([a_cq, a_ckv, zeros(NOPE_A), a_kr, zeros(HEAD_PAD_A - NOPE_A - ROPE_A)], axis=1)
    w_b = jnp.concatenate([_permute_heads(w_bq, 1, GQA_HEAD_ORDER), w_bkv], axis=1)
    uq = w_a_uq[l].reshape(Q_LORA, N_HEADS, NOPE_A + ROPE_A)
    uq = jnp.pad(uq, ((0, 0), (0, 0), (0, HEAD_PAD_A - NOPE_A - ROPE_A))).reshape(Q_LORA, N_HEADS * HEAD_PAD_A)
    ukv = w_a_ukv[l].reshape(KV_LORA, N_HEADS, NOPE_A + DH)
    wk = jnp.pad(ukv[:, :, :NOPE_A], ((0, 0), (0, 0), (0, HEAD_PAD_A - NOPE_A))).reshape(KV_LORA, N_HEADS * HEAD_PAD_A)
    wv = ukv[:, :, NOPE_A:].reshape(KV_LORA, HQ)
    wbr = w_branch[l]
    wbr = jnp.stack([wbr[0], _permute_heads(wbr[1], 0, GQA_HEAD_ORDER), wbr[2], wbr[3]])
    cast = lambda x: x.astype(BF16)
    return {
        "wa": cast(wa), "wb": cast(w_b), "wc": cast(w_c), "wd": cast(w_d), "wg": cast(w_g),
        "wuq": cast(uq), "wk": cast(wk), "wv": cast(wv), "wbr": cast(wbr),
        "gaq": g_a_q[l][None, :], "gakv": g_a_kv[l][None, :],
        "gdq": jnp.tile(g_d_q[l], LANES // DH)[None, :], "gdk": jnp.tile(g_d_k[l], LANES // DH)[None, :],
    }


def kernel(x, c, ctx, c_ctx, w_mod, b_mod, g_pre_mix, g_post_mix, g_pre_ffn, g_post_ffn,
           w_in, g_a_q, g_a_kv, w_a_uq, w_a_ukv, sink_b, rpb_c, g_d_q, g_d_k,
           w_branch, w_out, w_ffn_in, w_ffn_out):
    b, n_lat, d = x.shape
    n_ctx = ctx.shape[1]
    s = n_ctx + n_lat
    depth = w_mod.shape[0]
    assert n_ctx == TM and n_lat % TM == 0 and n_lat >= TM + 2 * WINDOW
    assert n_lat % GRID_W == 0 and n_lat // GRID_W >= KH
    assert KV_B * DH == LANES and KV_D * DH == LANES
    tiles_per_batch = s // TM

    n_rows = -(-(b + 1) // F32_SUBLANES) * F32_SUBLANES
    cc = jnp.zeros((n_rows, d), F32).at[:b].set(c).at[b].set(c_ctx)
    mods = _modulation(cc, w_mod, b_mod)
    mod_x = mods[:, :b].reshape(depth, b, 1, MOD_CHUNKS, d)
    mod_c = jnp.broadcast_to(mods[:, b].reshape(depth, 1, 1, MOD_CHUNKS, d), mod_x.shape)
    mod_tab = jnp.concatenate([mod_c, mod_x], axis=2)

    tabs = _rope_tables(n_ctx, n_lat)
    h = jnp.concatenate([ctx, x], axis=1).reshape(b * s, d)

    for l in range(depth):
        w = _layer_weights(l, w_in, g_a_q, g_a_kv, w_a_uq, w_a_ukv, g_d_q, g_d_k, w_branch)
        mod = mod_tab[l]
        g_pre = g_pre_mix[l][None, :]
        ka, qat, vat, pb, pc, kd, qdt, vdt = _project(h, mod, g_pre, w, tabs, b, tiles_per_batch)
        o_a = _dense_attention(ka.reshape(b, s, -1), qat, vat, n_ctx, HEAD_PAD_A)
        o_b = _pair_attention("window", pb.reshape(b, s, -1), n_ctx, k_w=LANES, k_blk=HQ // LANES,
                              sink=sink_b[l])
        o_c = _pair_attention("nbr", pc.reshape(b, s, -1), n_ctx, k_w=HQ, k_blk=1,
                              bias_tab=_nbr_bias(rpb_c[l]))
        o_d = _dense_attention(kd.reshape(b, s, -1), qdt, vdt, n_ctx, DH)
        flat = lambda t: t.reshape(b * s, HQ)
        h = _merge(flat(o_a), flat(o_b), flat(o_c), flat(o_d), h, mod, g_pre, g_post_mix[l][None, :],
                   w["wg"], w["wbr"], w_out[l].astype(BF16), tiles_per_batch)
        h = _ffn(h, mod, g_pre_ffn[l][None, :], g_post_ffn[l][None, :],
                 w_ffn_in[l].astype(BF16), w_ffn_out[l].astype(BF16), tiles_per_batch)
    return h.reshape(b, s, d)[:, n_ctx:, :]
```

```python
import functools

import jax
import jax.numpy as jnp
from jax import lax
from jax.experimental import pallas as pl
from jax.experimental.pallas import tpu as pltpu

F32 = jnp.float32
BF16 = jnp.bfloat16

GRID_W = 64
ROPE_THETA = 10000.0
EPS = 1e-6
NEG = -1e30
DH = 64
N_HEADS = 8
N_BRANCH = 4
Q_LORA = 256
KV_LORA = 128
NOPE_A = 64
ROPE_A = 32
KV_B = 2
WINDOW = 128
KH = 8
KW = 16
KV_D = 2
MOD_CHUNKS = 6

LANES = 128
F32_SUBLANES = 8
VMEM_LIMIT_BYTES = 56 * 1024 * 1024

TM = 256
HEAD_PAD_A = 128
HQ = N_HEADS * DH
S_A = (NOPE_A + ROPE_A) ** -0.5
S_H = DH ** -0.5
LOG2E = 1.4426950408889634
KEY_CHUNK = 768
GQA_HEAD_ORDER = tuple(h for j in range(N_HEADS // 2) for h in (j, j + N_HEADS // 2))


def _cparams(n_axes):
    return pltpu.CompilerParams(
        dimension_semantics=("arbitrary",) * n_axes, vmem_limit_bytes=VMEM_LIMIT_BYTES)


def _resident(shape):
    nd = len(shape)
    return pl.BlockSpec(shape, lambda *_: (0,) * nd, pipeline_mode=pl.Buffered(1))


def _row_tiles(m, tiles_per_batch, latent_only):
    if not latent_only:
        return (lambda i: (i, 0)), m // TM
    lat = tiles_per_batch - 1
    return (lambda i: ((i // lat) * tiles_per_batch + 1 + i % lat, 0)), m // TM // tiles_per_batch * lat


def _mod_spec(d, tiles_per_batch, latent_only=False):
    if latent_only:
        lat = tiles_per_batch - 1
        return pl.BlockSpec((None, None, MOD_CHUNKS, d), lambda i: (i // lat, 1, 0, 0))
    return pl.BlockSpec((None, None, MOD_CHUNKS, d),
                        lambda i: (i // tiles_per_batch, jnp.minimum(i % tiles_per_batch, 1), 0, 0))


def _dot(a, b):
    return jnp.dot(a, b, preferred_element_type=F32)


def _dot_nt(a, b):
    return lax.dot_general(a, b, (((1,), (1,)), ((), ())), preferred_element_type=F32)


def _rms(x, g):
    ms = jnp.mean(x * x, axis=-1, keepdims=True)
    return x * lax.rsqrt(ms + EPS) * g


def _sigmoid(x):
    return 1.0 / (1.0 + jnp.exp(-x))


def _rope(x, tab_ref, shift):
    n = x.shape[-1]
    return (x * tab_ref[0] + pltpu.roll(x, n - shift, 1) * tab_ref[1]
            + pltpu.roll(x, shift, 1) * tab_ref[2])


def _head_rms(x, g, bd):
    x2 = x * x
    hi = x2.astype(BF16)
    lo = (x2 - hi.astype(F32)).astype(BF16)
    ss = _dot(hi, bd) + _dot(lo, bd)
    return x * lax.rsqrt(ss * (1.0 / DH) + EPS) * g


def _softmax_pv(parts, sink=None):
    m = None
    for s, _ in parts:
        mi = jnp.max(s, axis=1, keepdims=True)
        m = mi if m is None else jnp.maximum(m, mi)
    if sink is not None:
        m = jnp.maximum(m, sink)
    l = None
    acc = None
    for s, v in parts:
        p = jnp.exp2(s - m)
        li = jnp.sum(p, axis=1, keepdims=True)
        ai = _dot(p.astype(BF16), v)
        l = li if l is None else l + li
        acc = ai if acc is None else acc + ai
    if sink is not None:
        l = l + jnp.exp2(sink - m)
    return acc / l


def _pair_scores(q_blk, parts):
    m = q_blk.shape[0]
    lane = lax.broadcasted_iota(jnp.int32, (m, LANES), 1)
    zero = jnp.zeros_like(q_blk)
    qs = jnp.concatenate([jnp.where(lane < DH, q_blk, zero), jnp.where(lane >= DH, q_blk, zero)], axis=0)
    scored = []
    for k, v, bias in parts:
        s = _dot_nt(qs, k)
        scored.append((s if bias is None else s + bias, v))
    return scored


def _pair_finish(scored, sinks=None):
    m = scored[0][0].shape[0] // 2
    sink = None
    if sinks is not None:
        row = lax.broadcasted_iota(jnp.int32, (2 * m, 1), 0)
        sink = jnp.where(row < m, sinks[0], sinks[1]) * LOG2E
    o = _softmax_pv(scored, sink)
    lane = lax.broadcasted_iota(jnp.int32, (m, LANES), 1)
    return jnp.where(lane < DH, o[:m], o[m:])


def _pipelined(items, scores_fn, finish_fn):
    cur = scores_fn(items[0])
    for n, item in enumerate(items):
        nxt = scores_fn(items[n + 1]) if n + 1 < len(items) else None
        finish_fn(item, cur)
        cur = nxt


def _mod_kernel(c_ref, w_ref, b_ref, o_ref):
    c = c_ref[...]
    sc = (c * _sigmoid(c)).astype(BF16)
    o_ref[...] = _dot(sc, w_ref[...].astype(BF16)) + b_ref[...]


def _modulation(cc, w_mod, b_mod):
    n_layers, d, n = w_mod.shape
    r = cc.shape[0]
    tn = n // 4
    return pl.pallas_call(
        _mod_kernel,
        grid=(n_layers, n // tn),
        in_specs=[
            pl.BlockSpec((r, d), lambda l, j: (0, 0)),
            pl.BlockSpec((None, d, tn), lambda l, j: (l, 0, j)),
            pl.BlockSpec((None, 1, tn), lambda l, j: (l, 0, j)),
        ],
        out_specs=pl.BlockSpec((None, r, tn), lambda l, j: (l, 0, j)),
        out_shape=jax.ShapeDtypeStruct((n_layers, r, n), F32),
        compiler_params=_cparams(2),
        name="modulation",
    )(cc, w_mod, b_mod.reshape(n_layers, 1, n))


def _nbr_bias_kernel(rpb_ref, o_ref):
    pair = pl.program_id(0)
    n_dr = 2 * KH - 1
    n_dc = 2 * KW - 1
    qc = lax.broadcasted_iota(jnp.int32, (GRID_W, 2 * GRID_W), 0)
    lane = lax.broadcasted_iota(jnp.int32, (GRID_W, 2 * GRID_W), 1)
    second = lane >= GRID_W
    kc = jnp.where(second, lane - GRID_W, lane)
    c0 = jnp.clip(qc - KW // 2, 0, GRID_W - KW)
    ok = (kc >= c0) & (kc < c0 + KW)
    dc = kc - qc + (KW - 1)

    for e in range(2):
        h = 2 * pair + e

        def d2_body(d2, carry, h=h, e=e):
            def d_body(d, acc):
                v0 = rpb_ref[(h * n_dr + d2) * n_dc + d]
                v1 = rpb_ref[(h * n_dr + d2 + 1) * n_dc + d]
                return jnp.where(dc == d, jnp.where(second, v1, v0), acc)

            acc = lax.fori_loop(0, n_dc, d_body, jnp.zeros((GRID_W, 2 * GRID_W), F32))
            o_ref[d2, e * GRID_W:(e + 1) * GRID_W, :] = jnp.where(ok, acc * LOG2E, NEG)
            return carry

        lax.fori_loop(0, n_dr - 1, d2_body, 0)


def _nbr_bias(rpb):
    h = rpb.shape[0]
    n_pairs = 2 * KH - 2
    return pl.pallas_call(
        _nbr_bias_kernel,
        grid=(h // 2,),
        in_specs=[pl.BlockSpec(memory_space=pltpu.SMEM)],
        out_specs=pl.BlockSpec((None, n_pairs, 2 * GRID_W, 2 * GRID_W), lambda i: (i, 0, 0, 0)),
        out_shape=jax.ShapeDtypeStruct((h // 2, n_pairs, 2 * GRID_W, 2 * GRID_W), F32),
        compiler_params=_cparams(1),
        name="nbr_bias",
    )(rpb.reshape(-1))


def _proj_kernel(h_ref, mod_ref, g_ref, wa_ref, wb_ref, wc_ref, wd_ref, wuq_ref, wk_ref, wv_ref,
                 gaq_ref, gakv_ref, gdq_ref, gdk_ref, ropea_ref, rope64_ref,
                 ka_ref, qat_ref, vat_ref, pb_ref, pc_ref, kd_ref, qdt_ref, vdt_ref):
    sh1 = mod_ref[0:1, :]
    sc1 = mod_ref[1:2, :]
    u = (_rms(h_ref[...], g_ref[...]) * (1.0 + sc1) + sh1).astype(BF16)

    rb = jnp.where(lax.broadcasted_iota(jnp.int32, (LANES, LANES), 0) < DH, 0, 1)
    cb = jnp.where(lax.broadcasted_iota(jnp.int32, (LANES, LANES), 1) < DH, 0, 1)
    bd = jnp.where(rb == cb, 1.0, 0.0).astype(BF16)

    ya = _dot(u, wa_ref[...])
    cq = _rms(ya[:, :Q_LORA], gaq_ref[...]).astype(BF16)
    ckv = _rms(ya[:, Q_LORA:Q_LORA + KV_LORA], gakv_ref[...]).astype(BF16)
    kr = _rope(ya[:, Q_LORA + KV_LORA:], ropea_ref, ROPE_A // 4)
    qa = _dot(cq, wuq_ref[...])
    ka = _dot(ckv, wk_ref[...])
    va = _dot(ckv, wv_ref[...])
    for hh in range(N_HEADS):
        lo, hi = hh * HEAD_PAD_A, (hh + 1) * HEAD_PAD_A
        q = _rope(qa[:, lo:hi], ropea_ref, ROPE_A // 4) * (S_A * LOG2E)
        qat_ref[lo:hi, :] = q.T.astype(BF16)
        ka_ref[:, lo:hi] = (ka[:, lo:hi] + kr).astype(BF16)
    for c in range(HQ // LANES):
        lo, hi = c * LANES, (c + 1) * LANES
        vat_ref[lo:hi, :] = va[:, lo:hi].T.astype(BF16)

    yb = _dot(u, wb_ref[...])
    for c in range(HQ // LANES):
        lo, hi = c * LANES, (c + 1) * LANES
        pb_ref[:, lo:hi] = (_rope(yb[:, lo:hi], rope64_ref, DH // 4) * (S_H * LOG2E)).astype(BF16)
    pb_ref[:, HQ:HQ + LANES] = _rope(yb[:, HQ:HQ + LANES], rope64_ref, DH // 4).astype(BF16)
    pb_ref[:, HQ + LANES:] = yb[:, HQ + LANES:].astype(BF16)

    yc = _dot(u, wc_ref[...])
    pc_ref[:, :HQ] = (yc[:, :HQ] * (S_H * LOG2E)).astype(BF16)
    pc_ref[:, HQ:] = yc[:, HQ:].astype(BF16)

    yd = _dot(u, wd_ref[...])
    for c in range(HQ // LANES):
        lo, hi = c * LANES, (c + 1) * LANES
        qn = _head_rms(yd[:, lo:hi], gdq_ref[...], bd)
        qdt_ref[lo:hi, :] = (_rope(qn, rope64_ref, DH // 4) * (S_H * LOG2E)).T.astype(BF16)
    kn = _head_rms(yd[:, HQ:HQ + LANES], gdk_ref[...], bd)
    kd_ref[...] = _rope(kn, rope64_ref, DH // 4).astype(BF16)
    vdt_ref[...] = yd[:, HQ + LANES:].T.astype(BF16)


def _project(h, mod, g_pre, w, tabs, n_batch, tiles_per_batch):
    m, d = h.shape
    s = tiles_per_batch * TM
    wq_a = N_HEADS * HEAD_PAD_A
    row = lambda i: (i, 0)
    col = lambda i: (i // tiles_per_batch, 0, i % tiles_per_batch)
    tile_pos = lambda i: (0, i % tiles_per_batch, 0)
    in_specs = [
        pl.BlockSpec((TM, d), row),
        _mod_spec(d, tiles_per_batch),
        _resident((1, d)),
        _resident(w["wa"].shape), _resident(w["wb"].shape), _resident(w["wc"].shape),
        _resident(w["wd"].shape), _resident(w["wuq"].shape),
        _resident(w["wk"].shape), _resident(w["wv"].shape),
        _resident((1, Q_LORA)), _resident((1, KV_LORA)), _resident((1, LANES)), _resident((1, LANES)),
        pl.BlockSpec((3, TM, LANES), tile_pos),
        pl.BlockSpec((3, TM, LANES), tile_pos),
    ]

    def rows_out(c):
        return pl.BlockSpec((TM, c), row), jax.ShapeDtypeStruct((m, c), BF16)

    def cols_out(r):
        return pl.BlockSpec((None, r, TM), col), jax.ShapeDtypeStruct((n_batch, r, s), BF16)

    outs = [rows_out(wq_a), cols_out(wq_a), cols_out(HQ),
            rows_out(HQ + 2 * LANES), rows_out(3 * HQ),
            rows_out(LANES), cols_out(HQ), cols_out(LANES)]
    return pl.pallas_call(
        _proj_kernel,
        grid=(m // TM,),
        in_specs=in_specs,
        out_specs=[o[0] for o in outs],
        out_shape=[o[1] for o in outs],
        compiler_params=_cparams(1),
        name="proj",
    )(h, mod, g_pre, w["wa"], w["wb"], w["wc"], w["wd"], w["wuq"], w["wk"], w["wv"],
      w["gaq"], w["gakv"], w["gdq"], w["gdk"], tabs["rope_a"], tabs["rope_64"])


def _dense_attn_kernel(k_ref, qt_ref, vt_ref, o_ref, ot_ref, *, n_ctx, dk, q_first):
    qi = pl.program_id(1) + q_first
    group = 1 if dk == LANES else N_HEADS // (LANES // DH)

    def attend(n_keys):
        n_chunks = -(-n_keys // KEY_CHUNK)
        bounds = [(i * n_keys // n_chunks, (i + 1) * n_keys // n_chunks) for i in range(n_chunks)]

        def q_operand(h):
            if dk == LANES:
                return qt_ref[h * LANES:(h + 1) * LANES, :]
            qh = qt_ref[h * DH:(h + 1) * DH, :]
            z = jnp.zeros_like(qh)
            return jnp.concatenate([qh, z] if h // group == 0 else [z, qh], axis=0)

        def scores(h, qt, i):
            lanes = slice(h * LANES, (h + 1) * LANES) if dk == LANES else slice(0, LANES)
            return _dot(k_ref[bounds[i][0]:bounds[i][1], lanes], qt)

        def colmax(parts):
            m = None
            for st in parts:
                mi = jnp.max(st, axis=0, keepdims=True)
                m = mi if m is None else jnp.maximum(m, mi)
            return m

        qt = q_operand(0)
        cur = [scores(0, qt, i) for i in range(n_chunks)]
        for h in range(N_HEADS):
            g = h // group
            m = colmax(cur)
            qt = q_operand(h + 1) if h + 1 < N_HEADS else None
            nxt = []
            l = acc = None
            for i in range(n_chunks):
                if qt is not None:
                    nxt.append(scores(h + 1, qt, i))
                lo, hi = bounds[i]
                p = jnp.exp2(cur[i] - m)
                li = jnp.sum(p, axis=0, keepdims=True)
                ai = _dot(vt_ref[g * DH:(g + 1) * DH, lo:hi], p.astype(BF16))
                l = li if l is None else l + li
                acc = ai if acc is None else acc + ai
            ot_ref[h * DH:(h + 1) * DH, :] = acc / l
            cur = nxt
        o_ref[...] = ot_ref[...].T.astype(o_ref.dtype)

    @pl.when(qi < n_ctx // TM)
    def _():
        attend(n_ctx)

    @pl.when(qi >= n_ctx // TM)
    def _():
        attend(k_ref.shape[0])


def _dense_attention(k, qt, vt, n_ctx, dk, q_first):
    b, s, kw = k.shape
    return pl.pallas_call(
        functools.partial(_dense_attn_kernel, n_ctx=n_ctx, dk=dk, q_first=q_first),
        grid=(b, s // TM - q_first),
        in_specs=[
            pl.BlockSpec((None, s, kw), lambda bi, qi: (bi, 0, 0)),
            pl.BlockSpec((None, qt.shape[1], TM), lambda bi, qi: (bi, 0, qi + q_first)),
            pl.BlockSpec((None, vt.shape[1], s), lambda bi, qi: (bi, 0, 0)),
        ],
        out_specs=pl.BlockSpec((None, TM, HQ), lambda bi, qi: (bi, qi + q_first, 0)),
        out_shape=jax.ShapeDtypeStruct((b, s, HQ), BF16),
        scratch_shapes=[pltpu.VMEM((HQ, TM), F32)],
        compiler_params=_cparams(2),
        name="attn_dense",
    )(k, qt, vt)


def _window_attn_kernel(sink_ref, q_ref, k_ref, v_ref, o_ref, *, n_ctx, q_first):
    qi = pl.program_id(1) + q_first
    n_lat = k_ref.shape[0] - n_ctx
    span = TM + 2 * WINDOW
    blocks = list(range(HQ // LANES))

    def sinks(j):
        return sink_ref[GQA_HEAD_ORDER[2 * j]], sink_ref[GQA_HEAD_ORDER[2 * j + 1]]

    def finish(j, scored):
        o_ref[:, j * LANES:(j + 1) * LANES] = _pair_finish(scored, sinks(j)).astype(o_ref.dtype)

    @pl.when(qi < n_ctx // TM)
    def _():
        def scores(j):
            return _pair_scores(q_ref[:, j * LANES:(j + 1) * LANES],
                                [(k_ref[0:n_ctx, :], v_ref[0:n_ctx, :], None)])

        _pipelined(blocks, scores, finish)

    @pl.when(qi >= n_ctx // TM)
    def _():
        q0 = (qi - n_ctx // TM) * TM
        start = pl.multiple_of(jnp.clip(q0 - WINDOW, 0, n_lat - span), WINDOW)
        dist = (lax.broadcasted_iota(jnp.int32, (2 * TM, span), 0)
                - lax.broadcasted_iota(jnp.int32, (2 * TM, span), 1) + (q0 - start))
        dist = jnp.where(lax.broadcasted_iota(jnp.int32, (2 * TM, span), 0) >= TM, dist - TM, dist)
        bias = jnp.where(jnp.abs(dist) <= WINDOW, 0.0, NEG)
        rows = pl.ds(n_ctx + start, span)

        def scores(j):
            return _pair_scores(q_ref[:, j * LANES:(j + 1) * LANES],
                                [(k_ref[rows, :], v_ref[rows, :], bias),
                                 (k_ref[0:n_ctx, :], v_ref[0:n_ctx, :], None)])

        _pipelined(blocks, scores, finish)


def _nbr_attn_kernel(q_ref, k_ref, v_ref, tb_ref, o_ref, *, n_ctx, q_first):
    qi = pl.program_id(1) + q_first
    n_lat = k_ref.shape[0] - n_ctx
    grid_rows = n_lat // GRID_W
    rows_per_tile = TM // GRID_W
    span = KH * GRID_W
    n_blk = HQ // LANES

    @pl.when(qi < n_ctx // TM)
    def _():
        def scores(j):
            blk = slice(j * LANES, (j + 1) * LANES)
            return _pair_scores(q_ref[:, blk], [(k_ref[0:n_ctx, blk], v_ref[0:n_ctx, blk], None)])

        def finish(j, scored):
            o_ref[:, j * LANES:(j + 1) * LANES] = _pair_finish(scored).astype(o_ref.dtype)

        _pipelined(list(range(n_blk)), scores, finish)

    @pl.when(qi >= n_ctx // TM)
    def _():
        def scores(item):
            rr, j = item
            r = (qi - n_ctx // TM) * rows_per_tile + rr
            r0 = jnp.clip(r - KH // 2, 0, grid_rows - KH)
            delta = r - r0
            krows = pl.ds(pl.multiple_of(n_ctx + r0 * GRID_W, GRID_W), span)
            blk = slice(j * LANES, (j + 1) * LANES)
            bias = jnp.concatenate(
                [tb_ref[j, 2 * jj - delta + (KH - 1)] for jj in range(KH // 2)], axis=1)
            return _pair_scores(q_ref[rr * GRID_W:(rr + 1) * GRID_W, blk],
                                [(k_ref[krows, blk], v_ref[krows, blk], bias),
                                 (k_ref[0:n_ctx, blk], v_ref[0:n_ctx, blk], None)])

        def finish(item, scored):
            rr, j = item
            o_ref[rr * GRID_W:(rr + 1) * GRID_W, j * LANES:(j + 1) * LANES] = (
                _pair_finish(scored).astype(o_ref.dtype))

        _pipelined([(rr, j) for rr in range(rows_per_tile) for j in range(n_blk)], scores, finish)


def _pair_attention(kind, p, n_ctx, q_first, *, k_w, k_blk, sink=None, bias_tab=None):
    b, s, _ = p.shape
    in_specs = [
        pl.BlockSpec((None, TM, HQ), lambda bi, qi: (bi, qi + q_first, 0)),
        pl.BlockSpec((None, s, k_w), lambda bi, qi: (bi, 0, k_blk)),
        pl.BlockSpec((None, s, k_w), lambda bi, qi: (bi, 0, k_blk + 1)),
    ]
    args = [p, p, p]
    if kind == "window":
        body = functools.partial(_window_attn_kernel, n_ctx=n_ctx, q_first=q_first)
        in_specs = [pl.BlockSpec(memory_space=pltpu.SMEM)] + in_specs
        args = [sink] + args
    else:
        body = functools.partial(_nbr_attn_kernel, n_ctx=n_ctx, q_first=q_first)
        in_specs = in_specs + [_resident(bias_tab.shape)]
        args = args + [bias_tab]
    return pl.pallas_call(
        body,
        grid=(b, s // TM - q_first),
        in_specs=in_specs,
        out_specs=pl.BlockSpec((None, TM, HQ), lambda bi, qi: (bi, qi + q_first, 0)),
        out_shape=jax.ShapeDtypeStruct((b, s, HQ), BF16),
        compiler_params=_cparams(2),
        name="attn_" + kind,
    )(*args)


def _merge_kernel(oa_ref, ob_ref, oc_ref, od_ref, h_ref, mod_ref, gpre_ref, gpost_ref,
                  wg_ref, wbr_ref, wout_ref, o_ref, acc_ref):
    d_model = h_ref.shape[1]
    cw = 256
    h = h_ref[...]
    sh1 = mod_ref[0:1, :]
    sc1 = mod_ref[1:2, :]
    ga1 = mod_ref[2:3, :]
    u = (_rms(h, gpre_ref[...]) * (1.0 + sc1) + sh1).astype(BF16)
    branches = (oa_ref, ob_ref, oc_ref, od_ref)
    for c in range(d_model // cw):
        acc = None
        for n, br in enumerate(branches):
            lo = n * d_model + c * cw
            gate = _dot(u, wg_ref[:, lo:lo + cw])
            up = _dot(br[...], wbr_ref[n, :, c * cw:(c + 1) * cw])
            term = _sigmoid(gate) * up
            acc = term if acc is None else acc + term
        acc_ref[:, c * cw:(c + 1) * cw] = acc.astype(BF16)
    y = _dot(acc_ref[...], wout_ref[...])
    o_ref[...] = h + ga1 * _rms(y, gpost_ref[...])


def _merge(o_a, o_b, o_c, o_d, h, mod, g_pre, g_post, wg, wbr, wout, tiles_per_batch, latent_only):
    m, d = h.shape
    row, n_tiles = _row_tiles(m, tiles_per_batch, latent_only)
    return pl.pallas_call(
        _merge_kernel,
        grid=(n_tiles,),
        in_specs=[pl.BlockSpec((TM, HQ), row)] * N_BRANCH + [
            pl.BlockSpec((TM, d), row),
            _mod_spec(d, tiles_per_batch, latent_only),
            _resident((1, d)), _resident((1, d)),
            _resident(wg.shape), _resident(wbr.shape), _resident(wout.shape),
        ],
        out_specs=pl.BlockSpec((TM, d), row),
        out_shape=jax.ShapeDtypeStruct((m, d), F32),
        scratch_shapes=[pltpu.VMEM((TM, d), BF16)],
        input_output_aliases={4: 0},
        compiler_params=_cparams(1),
        name="merge",
    )(o_a, o_b, o_c, o_d, h, mod, g_pre, g_post, wg, wbr, wout)


def _ffn_kernel(h_ref, mod_ref, gpre_ref, gpost_ref, w1_ref, w2_ref, o_ref, act_ref):
    d_ff = w2_ref.shape[0]
    cw = 256
    h = h_ref[...]
    sh2 = mod_ref[3:4, :]
    sc2 = mod_ref[4:5, :]
    ga2 = mod_ref[5:6, :]
    u = (_rms(h, gpre_ref[...]) * (1.0 + sc2) + sh2).astype(BF16)
    for c in range(d_ff // cw):
        a = _dot(u, w1_ref[:, c * cw:(c + 1) * cw])
        b = _dot(u, w1_ref[:, d_ff + c * cw:d_ff + (c + 1) * cw])
        act_ref[:, c * cw:(c + 1) * cw] = (a * _sigmoid(a) * b).astype(BF16)
    y = _dot(act_ref[...], w2_ref[...])
    o_ref[...] = h + ga2 * _rms(y, gpost_ref[...])


def _ffn(h, mod, g_pre, g_post, w1, w2, tiles_per_batch, latent_only):
    m, d = h.shape
    row, n_tiles = _row_tiles(m, tiles_per_batch, latent_only)
    return pl.pallas_call(
        _ffn_kernel,
        grid=(n_tiles,),
        in_specs=[
            pl.BlockSpec((TM, d), row),
            _mod_spec(d, tiles_per_batch, latent_only),
            _resident((1, d)), _resident((1, d)), _resident(w1.shape), _resident(w2.shape),
        ],
        out_specs=pl.BlockSpec((TM, d), lambda i: (i, 0)),
        out_shape=jax.ShapeDtypeStruct((n_tiles * TM, d), F32),
        scratch_shapes=[pltpu.VMEM((TM, w2.shape[0]), BF16)],
        input_output_aliases={} if latent_only else {0: 0},
        compiler_params=_cparams(1),
        name="ffn",
    )(h, mod, g_pre, g_post, w1, w2)


def _rope_tables(n_ctx, n_lat):
    t = jnp.arange(n_lat)
    rows_pos = (t // GRID_W).astype(F32)
    cols_pos = (t % GRID_W).astype(F32)

    def block(d):
        q = d // 4
        inv = ROPE_THETA ** (-jnp.arange(q, dtype=F32) * 2.0 / (d // 2))
        ar = rows_pos[:, None] * inv[None, :]
        ac = cols_pos[:, None] * inv[None, :]
        z = jnp.zeros_like(ar)
        cos = jnp.concatenate([jnp.cos(ar), jnp.cos(ar), jnp.cos(ac), jnp.cos(ac)], axis=1)
        up = jnp.concatenate([-jnp.sin(ar), z, -jnp.sin(ac), z], axis=1)
        dn = jnp.concatenate([z, jnp.sin(ar), z, jnp.sin(ac)], axis=1)
        return cos, up, dn

    def with_ctx(tab, fill):
        return jnp.concatenate([jnp.full((n_ctx, tab.shape[1]), fill, F32), tab], axis=0)

    c64, u64, d64 = block(DH)
    rope_64 = jnp.stack([with_ctx(jnp.tile(c64, (1, LANES // DH)), 1.0),
                         with_ctx(jnp.tile(u64, (1, LANES // DH)), 0.0),
                         with_ctx(jnp.tile(d64, (1, LANES // DH)), 0.0)])
    ca, ua, da = block(ROPE_A)
    pad_lo = NOPE_A
    pad_hi = HEAD_PAD_A - NOPE_A - ROPE_A

    def widen(tab, fill):
        return jnp.concatenate([jnp.full((n_lat, pad_lo), fill, F32), tab,
                                jnp.full((n_lat, pad_hi), fill, F32)], axis=1)

    rope_a = jnp.stack([with_ctx(widen(ca, 1.0), 1.0), with_ctx(widen(ua, 0.0), 0.0),
                        with_ctx(widen(da, 0.0), 0.0)])
    return {"rope_a": rope_a, "rope_64": rope_64}


def _permute_heads(w, axis, order):
    shape = w.shape
    w = w.reshape(shape[:axis] + (N_HEADS, DH) + shape[axis + 1:])
    w = jnp.take(w, jnp.asarray(order), axis=axis)
    return w.reshape(shape)


def _layer_weights(l, w_in, g_a_q, g_a_kv, w_a_uq, w_a_ukv, g_d_q, g_d_k, w_branch):
    d = w_in.shape[1]
    wl = w_in[l]
    o = 0
    a_cq = wl[:, o:o + Q_LORA]; o += Q_LORA
    a_ckv = wl[:, o:o + KV_LORA]; o += KV_LORA
    a_kr = wl[:, o:o + ROPE_A]; o += ROPE_A
    w_bq = wl[:, o:o + HQ]; o += HQ
    w_bkv = wl[:, o:o + 2 * KV_B * DH]; o += 2 * KV_B * DH
    w_c = wl[:, o:o + 3 * HQ]; o += 3 * HQ
    w_d = wl[:, o:o + HQ + 2 * KV_D * DH]; o += HQ + 2 * KV_D * DH
    w_g = wl[:, o:]
    zeros = lambda n: jnp.zeros((d, n), wl.dtype)
    wa = jnp.concatenate([a_cq, a_ckv, zeros(NOPE_A), a_kr, zeros(HEAD_PAD_A - NOPE_A - ROPE_A)], axis=1)
    w_b = jnp.concatenate([_permute_heads(w_bq, 1, GQA_HEAD_ORDER), w_bkv], axis=1)
    uq = w_a_uq[l].reshape(Q_LORA, N_HEADS, NOPE_A + ROPE_A)
    uq = jnp.pad(uq, ((0, 0), (0, 0), (0, HEAD_PAD_A - NOPE_A - ROPE_A))).reshape(Q_LORA, N_HEADS * HEAD_PAD_A)
    ukv = w_a_ukv[l].reshape(KV_LORA, N_HEADS, NOPE_A + DH)
    wk = jnp.pad(ukv[:, :, :NOPE_A], ((0, 0), (0, 0), (0, HEAD_PAD_A - NOPE_A))).reshape(KV_LORA, N_HEADS * HEAD_PAD_A)
    wv = ukv[:, :, NOPE_A:].reshape(KV_LORA, HQ)
    wbr = w_branch[l]
    wbr = jnp.stack([wbr[0], _permute_heads(wbr[1], 0, GQA_HEAD_ORDER), wbr[2], wbr[3]])
    cast = lambda x: x.astype(BF16)
    return {
        "wa": cast(wa), "wb": cast(w_b), "wc": cast(w_c), "wd": cast(w_d), "wg": cast(w_g),
        "wuq": cast(uq), "wk": cast(wk), "wv": cast(wv), "wbr": cast(wbr),
        "gaq": g_a_q[l][None, :], "gakv": g_a_kv[l][None, :],
        "gdq": jnp.tile(g_d_q[l], LANES // DH)[None, :], "gdk": jnp.tile(g_d_k[l], LANES // DH)[None, :],
    }


def kernel(x, c, ctx, c_ctx, w_mod, b_mod, g_pre_mix, g_post_mix, g_pre_ffn, g_post_ffn,
           w_in, g_a_q, g_a_kv, w_a_uq, w_a_ukv, sink_b, rpb_c, g_d_q, g_d_k,
           w_branch, w_out, w_ffn_in, w_ffn_out):
    b, n_lat, d = x.shape
    n_ctx = ctx.shape[1]
    s = n_ctx + n_lat
    depth = w_mod.shape[0]
    assert n_ctx == TM and n_lat % TM == 0 and n_lat >= TM + 2 * WINDOW
    assert n_lat % GRID_W == 0 and n_lat // GRID_W >= KH
    assert KV_B * DH == LANES and KV_D * DH == LANES
    tiles_per_batch = s // TM

    n_rows = -(-(b + 1) // F32_SUBLANES) * F32_SUBLANES
    cc = jnp.zeros((n_rows, d), F32).at[:b].set(c).at[b].set(c_ctx)
    mods = _modulation(cc, w_mod, b_mod)
    mod_x = mods[:, :b].reshape(depth, b, 1, MOD_CHUNKS, d)
    mod_c = jnp.broadcast_to(mods[:, b].reshape(depth, 1, 1, MOD_CHUNKS, d), mod_x.shape)
    mod_tab = jnp.concatenate([mod_c, mod_x], axis=2)

    tabs = _rope_tables(n_ctx, n_lat)
    h = jnp.concatenate([ctx, x], axis=1).reshape(b * s, d)

    for l in range(depth):
        w = _layer_weights(l, w_in, g_a_q, g_a_kv, w_a_uq, w_a_ukv, g_d_q, g_d_k, w_branch)
        mod = mod_tab[l]
        g_pre = g_pre_mix[l][None, :]
        last = l == depth - 1
        q_first = n_ctx // TM if last else 0
        ka, qat, vat, pb, pc, kd, qdt, vdt = _project(h, mod, g_pre, w, tabs, b, tiles_per_batch)
        o_a = _dense_attention(ka.reshape(b, s, -1), qat, vat, n_ctx, HEAD_PAD_A, q_first)
        o_b = _pair_attention("window", pb.reshape(b, s, -1), n_ctx, q_first, k_w=LANES, k_blk=HQ // LANES,
                              sink=sink_b[l])
        o_c = _pair_attention("nbr", pc.reshape(b, s, -1), n_ctx, q_first, k_w=HQ, k_blk=1,
                              bias_tab=_nbr_bias(rpb_c[l]))
        o_d = _dense_attention(kd.reshape(b, s, -1), qdt, vdt, n_ctx, DH, q_first)
        flat = lambda t: t.reshape(b * s, HQ)
        h = _merge(flat(o_a), flat(o_b), flat(o_c), flat(o_d), h, mod, g_pre, g_post_mix[l][None, :],
                   w["wg"], w["wbr"], w_out[l].astype(BF16), tiles_per_batch, last)
        h = _ffn(h, mod, g_pre_ffn[l][None, :], g_post_ffn[l][None, :],
                 w_ffn_in[l].astype(BF16), w_ffn_out[l].astype(BF16), tiles_per_batch, last)
    return h.reshape(b, n_lat, d)
```

```python
import functools

import jax
import jax.numpy as jnp
from jax import lax
from jax.experimental import pallas as pl
from jax.experimental.pallas import tpu as pltpu

F32 = jnp.float32
BF16 = jnp.bfloat16

GRID_W = 64
ROPE_THETA = 10000.0
EPS = 1e-6
NEG = -1e30
DH = 64
N_HEADS = 8
N_BRANCH = 4
Q_LORA = 256
KV_LORA = 128
NOPE_A = 64
ROPE_A = 32
KV_B = 2
WINDOW = 128
KH = 8
KW = 16
KV_D = 2
MOD_CHUNKS = 6

LANES = 128
F32_SUBLANES = 8
VMEM_LIMIT_BYTES = 56 * 1024 * 1024

TM = 256
HEAD_PAD_A = 128
HQ = N_HEADS * DH
S_A = (NOPE_A + ROPE_A) ** -0.5
S_H = DH ** -0.5
LOG2E = 1.4426950408889634
KEY_CHUNK = 1152
GQA_HEAD_ORDER = tuple(h for j in range(N_HEADS // 2) for h in (j, j + N_HEADS // 2))


def _cparams(n_axes):
    return pltpu.CompilerParams(
        dimension_semantics=("arbitrary",) * n_axes, vmem_limit_bytes=VMEM_LIMIT_BYTES)


def _resident(shape):
    nd = len(shape)
    return pl.BlockSpec(shape, lambda *_: (0,) * nd, pipeline_mode=pl.Buffered(1))


def _row_tiles(m, tiles_per_batch, latent_only):
    if not latent_only:
        return (lambda i: (i, 0)), m // TM
    lat = tiles_per_batch - 1
    return (lambda i: ((i // lat) * tiles_per_batch + 1 + i % lat, 0)), m // TM // tiles_per_batch * lat


def _mod_spec(d, tiles_per_batch, latent_only=False):
    if latent_only:
        lat = tiles_per_batch - 1
        return pl.BlockSpec((None, None, MOD_CHUNKS, d), lambda i: (i // lat, 1, 0, 0))
    return pl.BlockSpec((None, None, MOD_CHUNKS, d),
                        lambda i: (i // tiles_per_batch, jnp.minimum(i % tiles_per_batch, 1), 0, 0))


def _dot(a, b):
    return jnp.dot(a, b, preferred_element_type=F32)


def _dot_nt(a, b):
    return lax.dot_general(a, b, (((1,), (1,)), ((), ())), preferred_element_type=F32)


def _rms(x, g):
    ms = jnp.mean(x * x, axis=-1, keepdims=True)
    return x * lax.rsqrt(ms + EPS) * g


def _sigmoid(x):
    return 1.0 / (1.0 + jnp.exp(-x))


def _rope(x, tab_ref, shift):
    n = x.shape[-1]
    return (x * tab_ref[0] + pltpu.roll(x, n - shift, 1) * tab_ref[1]
            + pltpu.roll(x, shift, 1) * tab_ref[2])


def _head_rms(x, g, bd):
    x2 = x * x
    hi = x2.astype(BF16)
    lo = (x2 - hi.astype(F32)).astype(BF16)
    ss = _dot(hi, bd) + _dot(lo, bd)
    return x * lax.rsqrt(ss * (1.0 / DH) + EPS) * g


def _pair_scores(q_blk, parts, s_ref, slot):
    m = q_blk.shape[0]
    lane = lax.broadcasted_iota(jnp.int32, (m, LANES), 1)
    zero = jnp.zeros_like(q_blk)
    qs = jnp.concatenate([jnp.where(lane < DH, q_blk, zero), jnp.where(lane >= DH, q_blk, zero)], axis=0)
    layout = []
    off = 0
    for k, load_v, bias in parts:
        n = k.shape[0]
        s = _dot_nt(qs, k)
        s_ref[slot, 0:2 * m, off:off + n] = s if bias is None else s + bias
        layout.append((off, n, load_v))
        off += n
    return m, layout


def _pair_finish(handle, s_ref, slot, sinks=None):
    m, layout = handle
    width = sum(n for _, n, _ in layout)
    s = s_ref[slot, 0:2 * m, 0:width]
    mx = jnp.max(s, axis=1, keepdims=True)
    sink = None
    if sinks is not None:
        row = lax.broadcasted_iota(jnp.int32, (2 * m, 1), 0)
        sink = jnp.where(row < m, sinks[0], sinks[1]) * LOG2E
        mx = jnp.maximum(mx, sink)
    p = jnp.exp2(s - mx)
    l = jnp.sum(p, axis=1, keepdims=True)
    if sink is not None:
        l = l + jnp.exp2(sink - mx)
    pb = p.astype(BF16)
    acc = None
    for off, n, load_v in layout:
        a = _dot(pb[:, off:off + n], load_v())
        acc = a if acc is None else acc + a
    o = acc / l
    lane = lax.broadcasted_iota(jnp.int32, (m, LANES), 1)
    return jnp.where(lane < DH, o[:m], o[m:])


def _pipelined(items, scores_fn, finish_fn):
    cur = scores_fn(items[0], 0)
    for n, item in enumerate(items):
        nxt = scores_fn(items[n + 1], (n + 1) % 2) if n + 1 < len(items) else None
        finish_fn(item, cur, n % 2)
        cur = nxt


def _mod_kernel(c_ref, w_ref, b_ref, o_ref):
    c = c_ref[...]
    sc = (c * _sigmoid(c)).astype(BF16)
    o_ref[...] = _dot(sc, w_ref[...].astype(BF16)) + b_ref[...]


def _modulation(cc, w_mod, b_mod):
    n_layers, d, n = w_mod.shape
    r = cc.shape[0]
    tn = n // 4
    return pl.pallas_call(
        _mod_kernel,
        grid=(n_layers, n // tn),
        in_specs=[
            pl.BlockSpec((r, d), lambda l, j: (0, 0)),
            pl.BlockSpec((None, d, tn), lambda l, j: (l, 0, j)),
            pl.BlockSpec((None, 1, tn), lambda l, j: (l, 0, j)),
        ],
        out_specs=pl.BlockSpec((None, r, tn), lambda l, j: (l, 0, j)),
        out_shape=jax.ShapeDtypeStruct((n_layers, r, n), F32),
        compiler_params=_cparams(2),
        name="modulation",
    )(cc, w_mod, b_mod.reshape(n_layers, 1, n))


def _nbr_bias_kernel(rpb_ref, o_ref):
    pair = pl.program_id(0)
    n_dr = 2 * KH - 1
    n_dc = 2 * KW - 1
    qc = lax.broadcasted_iota(jnp.int32, (GRID_W, 2 * GRID_W), 0)
    lane = lax.broadcasted_iota(jnp.int32, (GRID_W, 2 * GRID_W), 1)
    second = lane >= GRID_W
    kc = jnp.where(second, lane - GRID_W, lane)
    c0 = jnp.clip(qc - KW // 2, 0, GRID_W - KW)
    ok = (kc >= c0) & (kc < c0 + KW)
    dc = kc - qc + (KW - 1)

    for e in range(2):
        h = 2 * pair + e

        def d2_body(d2, carry, h=h, e=e):
            def d_body(d, acc):
                v0 = rpb_ref[(h * n_dr + d2) * n_dc + d]
                v1 = rpb_ref[(h * n_dr + d2 + 1) * n_dc + d]
                return jnp.where(dc == d, jnp.where(second, v1, v0), acc)

            acc = lax.fori_loop(0, n_dc, d_body, jnp.zeros((GRID_W, 2 * GRID_W), F32))
            o_ref[d2, e * GRID_W:(e + 1) * GRID_W, :] = jnp.where(ok, acc * LOG2E, NEG)
            return carry

        lax.fori_loop(0, n_dr - 1, d2_body, 0)


def _nbr_bias(rpb):
    h = rpb.shape[0]
    n_pairs = 2 * KH - 2
    return pl.pallas_call(
        _nbr_bias_kernel,
        grid=(h // 2,),
        in_specs=[pl.BlockSpec(memory_space=pltpu.SMEM)],
        out_specs=pl.BlockSpec((None, n_pairs, 2 * GRID_W, 2 * GRID_W), lambda i: (i, 0, 0, 0)),
        out_shape=jax.ShapeDtypeStruct((h // 2, n_pairs, 2 * GRID_W, 2 * GRID_W), F32),
        compiler_params=_cparams(1),
        name="nbr_bias",
    )(rpb.reshape(-1))


def _proj_kernel(h_ref, mod_ref, g_ref, wa_ref, wb_ref, wc_ref, wd_ref, wuq_ref, wk_ref, wv_ref,
                 gaq_ref, gakv_ref, gdq_ref, gdk_ref, ropea_ref, rope64_ref,
                 ka_ref, qat_ref, vat_ref, pb_ref, pc_ref, kd_ref, qdt_ref, vdt_ref):
    sh1 = mod_ref[0:1, :]
    sc1 = mod_ref[1:2, :]
    u = (_rms(h_ref[...], g_ref[...]) * (1.0 + sc1) + sh1).astype(BF16)

    rb = jnp.where(lax.broadcasted_iota(jnp.int32, (LANES, LANES), 0) < DH, 0, 1)
    cb = jnp.where(lax.broadcasted_iota(jnp.int32, (LANES, LANES), 1) < DH, 0, 1)
    bd = jnp.where(rb == cb, 1.0, 0.0).astype(BF16)

    ya = _dot(u, wa_ref[...])
    yd = _dot(u, wd_ref[...])
    cq = _rms(ya[:, :Q_LORA], gaq_ref[...]).astype(BF16)
    ckv = _rms(ya[:, Q_LORA:Q_LORA + KV_LORA], gakv_ref[...]).astype(BF16)
    kr = _rope(ya[:, Q_LORA + KV_LORA:], ropea_ref, ROPE_A // 4)
    yb = _dot(u, wb_ref[...])
    qa = _dot(cq, wuq_ref[...])
    ka = _dot(ckv, wk_ref[...])
    va = _dot(ckv, wv_ref[...])

    for c in range(HQ // LANES):
        lo, hi = c * LANES, (c + 1) * LANES
        qn = _head_rms(yd[:, lo:hi], gdq_ref[...], bd)
        qdt_ref[lo:hi, :] = (_rope(qn, rope64_ref, DH // 4) * (S_H * LOG2E)).T.astype(BF16)
    kn = _head_rms(yd[:, HQ:HQ + LANES], gdk_ref[...], bd)
    kd_ref[...] = _rope(kn, rope64_ref, DH // 4).astype(BF16)
    vdt_ref[...] = yd[:, HQ + LANES:].T.astype(BF16)

    yc = _dot(u, wc_ref[...])

    for c in range(HQ // LANES):
        lo, hi = c * LANES, (c + 1) * LANES
        pb_ref[:, lo:hi] = (_rope(yb[:, lo:hi], rope64_ref, DH // 4) * (S_H * LOG2E)).astype(BF16)
    pb_ref[:, HQ:HQ + LANES] = _rope(yb[:, HQ:HQ + LANES], rope64_ref, DH // 4).astype(BF16)
    pb_ref[:, HQ + LANES:] = yb[:, HQ + LANES:].astype(BF16)

    for hh in range(N_HEADS):
        lo, hi = hh * HEAD_PAD_A, (hh + 1) * HEAD_PAD_A
        q = _rope(qa[:, lo:hi], ropea_ref, ROPE_A // 4) * (S_A * LOG2E)
        qat_ref[lo:hi, :] = q.T.astype(BF16)
        ka_ref[:, lo:hi] = (ka[:, lo:hi] + kr).astype(BF16)
    for c in range(HQ // LANES):
        lo, hi = c * LANES, (c + 1) * LANES
        vat_ref[lo:hi, :] = va[:, lo:hi].T.astype(BF16)

    pc_ref[:, :HQ] = (yc[:, :HQ] * (S_H * LOG2E)).astype(BF16)
    pc_ref[:, HQ:] = yc[:, HQ:].astype(BF16)


def _project(h, mod, g_pre, w, tabs, n_batch, tiles_per_batch):
    m, d = h.shape
    s = tiles_per_batch * TM
    wq_a = N_HEADS * HEAD_PAD_A
    row = lambda i: (i, 0)
    col = lambda i: (i // tiles_per_batch, 0, i % tiles_per_batch)
    tile_pos = lambda i: (0, i % tiles_per_batch, 0)
    in_specs = [
        pl.BlockSpec((TM, d), row),
        _mod_spec(d, tiles_per_batch),
        _resident((1, d)),
        _resident(w["wa"].shape), _resident(w["wb"].shape), _resident(w["wc"].shape),
        _resident(w["wd"].shape), _resident(w["wuq"].shape),
        _resident(w["wk"].shape), _resident(w["wv"].shape),
        _resident((1, Q_LORA)), _resident((1, KV_LORA)), _resident((1, LANES)), _resident((1, LANES)),
        pl.BlockSpec((3, TM, LANES), tile_pos),
        pl.BlockSpec((3, TM, LANES), tile_pos),
    ]

    def rows_out(c):
        return pl.BlockSpec((TM, c), row), jax.ShapeDtypeStruct((m, c), BF16)

    def cols_out(r):
        return pl.BlockSpec((None, r, TM), col), jax.ShapeDtypeStruct((n_batch, r, s), BF16)

    outs = [rows_out(wq_a), cols_out(wq_a), cols_out(HQ),
            rows_out(HQ + 2 * LANES), rows_out(3 * HQ),
            rows_out(LANES), cols_out(HQ), cols_out(LANES)]
    return pl.pallas_call(
        _proj_kernel,
        grid=(m // TM,),
        in_specs=in_specs,
        out_specs=[o[0] for o in outs],
        out_shape=[o[1] for o in outs],
        compiler_params=_cparams(1),
        name="proj",
    )(h, mod, g_pre, w["wa"], w["wb"], w["wc"], w["wd"], w["wuq"], w["wk"], w["wv"],
      w["gaq"], w["gakv"], w["gdq"], w["gdk"], tabs["rope_a"], tabs["rope_64"])


def _dense_attn_kernel(k_ref, qt_ref, vt_ref, o_ref, ot_ref, st_ref, *, n_ctx, dk, q_first):
    qi = pl.program_id(1) + q_first
    group = 1 if dk == LANES else N_HEADS // (LANES // DH)

    def attend(n_keys):
        n_chunks = -(-n_keys // KEY_CHUNK)
        bounds = [(i * n_keys // n_chunks, (i + 1) * n_keys // n_chunks) for i in range(n_chunks)]

        def q_operand(h):
            if dk == LANES:
                return qt_ref[h * LANES:(h + 1) * LANES, :]
            qh = qt_ref[h * DH:(h + 1) * DH, :]
            z = jnp.zeros_like(qh)
            return jnp.concatenate([qh, z] if h // group == 0 else [z, qh], axis=0)

        def scores(h, qt, i):
            lanes = slice(h * LANES, (h + 1) * LANES) if dk == LANES else slice(0, LANES)
            lo, hi = bounds[i]
            st_ref[h % 2, lo:hi, :] = _dot(k_ref[lo:hi, lanes], qt)

        def colmax(h):
            m = None
            for lo, hi in bounds:
                mi = jnp.max(st_ref[h % 2, lo:hi, :], axis=0, keepdims=True)
                m = mi if m is None else jnp.maximum(m, mi)
            return m

        qt = q_operand(0)
        for i in range(n_chunks):
            scores(0, qt, i)
        for h in range(N_HEADS):
            g = h // group
            m = colmax(h)
            qt = q_operand(h + 1) if h + 1 < N_HEADS else None
            l = acc = None
            for i in range(n_chunks):
                if qt is not None:
                    scores(h + 1, qt, i)
                lo, hi = bounds[i]
                p = jnp.exp2(st_ref[h % 2, lo:hi, :] - m)
                li = jnp.sum(p, axis=0, keepdims=True)
                ai = _dot(vt_ref[g * DH:(g + 1) * DH, lo:hi], p.astype(BF16))
                l = li if l is None else l + li
                acc = ai if acc is None else acc + ai
            ot_ref[h * DH:(h + 1) * DH, :] = acc / l
        o_ref[...] = ot_ref[...].T.astype(o_ref.dtype)

    @pl.when(qi < n_ctx // TM)
    def _():
        attend(n_ctx)

    @pl.when(qi >= n_ctx // TM)
    def _():
        attend(k_ref.shape[0])


def _dense_attention(k, qt, vt, n_ctx, dk, q_first):
    b, s, kw = k.shape
    return pl.pallas_call(
        functools.partial(_dense_attn_kernel, n_ctx=n_ctx, dk=dk, q_first=q_first),
        grid=(b, s // TM - q_first),
        in_specs=[
            pl.BlockSpec((None, s, kw), lambda bi, qi: (bi, 0, 0)),
            pl.BlockSpec((None, qt.shape[1], TM), lambda bi, qi: (bi, 0, qi + q_first)),
            pl.BlockSpec((None, vt.shape[1], s), lambda bi, qi: (bi, 0, 0)),
        ],
        out_specs=pl.BlockSpec((None, TM, HQ), lambda bi, qi: (bi, qi + q_first, 0)),
        out_shape=jax.ShapeDtypeStruct((b, s, HQ), BF16),
        scratch_shapes=[pltpu.VMEM((HQ, TM), F32), pltpu.VMEM((2, s, TM), F32)],
        compiler_params=_cparams(2),
        name="attn_dense",
    )(k, qt, vt)


def _window_attn_kernel(sink_ref, q_ref, k_ref, v_ref, o_ref, s_ref, *, n_ctx, q_first):
    qi = pl.program_id(1) + q_first
    n_lat = k_ref.shape[0] - n_ctx
    span = TM + 2 * WINDOW
    blocks = list(range(HQ // LANES))

    def ctx_part():
        return k_ref[0:n_ctx, :], lambda: v_ref[0:n_ctx, :], None

    def finish(j, handle, slot):
        sinks = sink_ref[GQA_HEAD_ORDER[2 * j]], sink_ref[GQA_HEAD_ORDER[2 * j + 1]]
        o_ref[:, j * LANES:(j + 1) * LANES] = _pair_finish(handle, s_ref, slot, sinks).astype(o_ref.dtype)

    @pl.when(qi < n_ctx // TM)
    def _():
        def scores(j, slot):
            return _pair_scores(q_ref[:, j * LANES:(j + 1) * LANES], [ctx_part()], s_ref, slot)

        _pipelined(blocks, scores, finish)

    @pl.when(qi >= n_ctx // TM)
    def _():
        q0 = (qi - n_ctx // TM) * TM
        start = pl.multiple_of(jnp.clip(q0 - WINDOW, 0, n_lat - span), WINDOW)
        dist = (lax.broadcasted_iota(jnp.int32, (2 * TM, span), 0)
                - lax.broadcasted_iota(jnp.int32, (2 * TM, span), 1) + (q0 - start))
        dist = jnp.where(lax.broadcasted_iota(jnp.int32, (2 * TM, span), 0) >= TM, dist - TM, dist)
        bias = jnp.where(jnp.abs(dist) <= WINDOW, 0.0, NEG)
        rows = pl.ds(n_ctx + start, span)

        def scores(j, slot):
            return _pair_scores(q_ref[:, j * LANES:(j + 1) * LANES],
                                [(k_ref[rows, :], lambda: v_ref[rows, :], bias), ctx_part()], s_ref, slot)

        _pipelined(blocks, scores, finish)


def _nbr_attn_kernel(q_ref, k_ref, v_ref, tb_ref, o_ref, s_ref, *, n_ctx, q_first):
    qi = pl.program_id(1) + q_first
    n_lat = k_ref.shape[0] - n_ctx
    grid_rows = n_lat // GRID_W
    rows_per_tile = TM // GRID_W
    span = KH * GRID_W
    n_blk = HQ // LANES

    def ctx_part(blk):
        return k_ref[0:n_ctx, blk], lambda: v_ref[0:n_ctx, blk], None

    @pl.when(qi < n_ctx // TM)
    def _():
        def scores(j, slot):
            blk = slice(j * LANES, (j + 1) * LANES)
            return _pair_scores(q_ref[:, blk], [ctx_part(blk)], s_ref, slot)

        def finish(j, handle, slot):
            o_ref[:, j * LANES:(j + 1) * LANES] = _pair_finish(handle, s_ref, slot).astype(o_ref.dtype)

        _pipelined(list(range(n_blk)), scores, finish)

    @pl.when(qi >= n_ctx // TM)
    def _():
        def scores(item, slot):
            rr, j = item
            r = (qi - n_ctx // TM) * rows_per_tile + rr
            r0 = jnp.clip(r - KH // 2, 0, grid_rows - KH)
            delta = r - r0
            krows = pl.ds(pl.multiple_of(n_ctx + r0 * GRID_W, GRID_W), span)
            blk = slice(j * LANES, (j + 1) * LANES)
            bias = jnp.concatenate(
                [tb_ref[j, 2 * jj - delta + (KH - 1)] for jj in range(KH // 2)], axis=1)
            return _pair_scores(q_ref[rr * GRID_W:(rr + 1) * GRID_W, blk],
                                [(k_ref[krows, blk], lambda: v_ref[krows, blk], bias), ctx_part(blk)],
                                s_ref, slot)

        def finish(item, handle, slot):
            rr, j = item
            o_ref[rr * GRID_W:(rr + 1) * GRID_W, j * LANES:(j + 1) * LANES] = (
                _pair_finish(handle, s_ref, slot).astype(o_ref.dtype))

        _pipelined([(rr, j) for rr in range(rows_per_tile) for j in range(n_blk)], scores, finish)


def _pair_attention(kind, p, n_ctx, q_first, *, k_w, k_blk, sink=None, bias_tab=None):
    b, s, _ = p.shape
    in_specs = [
        pl.BlockSpec((None, TM, HQ), lambda bi, qi: (bi, qi + q_first, 0)),
        pl.BlockSpec((None, s, k_w), lambda bi, qi: (bi, 0, k_blk)),
        pl.BlockSpec((None, s, k_w), lambda bi, qi: (bi, 0, k_blk + 1)),
    ]
    args = [p, p, p]
    if kind == "window":
        body = functools.partial(_window_attn_kernel, n_ctx=n_ctx, q_first=q_first)
        in_specs = [pl.BlockSpec(memory_space=pltpu.SMEM)] + in_specs
        args = [sink] + args
    else:
        body = functools.partial(_nbr_attn_kernel, n_ctx=n_ctx, q_first=q_first)
        in_specs = in_specs + [_resident(bias_tab.shape)]
        args = args + [bias_tab]
    return pl.pallas_call(
        body,
        grid=(b, s // TM - q_first),
        in_specs=in_specs,
        out_specs=pl.BlockSpec((None, TM, HQ), lambda bi, qi: (bi, qi + q_first, 0)),
        out_shape=jax.ShapeDtypeStruct((b, s, HQ), BF16),
        scratch_shapes=[pltpu.VMEM((2, 2 * TM, max(TM + 2 * WINDOW, KH * GRID_W) + n_ctx), F32)],
        compiler_params=_cparams(2),
        name="attn_" + kind,
    )(*args)


def _merge_kernel(oa_ref, ob_ref, oc_ref, od_ref, h_ref, mod_ref, gpre_ref, gpost_ref,
                  wg_ref, wbr_ref, wout_ref, o_ref, acc_ref):
    d_model = h_ref.shape[1]
    cw = 256
    h = h_ref[...]
    sh1 = mod_ref[0:1, :]
    sc1 = mod_ref[1:2, :]
    ga1 = mod_ref[2:3, :]
    u = (_rms(h, gpre_ref[...]) * (1.0 + sc1) + sh1).astype(BF16)
    branches = (oa_ref, ob_ref, oc_ref, od_ref)
    for c in range(d_model // cw):
        acc = None
        for n, br in enumerate(branches):
            lo = n * d_model + c * cw
            gate = _dot(u, wg_ref[:, lo:lo + cw])
            up = _dot(br[...], wbr_ref[n, :, c * cw:(c + 1) * cw])
            term = _sigmoid(gate) * up
            acc = term if acc is None else acc + term
        acc_ref[:, c * cw:(c + 1) * cw] = acc.astype(BF16)
    y = _dot(acc_ref[...], wout_ref[...])
    o_ref[...] = h + ga1 * _rms(y, gpost_ref[...])


def _merge(o_a, o_b, o_c, o_d, h, mod, g_pre, g_post, wg, wbr, wout, tiles_per_batch, latent_only):
    m, d = h.shape
    row, n_tiles = _row_tiles(m, tiles_per_batch, latent_only)
    return pl.pallas_call(
        _merge_kernel,
        grid=(n_tiles,),
        in_specs=[pl.BlockSpec((TM, HQ), row)] * N_BRANCH + [
            pl.BlockSpec((TM, d), row),
            _mod_spec(d, tiles_per_batch, latent_only),
            _resident((1, d)), _resident((1, d)),
            _resident(wg.shape), _resident(wbr.shape), _resident(wout.shape),
        ],
        out_specs=pl.BlockSpec((TM, d), row),
        out_shape=jax.ShapeDtypeStruct((m, d), F32),
        scratch_shapes=[pltpu.VMEM((TM, d), BF16)],
        input_output_aliases={4: 0},
        compiler_params=_cparams(1),
        name="merge",
    )(o_a, o_b, o_c, o_d, h, mod, g_pre, g_post, wg, wbr, wout)


def _ffn_kernel(h_ref, mod_ref, gpre_ref, gpost_ref, w1_ref, w2_ref, o_ref, act_ref):
    d_ff = w2_ref.shape[0]
    cw = 256
    h = h_ref[...]
    sh2 = mod_ref[3:4, :]
    sc2 = mod_ref[4:5, :]
    ga2 = mod_ref[5:6, :]
    u = (_rms(h, gpre_ref[...]) * (1.0 + sc2) + sh2).astype(BF16)
    for c in range(d_ff // cw):
        a = _dot(u, w1_ref[:, c * cw:(c + 1) * cw])
        b = _dot(u, w1_ref[:, d_ff + c * cw:d_ff + (c + 1) * cw])
        act_ref[:, c * cw:(c + 1) * cw] = (a * _sigmoid(a) * b).astype(BF16)
    y = _dot(act_ref[...], w2_ref[...])
    o_ref[...] = h + ga2 * _rms(y, gpost_ref[...])


def _ffn(h, mod, g_pre, g_post, w1, w2, tiles_per_batch, latent_only):
    m, d = h.shape
    row, n_tiles = _row_tiles(m, tiles_per_batch, latent_only)
    return pl.pallas_call(
        _ffn_kernel,
        grid=(n_tiles,),
        in_specs=[
            pl.BlockSpec((TM, d), row),
            _mod_spec(d, tiles_per_batch, latent_only),
            _resident((1, d)), _resident((1, d)), _resident(w1.shape), _resident(w2.shape),
        ],
        out_specs=pl.BlockSpec((TM, d), lambda i: (i, 0)),
        out_shape=jax.ShapeDtypeStruct((n_tiles * TM, d), F32),
        scratch_shapes=[pltpu.VMEM((TM, w2.shape[0]), BF16)],
        input_output_aliases={} if latent_only else {0: 0},
        compiler_params=_cparams(1),
        name="ffn",
    )(h, mod, g_pre, g_post, w1, w2)


def _rope_tables(n_ctx, n_lat):
    t = jnp.arange(n_lat)
    rows_pos = (t // GRID_W).astype(F32)
    cols_pos = (t % GRID_W).astype(F32)

    def block(d):
        q = d // 4
        inv = ROPE_THETA ** (-jnp.arange(q, dtype=F32) * 2.0 / (d // 2))
        ar = rows_pos[:, None] * inv[None, :]
        ac = cols_pos[:, None] * inv[None, :]
        z = jnp.zeros_like(ar)
        cos = jnp.concatenate([jnp.cos(ar), jnp.cos(ar), jnp.cos(ac), jnp.cos(ac)], axis=1)
        up = jnp.concatenate([-jnp.sin(ar), z, -jnp.sin(ac), z], axis=1)
        dn = jnp.concatenate([z, jnp.sin(ar), z, jnp.sin(ac)], axis=1)
        return cos, up, dn

    def with_ctx(tab, fill):
        return jnp.concatenate([jnp.full((n_ctx, tab.shape[1]), fill, F32), tab], axis=0)

    c64, u64, d64 = block(DH)
    rope_64 = jnp.stack([with_ctx(jnp.tile(c64, (1, LANES // DH)), 1.0),
                         with_ctx(jnp.tile(u64, (1, LANES // DH)), 0.0),
                         with_ctx(jnp.tile(d64, (1, LANES // DH)), 0.0)])
    ca, ua, da = block(ROPE_A)
    pad_lo = NOPE_A
    pad_hi = HEAD_PAD_A - NOPE_A - ROPE_A

    def widen(tab, fill):
        return jnp.concatenate([jnp.full((n_lat, pad_lo), fill, F32), tab,
                                jnp.full((n_lat, pad_hi), fill, F32)], axis=1)

    rope_a = jnp.stack([with_ctx(widen(ca, 1.0), 1.0), with_ctx(widen(ua, 0.0), 0.0),
                        with_ctx(widen(da, 0.0), 0.0)])
    return {"rope_a": rope_a, "rope_64": rope_64}


def _permute_heads(w, axis, order):
    shape = w.shape
    w = w.reshape(shape[:axis] + (N_HEADS, DH) + shape[axis + 1:])
    w = jnp.take(w, jnp.asarray(order), axis=axis)
    return w.reshape(shape)


def _layer_weights(l, w_in, g_a_q, g_a_kv, w_a_uq, w_a_ukv, g_d_q, g_d_k, w_branch):
    d = w_in.shape[1]
    wl = w_in[l]
    o = 0
    a_cq = wl[:, o:o + Q_LORA]; o += Q_LORA
    a_ckv = wl[:, o:o + KV_LORA]; o += KV_LORA
    a_kr = wl[:, o:o + ROPE_A]; o += ROPE_A
    w_bq = wl[:, o:o + HQ]; o += HQ
    w_bkv = wl[:, o:o + 2 * KV_B * DH]; o += 2 * KV_B * DH
    w_c = wl[:, o:o + 3 * HQ]; o += 3 * HQ
    w_d = wl[:, o:o + HQ + 2 * KV_D * DH]; o += HQ + 2 * KV_D * DH
    w_g = wl[:, o:]
    zeros = lambda n: jnp.zeros((d, n), wl.dtype)
    wa = jnp.concatenate([a_cq, a_ckv, zeros(NOPE_A), a_kr, zeros(HEAD_PAD_A - NOPE_A - ROPE_A)], axis=1)
    w_b = jnp.concatenate([_permute_heads(w_bq, 1, GQA_HEAD_ORDER), w_bkv], axis=1)
    uq = w_a_uq[l].reshape(Q_LORA, N_HEADS, NOPE_A + ROPE_A)
    uq = jnp.pad(uq, ((0, 0), (0, 0), (0, HEAD_PAD_A - NOPE_A - ROPE_A))).reshape(Q_LORA, N_HEADS * HEAD_PAD_A)
    ukv = w_a_ukv[l].reshape(KV_LORA, N_HEADS, NOPE_A + DH)
    wk = jnp.pad(ukv[:, :, :NOPE_A], ((0, 0), (0, 0), (0, HEAD_PAD_A - NOPE_A))).reshape(KV_LORA, N_HEADS * HEAD_PAD_A)
    wv = ukv[:, :, NOPE_A:].reshape(KV_LORA, HQ)
    wbr = w_branch[l]
    wbr = jnp.stack([wbr[0], _permute_heads(wbr[1], 0, GQA_HEAD_ORDER), wbr[2], wbr[3]])
    cast = lambda x: x.astype(BF16)
    return {
        "wa": cast(wa), "wb": cast(w_b), "wc": cast(w_c), "wd": cast(w_d), "wg": cast(w_g),
        "wuq": cast(uq), "wk": cast(wk), "wv": cast(wv), "wbr": cast(wbr),
        "gaq": g_a_q[l][None, :], "gakv": g_a_kv[l][None, :],
        "gdq": jnp.tile(g_d_q[l], LANES // DH)[None, :], "gdk": jnp.tile(g_d_k[l], LANES // DH)[None, :],
    }


def kernel(x, c, ctx, c_ctx, w_mod, b_mod, g_pre_mix, g_post_mix, g_pre_ffn, g_post_ffn,
           w_in, g_a_q, g_a_kv, w_a_uq, w_a_ukv, sink_b, rpb_c, g_d_q, g_d_k,
           w_branch, w_out, w_ffn_in, w_ffn_out):
    b, n_lat, d = x.shape
    n_ctx = ctx.shape[1]
    s = n_ctx + n_lat
    depth = w_mod.shape[0]
    assert n_ctx == TM and n_lat % TM == 0 and n_lat >= TM + 2 * WINDOW
    assert n_lat % GRID_W == 0 and n_lat // GRID_W >= KH
    assert KV_B * DH == LANES and KV_D * DH == LANES
    tiles_per_batch = s // TM

    n_rows = -(-(b + 1) // F32_SUBLANES) * F32_SUBLANES
    cc = jnp.zeros((n_rows, d), F32).at[:b].set(c).at[b].set(c_ctx)
    mods = _modulation(cc, w_mod, b_mod)
    mod_x = mods[:, :b].reshape(depth, b, 1, MOD_CHUNKS, d)
    mod_c = jnp.broadcast_to(mods[:, b].reshape(depth, 1, 1, MOD_CHUNKS, d), mod_x.shape)
    mod_tab = jnp.concatenate([mod_c, mod_x], axis=2)

    tabs = _rope_tables(n_ctx, n_lat)
    h = jnp.concatenate([ctx, x], axis=1).reshape(b * s, d)

    for l in range(depth):
        w = _layer_weights(l, w_in, g_a_q, g_a_kv, w_a_uq, w_a_ukv, g_d_q, g_d_k, w_branch)
        mod = mod_tab[l]
        g_pre = g_pre_mix[l][None, :]
        last = l == depth - 1
        q_first = n_ctx // TM if last else 0
        ka, qat, vat, pb, pc, kd, qdt, vdt = _project(h, mod, g_pre, w, tabs, b, tiles_per_batch)
        o_a = _dense_attention(ka.reshape(b, s, -1), qat, vat, n_ctx, HEAD_PAD_A, q_first)
        o_b = _pair_attention("window", pb.reshape(b, s, -1), n_ctx, q_first, k_w=LANES, k_blk=HQ // LANES,
                              sink=sink_b[l])
        o_c = _pair_attention("nbr", pc.reshape(b, s, -1), n_ctx, q_first, k_w=HQ, k_blk=1,
                              bias_tab=_nbr_bias(rpb_c[l]))
        o_d = _dense_attention(kd.reshape(b, s, -1), qdt, vdt, n_ctx, DH, q_first)
        flat = lambda t: t.reshape(b * s, HQ)
        h = _merge(flat(o_a), flat(o_b), flat(o_c), flat(o_d), h, mod, g_pre, g_post_mix[l][None, :],
                   w["wg"], w["wbr"], w_out[l].astype(BF16), tiles_per_batch, last)
        h = _ffn(h, mod, g_pre_ffn[l][None, :], g_post_ffn[l][None, :],
                 w_ffn_in[l].astype(BF16), w_ffn_out[l].astype(BF16), tiles_per_batch, last)
    return h.reshape(b, n_lat, d)
```

```python
import functools

import jax
import jax.numpy as jnp
from jax import lax
from jax.experimental import pallas as pl
from jax.experimental.pallas import tpu as pltpu

F32 = jnp.float32
BF16 = jnp.bfloat16

GRID_W = 64
ROPE_THETA = 10000.0
EPS = 1e-6
NEG = -1e30
DH = 64
N_HEADS = 8
N_BRANCH = 4
Q_LORA = 256
KV_LORA = 128
NOPE_A = 64
ROPE_A = 32
KV_B = 2
WINDOW = 128
KH = 8
KW = 16
KV_D = 2
MOD_CHUNKS = 6

LANES = 128
F32_SUBLANES = 8
VMEM_LIMIT_BYTES = 56 * 1024 * 1024

TM = 256
HEAD_PAD_A = 128
HQ = N_HEADS * DH
S_A = (NOPE_A + ROPE_A) ** -0.5
S_H = DH ** -0.5
LOG2E = 1.4426950408889634
KEY_CHUNK = 1152
ONES_ROWS = 16
VT_ROWS = DH + ONES_ROWS
GQA_HEAD_ORDER = tuple(h for j in range(N_HEADS // 2) for h in (j, j + N_HEADS // 2))


def _cparams(n_axes):
    return pltpu.CompilerParams(
        dimension_semantics=("arbitrary",) * n_axes, vmem_limit_bytes=VMEM_LIMIT_BYTES)


def _resident(shape):
    nd = len(shape)
    return pl.BlockSpec(shape, lambda *_: (0,) * nd, pipeline_mode=pl.Buffered(1))


def _row_tiles(m, tiles_per_batch, latent_only):
    if not latent_only:
        return (lambda i: (i, 0)), m // TM
    lat = tiles_per_batch - 1
    return (lambda i: ((i // lat) * tiles_per_batch + 1 + i % lat, 0)), m // TM // tiles_per_batch * lat


def _mod_spec(d, tiles_per_batch, latent_only=False):
    if latent_only:
        lat = tiles_per_batch - 1
        return pl.BlockSpec((None, None, MOD_CHUNKS, d), lambda i: (i // lat, 1, 0, 0))
    return pl.BlockSpec((None, None, MOD_CHUNKS, d),
                        lambda i: (i // tiles_per_batch, jnp.minimum(i % tiles_per_batch, 1), 0, 0))


def _dot(a, b):
    return jnp.dot(a, b, preferred_element_type=F32)


def _dot_nt(a, b):
    return lax.dot_general(a, b, (((1,), (1,)), ((), ())), preferred_element_type=F32)


def _rms(x, g):
    ms = jnp.mean(x * x, axis=-1, keepdims=True)
    return x * lax.rsqrt(ms + EPS) * g


def _sigmoid(x):
    return 1.0 / (1.0 + jnp.exp(-x))


def _rope(x, tab_ref, shift):
    n = x.shape[-1]
    return (x * tab_ref[0] + pltpu.roll(x, n - shift, 1) * tab_ref[1]
            + pltpu.roll(x, shift, 1) * tab_ref[2])


def _head_rms(x, g, bd):
    x2 = x * x
    hi = x2.astype(BF16)
    lo = (x2 - hi.astype(F32)).astype(BF16)
    ss = _dot(hi, bd) + _dot(lo, bd)
    return x * lax.rsqrt(ss * (1.0 / DH) + EPS) * g


def _pair_scores(q_blk, parts, s_ref, slot):
    m = q_blk.shape[0]
    lane = lax.broadcasted_iota(jnp.int32, (m, LANES), 1)
    zero = jnp.zeros_like(q_blk)
    qs = jnp.concatenate([jnp.where(lane < DH, q_blk, zero), jnp.where(lane >= DH, q_blk, zero)], axis=0)
    layout = []
    off = 0
    for k, load_v, bias in parts:
        n = k.shape[0]
        s = _dot_nt(qs, k)
        s_ref[slot, 0:2 * m, off:off + n] = s if bias is None else s + bias
        layout.append((off, n, load_v))
        off += n
    return m, layout


def _pair_finish(handle, s_ref, slot, sinks=None):
    m, layout = handle
    width = sum(n for _, n, _ in layout)
    s = s_ref[slot, 0:2 * m, 0:width]
    mx = jnp.max(s, axis=1, keepdims=True)
    sink = None
    if sinks is not None:
        row = lax.broadcasted_iota(jnp.int32, (2 * m, 1), 0)
        sink = jnp.where(row < m, sinks[0], sinks[1]) * LOG2E
        mx = jnp.maximum(mx, sink)
    p = jnp.exp2(s - mx)
    l = jnp.sum(p, axis=1, keepdims=True)
    if sink is not None:
        l = l + jnp.exp2(sink - mx)
    pb = p.astype(BF16)
    acc = None
    for off, n, load_v in layout:
        a = _dot(pb[:, off:off + n], load_v())
        acc = a if acc is None else acc + a
    o = acc / l
    lane = lax.broadcasted_iota(jnp.int32, (m, LANES), 1)
    return jnp.where(lane < DH, o[:m], o[m:])


def _pipelined(items, scores_fn, finish_fn):
    cur = scores_fn(items[0], 0)
    for n, item in enumerate(items):
        nxt = scores_fn(items[n + 1], (n + 1) % 2) if n + 1 < len(items) else None
        finish_fn(item, cur, n % 2)
        cur = nxt


def _mod_kernel(c_ref, w_ref, b_ref, o_ref):
    c = c_ref[...]
    sc = (c * _sigmoid(c)).astype(BF16)
    o_ref[...] = _dot(sc, w_ref[...].astype(BF16)) + b_ref[...]


def _modulation(cc, w_mod, b_mod):
    n_layers, d, n = w_mod.shape
    r = cc.shape[0]
    tn = n // 4
    return pl.pallas_call(
        _mod_kernel,
        grid=(n_layers, n // tn),
        in_specs=[
            pl.BlockSpec((r, d), lambda l, j: (0, 0)),
            pl.BlockSpec((None, d, tn), lambda l, j: (l, 0, j)),
            pl.BlockSpec((None, 1, tn), lambda l, j: (l, 0, j)),
        ],
        out_specs=pl.BlockSpec((None, r, tn), lambda l, j: (l, 0, j)),
        out_shape=jax.ShapeDtypeStruct((n_layers, r, n), F32),
        compiler_params=_cparams(2),
        name="modulation",
    )(cc, w_mod, b_mod.reshape(n_layers, 1, n))


def _nbr_bias_kernel(rpb_ref, o_ref):
    pair = pl.program_id(0)
    n_dr = 2 * KH - 1
    n_dc = 2 * KW - 1
    qc = lax.broadcasted_iota(jnp.int32, (GRID_W, 2 * GRID_W), 0)
    lane = lax.broadcasted_iota(jnp.int32, (GRID_W, 2 * GRID_W), 1)
    second = lane >= GRID_W
    kc = jnp.where(second, lane - GRID_W, lane)
    c0 = jnp.clip(qc - KW // 2, 0, GRID_W - KW)
    ok = (kc >= c0) & (kc < c0 + KW)
    dc = kc - qc + (KW - 1)

    for e in range(2):
        h = 2 * pair + e

        def d2_body(d2, carry, h=h, e=e):
            def d_body(d, acc):
                v0 = rpb_ref[(h * n_dr + d2) * n_dc + d]
                v1 = rpb_ref[(h * n_dr + d2 + 1) * n_dc + d]
                return jnp.where(dc == d, jnp.where(second, v1, v0), acc)

            acc = lax.fori_loop(0, n_dc, d_body, jnp.zeros((GRID_W, 2 * GRID_W), F32))
            o_ref[d2, e * GRID_W:(e + 1) * GRID_W, :] = jnp.where(ok, acc * LOG2E, NEG)
            return carry

        lax.fori_loop(0, n_dr - 1, d2_body, 0)


def _nbr_bias(rpb):
    h = rpb.shape[0]
    n_pairs = 2 * KH - 2
    return pl.pallas_call(
        _nbr_bias_kernel,
        grid=(h // 2,),
        in_specs=[pl.BlockSpec(memory_space=pltpu.SMEM)],
        out_specs=pl.BlockSpec((None, n_pairs, 2 * GRID_W, 2 * GRID_W), lambda i: (i, 0, 0, 0)),
        out_shape=jax.ShapeDtypeStruct((h // 2, n_pairs, 2 * GRID_W, 2 * GRID_W), F32),
        compiler_params=_cparams(1),
        name="nbr_bias",
    )(rpb.reshape(-1))


def _store_vt(vt_ref, c, v):
    t = v.T.astype(BF16)
    ones = jnp.ones((ONES_ROWS, v.shape[0]), BF16)
    for e in range(LANES // DH):
        base = (2 * c + e) * VT_ROWS
        vt_ref[base:base + DH, :] = t[e * DH:(e + 1) * DH, :]
        vt_ref[base + DH:base + VT_ROWS, :] = ones


def _proj_kernel(h_ref, mod_ref, g_ref, wa_ref, wb_ref, wc_ref, wd_ref, wuq_ref, wk_ref, wv_ref,
                 gaq_ref, gakv_ref, gdq_ref, gdk_ref, ropea_ref, rope64_ref,
                 ka_ref, qat_ref, vat_ref, pb_ref, pc_ref, kd_ref, qdt_ref, vdt_ref):
    sh1 = mod_ref[0:1, :]
    sc1 = mod_ref[1:2, :]
    u = (_rms(h_ref[...], g_ref[...]) * (1.0 + sc1) + sh1).astype(BF16)

    rb = jnp.where(lax.broadcasted_iota(jnp.int32, (LANES, LANES), 0) < DH, 0, 1)
    cb = jnp.where(lax.broadcasted_iota(jnp.int32, (LANES, LANES), 1) < DH, 0, 1)
    bd = jnp.where(rb == cb, 1.0, 0.0).astype(BF16)

    ya = _dot(u, wa_ref[...])
    yd = _dot(u, wd_ref[...])
    cq = _rms(ya[:, :Q_LORA], gaq_ref[...]).astype(BF16)
    ckv = _rms(ya[:, Q_LORA:Q_LORA + KV_LORA], gakv_ref[...]).astype(BF16)
    kr = _rope(ya[:, Q_LORA + KV_LORA:], ropea_ref, ROPE_A // 4)
    yb = _dot(u, wb_ref[...])
    qa = _dot(cq, wuq_ref[...])
    ka = _dot(ckv, wk_ref[...])
    va = _dot(ckv, wv_ref[...])

    for c in range(HQ // LANES):
        lo, hi = c * LANES, (c + 1) * LANES
        qn = _head_rms(yd[:, lo:hi], gdq_ref[...], bd)
        qdt_ref[lo:hi, :] = (_rope(qn, rope64_ref, DH // 4) * (S_H * LOG2E)).T.astype(BF16)
    kn = _head_rms(yd[:, HQ:HQ + LANES], gdk_ref[...], bd)
    kd_ref[...] = _rope(kn, rope64_ref, DH // 4).astype(BF16)
    _store_vt(vdt_ref, 0, yd[:, HQ + LANES:])

    yc = _dot(u, wc_ref[...])

    for c in range(HQ // LANES):
        lo, hi = c * LANES, (c + 1) * LANES
        pb_ref[:, lo:hi] = (_rope(yb[:, lo:hi], rope64_ref, DH // 4) * (S_H * LOG2E)).astype(BF16)
    pb_ref[:, HQ:HQ + LANES] = _rope(yb[:, HQ:HQ + LANES], rope64_ref, DH // 4).astype(BF16)
    pb_ref[:, HQ + LANES:] = yb[:, HQ + LANES:].astype(BF16)

    for hh in range(N_HEADS):
        lo, hi = hh * HEAD_PAD_A, (hh + 1) * HEAD_PAD_A
        q = _rope(qa[:, lo:hi], ropea_ref, ROPE_A // 4) * (S_A * LOG2E)
        qat_ref[lo:hi, :] = q.T.astype(BF16)
        ka_ref[:, lo:hi] = (ka[:, lo:hi] + kr).astype(BF16)
    for c in range(HQ // LANES):
        lo, hi = c * LANES, (c + 1) * LANES
        _store_vt(vat_ref, c, va[:, lo:hi])

    pc_ref[:, :HQ] = (yc[:, :HQ] * (S_H * LOG2E)).astype(BF16)
    pc_ref[:, HQ:] = yc[:, HQ:].astype(BF16)


def _project(h, mod, g_pre, w, tabs, n_batch, tiles_per_batch):
    m, d = h.shape
    s = tiles_per_batch * TM
    wq_a = N_HEADS * HEAD_PAD_A
    row = lambda i: (i, 0)
    col = lambda i: (i // tiles_per_batch, 0, i % tiles_per_batch)
    tile_pos = lambda i: (0, i % tiles_per_batch, 0)
    in_specs = [
        pl.BlockSpec((TM, d), row),
        _mod_spec(d, tiles_per_batch),
        _resident((1, d)),
        _resident(w["wa"].shape), _resident(w["wb"].shape), _resident(w["wc"].shape),
        _resident(w["wd"].shape), _resident(w["wuq"].shape),
        _resident(w["wk"].shape), _resident(w["wv"].shape),
        _resident((1, Q_LORA)), _resident((1, KV_LORA)), _resident((1, LANES)), _resident((1, LANES)),
        pl.BlockSpec((3, TM, LANES), tile_pos),
        pl.BlockSpec((3, TM, LANES), tile_pos),
    ]

    def rows_out(c):
        return pl.BlockSpec((TM, c), row), jax.ShapeDtypeStruct((m, c), BF16)

    def cols_out(r):
        return pl.BlockSpec((None, r, TM), col), jax.ShapeDtypeStruct((n_batch, r, s), BF16)

    outs = [rows_out(wq_a), cols_out(wq_a), cols_out(N_HEADS * VT_ROWS),
            rows_out(HQ + 2 * LANES), rows_out(3 * HQ),
            rows_out(LANES), cols_out(HQ), cols_out(KV_D * VT_ROWS)]
    return pl.pallas_call(
        _proj_kernel,
        grid=(m // TM,),
        in_specs=in_specs,
        out_specs=[o[0] for o in outs],
        out_shape=[o[1] for o in outs],
        compiler_params=_cparams(1),
        name="proj",
    )(h, mod, g_pre, w["wa"], w["wb"], w["wc"], w["wd"], w["wuq"], w["wk"], w["wv"],
      w["gaq"], w["gakv"], w["gdq"], w["gdk"], tabs["rope_a"], tabs["rope_64"])


def _dense_attn_kernel(k_ref, qt_ref, vt_ref, o_ref, ot_ref, st_ref, *, n_ctx, dk, q_first):
    qi = pl.program_id(1) + q_first
    group = 1 if dk == LANES else N_HEADS // (LANES // DH)

    def attend(n_keys):
        n_chunks = -(-n_keys // KEY_CHUNK)
        bounds = [(i * n_keys // n_chunks, (i + 1) * n_keys // n_chunks) for i in range(n_chunks)]

        def q_operand(h):
            if dk == LANES:
                return qt_ref[h * LANES:(h + 1) * LANES, :]
            qh = qt_ref[h * DH:(h + 1) * DH, :]
            z = jnp.zeros_like(qh)
            return jnp.concatenate([qh, z] if h // group == 0 else [z, qh], axis=0)

        def scores(h, qt, i):
            lanes = slice(h * LANES, (h + 1) * LANES) if dk == LANES else slice(0, LANES)
            lo, hi = bounds[i]
            st_ref[h % 2, lo:hi, :] = _dot(k_ref[lo:hi, lanes], qt)

        def colmax(h):
            m = None
            for lo, hi in bounds:
                mi = jnp.max(st_ref[h % 2, lo:hi, :], axis=0, keepdims=True)
                m = mi if m is None else jnp.maximum(m, mi)
            return m

        qt = q_operand(0)
        for i in range(n_chunks):
            scores(0, qt, i)
        for h in range(N_HEADS):
            g = h // group
            m = colmax(h)
            qt = q_operand(h + 1) if h + 1 < N_HEADS else None
            acc = None
            for i in range(n_chunks):
                if qt is not None:
                    scores(h + 1, qt, i)
                lo, hi = bounds[i]
                p = jnp.exp2(st_ref[h % 2, lo:hi, :] - m).astype(BF16)
                ai = _dot(vt_ref[g * VT_ROWS:(g + 1) * VT_ROWS, lo:hi], p)
                acc = ai if acc is None else acc + ai
            ot_ref[h * DH:(h + 1) * DH, :] = acc[:DH] / acc[DH:DH + 1]
        o_ref[...] = ot_ref[...].T.astype(o_ref.dtype)

    @pl.when(qi < n_ctx // TM)
    def _():
        attend(n_ctx)

    @pl.when(qi >= n_ctx // TM)
    def _():
        attend(k_ref.shape[0])


def _dense_attention(k, qt, vt, n_ctx, dk, q_first):
    b, s, kw = k.shape
    return pl.pallas_call(
        functools.partial(_dense_attn_kernel, n_ctx=n_ctx, dk=dk, q_first=q_first),
        grid=(b, s // TM - q_first),
        in_specs=[
            pl.BlockSpec((None, s, kw), lambda bi, qi: (bi, 0, 0)),
            pl.BlockSpec((None, qt.shape[1], TM), lambda bi, qi: (bi, 0, qi + q_first)),
            pl.BlockSpec((None, vt.shape[1], s), lambda bi, qi: (bi, 0, 0)),
        ],
        out_specs=pl.BlockSpec((None, TM, HQ), lambda bi, qi: (bi, qi + q_first, 0)),
        out_shape=jax.ShapeDtypeStruct((b, s, HQ), BF16),
        scratch_shapes=[pltpu.VMEM((HQ, TM), F32), pltpu.VMEM((2, s, TM), F32)],
        compiler_params=_cparams(2),
        name="attn_dense",
    )(k, qt, vt)


def _window_attn_kernel(sink_ref, q_ref, k_ref, v_ref, o_ref, s_ref, *, n_ctx, q_first):
    qi = pl.program_id(1) + q_first
    n_lat = k_ref.shape[0] - n_ctx
    span = TM + 2 * WINDOW
    blocks = list(range(HQ // LANES))

    def ctx_part():
        return k_ref[0:n_ctx, :], lambda: v_ref[0:n_ctx, :], None

    def finish(j, handle, slot):
        sinks = sink_ref[GQA_HEAD_ORDER[2 * j]], sink_ref[GQA_HEAD_ORDER[2 * j + 1]]
        o_ref[:, j * LANES:(j + 1) * LANES] = _pair_finish(handle, s_ref, slot, sinks).astype(o_ref.dtype)

    @pl.when(qi < n_ctx // TM)
    def _():
        def scores(j, slot):
            return _pair_scores(q_ref[:, j * LANES:(j + 1) * LANES], [ctx_part()], s_ref, slot)

        _pipelined(blocks, scores, finish)

    @pl.when(qi >= n_ctx // TM)
    def _():
        q0 = (qi - n_ctx // TM) * TM
        start = pl.multiple_of(jnp.clip(q0 - WINDOW, 0, n_lat - span), WINDOW)
        dist = (lax.broadcasted_iota(jnp.int32, (2 * TM, span), 0)
                - lax.broadcasted_iota(jnp.int32, (2 * TM, span), 1) + (q0 - start))
        dist = jnp.where(lax.broadcasted_iota(jnp.int32, (2 * TM, span), 0) >= TM, dist - TM, dist)
        bias = jnp.where(jnp.abs(dist) <= WINDOW, 0.0, NEG)
        rows = pl.ds(n_ctx + start, span)

        def scores(j, slot):
            return _pair_scores(q_ref[:, j * LANES:(j + 1) * LANES],
                                [(k_ref[rows, :], lambda: v_ref[rows, :], bias), ctx_part()], s_ref, slot)

        _pipelined(blocks, scores, finish)


def _nbr_attn_kernel(q_ref, k_ref, v_ref, tb_ref, o_ref, s_ref, *, n_ctx, q_first):
    qi = pl.program_id(1) + q_first
    n_lat = k_ref.shape[0] - n_ctx
    grid_rows = n_lat // GRID_W
    rows_per_tile = TM // GRID_W
    span = KH * GRID_W
    n_blk = HQ // LANES

    def ctx_part(blk):
        return k_ref[0:n_ctx, blk], lambda: v_ref[0:n_ctx, blk], None

    @pl.when(qi < n_ctx // TM)
    def _():
        def scores(j, slot):
            blk = slice(j * LANES, (j + 1) * LANES)
            return _pair_scores(q_ref[:, blk], [ctx_part(blk)], s_ref, slot)

        def finish(j, handle, slot):
            o_ref[:, j * LANES:(j + 1) * LANES] = _pair_finish(handle, s_ref, slot).astype(o_ref.dtype)

        _pipelined(list(range(n_blk)), scores, finish)

    @pl.when(qi >= n_ctx // TM)
    def _():
        def scores(item, slot):
            rr, j = item
            r = (qi - n_ctx // TM) * rows_per_tile + rr
            r0 = jnp.clip(r - KH // 2, 0, grid_rows - KH)
            delta = r - r0
            krows = pl.ds(pl.multiple_of(n_ctx + r0 * GRID_W, GRID_W), span)
            blk = slice(j * LANES, (j + 1) * LANES)
            bias = jnp.concatenate(
                [tb_ref[j, 2 * jj - delta + (KH - 1)] for jj in range(KH // 2)], axis=1)
            return _pair_scores(q_ref[rr * GRID_W:(rr + 1) * GRID_W, blk],
                                [(k_ref[krows, blk], lambda: v_ref[krows, blk], bias), ctx_part(blk)],
                                s_ref, slot)

        def finish(item, handle, slot):
            rr, j = item
            o_ref[rr * GRID_W:(rr + 1) * GRID_W, j * LANES:(j + 1) * LANES] = (
                _pair_finish(handle, s_ref, slot).astype(o_ref.dtype))

        _pipelined([(rr, j) for rr in range(rows_per_tile) for j in range(n_blk)], scores, finish)


def _pair_attention(kind, p, n_ctx, q_first, *, k_w, k_blk, sink=None, bias_tab=None):
    b, s, _ = p.shape
    in_specs = [
        pl.BlockSpec((None, TM, HQ), lambda bi, qi: (bi, qi + q_first, 0)),
        pl.BlockSpec((None, s, k_w), lambda bi, qi: (bi, 0, k_blk)),
        pl.BlockSpec((None, s, k_w), lambda bi, qi: (bi, 0, k_blk + 1)),
    ]
    args = [p, p, p]
    if kind == "window":
        body = functools.partial(_window_attn_kernel, n_ctx=n_ctx, q_first=q_first)
        in_specs = [pl.BlockSpec(memory_space=pltpu.SMEM)] + in_specs
        args = [sink] + args
    else:
        body = functools.partial(_nbr_attn_kernel, n_ctx=n_ctx, q_first=q_first)
        in_specs = in_specs + [_resident(bias_tab.shape)]
        args = args + [bias_tab]
    return pl.pallas_call(
        body,
        grid=(b, s // TM - q_first),
        in_specs=in_specs,
        out_specs=pl.BlockSpec((None, TM, HQ), lambda bi, qi: (bi, qi + q_first, 0)),
        out_shape=jax.ShapeDtypeStruct((b, s, HQ), BF16),
        scratch_shapes=[pltpu.VMEM((2, 2 * TM, max(TM + 2 * WINDOW, KH * GRID_W) + n_ctx), F32)],
        compiler_params=_cparams(2),
        name="attn_" + kind,
    )(*args)


def _merge_kernel(oa_ref, ob_ref, oc_ref, od_ref, h_ref, mod_ref, gpre_ref, gpost_ref,
                  wg_ref, wbr_ref, wout_ref, o_ref, acc_ref):
    d_model = h_ref.shape[1]
    cw = 256
    h = h_ref[...]
    sh1 = mod_ref[0:1, :]
    sc1 = mod_ref[1:2, :]
    ga1 = mod_ref[2:3, :]
    branches = (oa_ref, ob_ref, oc_ref, od_ref)

    def up_proj(n, c):
        return _dot(branches[n][...], wbr_ref[n, :, c * cw:(c + 1) * cw])

    up0 = [up_proj(n, 0) for n in range(N_BRANCH)]
    u = (_rms(h, gpre_ref[...]) * (1.0 + sc1) + sh1).astype(BF16)
    for c in range(d_model // cw):
        acc = None
        for n in range(N_BRANCH):
            lo = n * d_model + c * cw
            gate = _dot(u, wg_ref[:, lo:lo + cw])
            up = up0[n] if c == 0 else up_proj(n, c)
            term = _sigmoid(gate) * up
            acc = term if acc is None else acc + term
        acc_ref[:, c * cw:(c + 1) * cw] = acc.astype(BF16)
    y = _dot(acc_ref[...], wout_ref[...])
    o_ref[...] = h + ga1 * _rms(y, gpost_ref[...])


def _merge(o_a, o_b, o_c, o_d, h, mod, g_pre, g_post, wg, wbr, wout, tiles_per_batch, latent_only):
    m, d = h.shape
    row, n_tiles = _row_tiles(m, tiles_per_batch, latent_only)
    return pl.pallas_call(
        _merge_kernel,
        grid=(n_tiles,),
        in_specs=[pl.BlockSpec((TM, HQ), row)] * N_BRANCH + [
            pl.BlockSpec((TM, d), row),
            _mod_spec(d, tiles_per_batch, latent_only),
            _resident((1, d)), _resident((1, d)),
            _resident(wg.shape), _resident(wbr.shape), _resident(wout.shape),
        ],
        out_specs=pl.BlockSpec((TM, d), row),
        out_shape=jax.ShapeDtypeStruct((m, d), F32),
        scratch_shapes=[pltpu.VMEM((TM, d), BF16)],
        input_output_aliases={4: 0},
        compiler_params=_cparams(1),
        name="merge",
    )(o_a, o_b, o_c, o_d, h, mod, g_pre, g_post, wg, wbr, wout)


def _ffn_kernel(h_ref, mod_ref, gpre_ref, gpost_ref, w1_ref, w2_ref, o_ref, act_ref):
    d_ff = w2_ref.shape[0]
    cw = 256
    h = h_ref[...]
    sh2 = mod_ref[3:4, :]
    sc2 = mod_ref[4:5, :]
    ga2 = mod_ref[5:6, :]
    u = (_rms(h, gpre_ref[...]) * (1.0 + sc2) + sh2).astype(BF16)
    for c in range(d_ff // cw):
        a = _dot(u, w1_ref[:, c * cw:(c + 1) * cw])
        b = _dot(u, w1_ref[:, d_ff + c * cw:d_ff + (c + 1) * cw])
        act_ref[:, c * cw:(c + 1) * cw] = (a * _sigmoid(a) * b).astype(BF16)
    y = _dot(act_ref[...], w2_ref[...])
    o_ref[...] = h + ga2 * _rms(y, gpost_ref[...])


def _ffn(h, mod, g_pre, g_post, w1, w2, tiles_per_batch, latent_only):
    m, d = h.shape
    row, n_tiles = _row_tiles(m, tiles_per_batch, latent_only)
    return pl.pallas_call(
        _ffn_kernel,
        grid=(n_tiles,),
        in_specs=[
            pl.BlockSpec((TM, d), row),
            _mod_spec(d, tiles_per_batch, latent_only),
            _resident((1, d)), _resident((1, d)), _resident(w1.shape), _resident(w2.shape),
        ],
        out_specs=pl.BlockSpec((TM, d), lambda i: (i, 0)),
        out_shape=jax.ShapeDtypeStruct((n_tiles * TM, d), F32),
        scratch_shapes=[pltpu.VMEM((TM, w2.shape[0]), BF16)],
        input_output_aliases={} if latent_only else {0: 0},
        compiler_params=_cparams(1),
        name="ffn",
    )(h, mod, g_pre, g_post, w1, w2)


def _rope_tables(n_ctx, n_lat):
    t = jnp.arange(n_lat)
    rows_pos = (t // GRID_W).astype(F32)
    cols_pos = (t % GRID_W).astype(F32)

    def block(d):
        q = d // 4
        inv = ROPE_THETA ** (-jnp.arange(q, dtype=F32) * 2.0 / (d // 2))
        ar = rows_pos[:, None] * inv[None, :]
        ac = cols_pos[:, None] * inv[None, :]
        z = jnp.zeros_like(ar)
        cos = jnp.concatenate([jnp.cos(ar), jnp.cos(ar), jnp.cos(ac), jnp.cos(ac)], axis=1)
        up = jnp.concatenate([-jnp.sin(ar), z, -jnp.sin(ac), z], axis=1)
        dn = jnp.concatenate([z, jnp.sin(ar), z, jnp.sin(ac)], axis=1)
        return cos, up, dn

    def with_ctx(tab, fill):
        return jnp.concatenate([jnp.full((n_ctx, tab.shape[1]), fill, F32), tab], axis=0)

    c64, u64, d64 = block(DH)
    rope_64 = jnp.stack([with_ctx(jnp.tile(c64, (1, LANES // DH)), 1.0),
                         with_ctx(jnp.tile(u64, (1, LANES // DH)), 0.0),
                         with_ctx(jnp.tile(d64, (1, LANES // DH)), 0.0)])
    ca, ua, da = block(ROPE_A)
    pad_lo = NOPE_A
    pad_hi = HEAD_PAD_A - NOPE_A - ROPE_A

    def widen(tab, fill):
        return jnp.concatenate([jnp.full((n_lat, pad_lo), fill, F32), tab,
                                jnp.full((n_lat, pad_hi), fill, F32)], axis=1)

    rope_a = jnp.stack([with_ctx(widen(ca, 1.0), 1.0), with_ctx(widen(ua, 0.0), 0.0),
                        with_ctx(widen(da, 0.0), 0.0)])
    return {"rope_a": rope_a, "rope_64": rope_64}


def _permute_heads(w, axis, order):
    shape = w.shape
    w = w.reshape(shape[:axis] + (N_HEADS, DH) + shape[axis + 1:])
    w = jnp.take(w, jnp.asarray(order), axis=axis)
    return w.reshape(shape)


def _layer_weights(l, w_in, g_a_q, g_a_kv, w_a_uq, w_a_ukv, g_d_q, g_d_k, w_branch):
    d = w_in.shape[1]
    wl = w_in[l]
    o = 0
    a_cq = wl[:, o:o + Q_LORA]; o += Q_LORA
    a_ckv = wl[:, o:o + KV_LORA]; o += KV_LORA
    a_kr = wl[:, o:o + ROPE_A]; o += ROPE_A
    w_bq = wl[:, o:o + HQ]; o += HQ
    w_bkv = wl[:, o:o + 2 * KV_B * DH]; o += 2 * KV_B * DH
    w_c = wl[:, o:o + 3 * HQ]; o += 3 * HQ
    w_d = wl[:, o:o + HQ + 2 * KV_D * DH]; o += HQ + 2 * KV_D * DH
    w_g = wl[:, o:]
    zeros = lambda n: jnp.zeros((d, n), wl.dtype)
    wa = jnp.concatenate([a_cq, a_ckv, zeros(NOPE_A), a_kr, zeros(HEAD_PAD_A - NOPE_A - ROPE_A)], axis=1)
    w_b = jnp.concatenate([_permute_heads(w_bq, 1, GQA_HEAD_ORDER), w_bkv], axis=1)
    uq = w_a_uq[l].reshape(Q_LORA, N_HEADS, NOPE_A + ROPE_A)
    uq = jnp.pad(uq, ((0, 0), (0, 0), (0, HEAD_PAD_A - NOPE_A - ROPE_A))).reshape(Q_LORA, N_HEADS * HEAD_PAD_A)
    ukv = w_a_ukv[l].reshape(KV_LORA, N_HEADS, NOPE_A + DH)
    wk = jnp.pad(ukv[:, :, :NOPE_A], ((0, 0), (0, 0), (0, HEAD_PAD_A - NOPE_A))).reshape(KV_LORA, N_HEADS * HEAD_PAD_A)
    wv = ukv[:, :, NOPE_A:].reshape(KV_LORA, HQ)
    wbr = w_branch[l]
    wbr = jnp.stack([wbr[0], _permute_heads(wbr[1], 0, GQA_HEAD_ORDER), wbr[2], wbr[3]])
    cast = lambda x: x.astype(BF16)
    return {
        "wa": cast(wa), "wb": cast(w_b), "wc": cast(w_c), "wd": cast(w_d), "wg": cast(w_g),
        "wuq": cast(uq), "wk": cast(wk), "wv": cast(wv), "wbr": cast(wbr),
        "gaq": g_a_q[l][None, :], "gakv": g_a_kv[l][None, :],
        "gdq": jnp.tile(g_d_q[l], LANES // DH)[None, :], "gdk": jnp.tile(g_d_k[l], LANES // DH)[None, :],
    }


def kernel(x, c, ctx, c_ctx, w_mod, b_mod, g_pre_mix, g_post_mix, g_pre_ffn, g_post_ffn,
           w_in, g_a_q, g_a_kv, w_a_uq, w_a_ukv, sink_b, rpb_c, g_d_q, g_d_k,
           w_branch, w_out, w_ffn_in, w_ffn_out):
    b, n_lat, d = x.shape
    n_ctx = ctx.shape[1]
    s = n_ctx + n_lat
    depth = w_mod.shape[0]
    assert n_ctx == TM and n_lat % TM == 0 and n_lat >= TM + 2 * WINDOW
    assert n_lat % GRID_W == 0 and n_lat // GRID_W >= KH
    assert KV_B * DH == LANES and KV_D * DH == LANES
    tiles_per_batch = s // TM

    n_rows = -(-(b + 1) // F32_SUBLANES) * F32_SUBLANES
    cc = jnp.zeros((n_rows, d), F32).at[:b].set(c).at[b].set(c_ctx)
    mods = _modulation(cc, w_mod, b_mod)
    mod_x = mods[:, :b].reshape(depth, b, 1, MOD_CHUNKS, d)
    mod_c = jnp.broadcast_to(mods[:, b].reshape(depth, 1, 1, MOD_CHUNKS, d), mod_x.shape)
    mod_tab = jnp.concatenate([mod_c, mod_x], axis=2)

    tabs = _rope_tables(n_ctx, n_lat)
    h = jnp.concatenate([ctx, x], axis=1).reshape(b * s, d)

    for l in range(depth):
        w = _layer_weights(l, w_in, g_a_q, g_a_kv, w_a_uq, w_a_ukv, g_d_q, g_d_k, w_branch)
        mod = mod_tab[l]
        g_pre = g_pre_mix[l][None, :]
        last = l == depth - 1
        q_first = n_ctx // TM if last else 0
        ka, qat, vat, pb, pc, kd, qdt, vdt = _project(h, mod, g_pre, w, tabs, b, tiles_per_batch)
        o_a = _dense_attention(ka.reshape(b, s, -1), qat, vat, n_ctx, HEAD_PAD_A, q_first)
        o_b = _pair_attention("window", pb.reshape(b, s, -1), n_ctx, q_first, k_w=LANES, k_blk=HQ // LANES,
                              sink=sink_b[l])
        o_c = _pair_attention("nbr", pc.reshape(b, s, -1), n_ctx, q_first, k_w=HQ, k_blk=1,
                              bias_tab=_nbr_bias(rpb_c[l]))
        o_d = _dense_attention(kd.reshape(b, s, -1), qdt, vdt, n_ctx, DH, q_first)
        flat = lambda t: t.reshape(b * s, HQ)
        h = _merge(flat(o_a), flat(o_b), flat(o_c), flat(o_d), h, mod, g_pre, g_post_mix[l][None, :],
                   w["wg"], w["wbr"], w_out[l].astype(BF16), tiles_per_batch, last)
        h = _ffn(h, mod, g_pre_ffn[l][None, :], g_post_ffn[l][None, :],
                 w_ffn_in[l].astype(BF16), w_ffn_out[l].astype(BF16), tiles_per_batch, last)
    return h.reshape(b, n_lat, d)
```

```python
import functools

import jax
import jax.numpy as jnp
from jax import lax
from jax.experimental import pallas as pl
from jax.experimental.pallas import tpu as pltpu

F32 = jnp.float32
BF16 = jnp.bfloat16
UINT = jnp.uint32

GRID_W = 64
ROPE_THETA = 10000.0
EPS = 1e-6
NEG = -1e30
DH = 64
N_HEADS = 8
N_BRANCH = 4
Q_LORA = 256
KV_LORA = 128
NOPE_A = 64
ROPE_A = 32
KV_B = 2
WINDOW = 128
KH = 8
KW = 16
KV_D = 2
MOD_CHUNKS = 6

LANES = 128
F32_SUBLANES = 8
VMEM_LIMIT_BYTES = 56 * 1024 * 1024

TM = 256
HEAD_PAD_A = 128
HQ = N_HEADS * DH
S_A = (NOPE_A + ROPE_A) ** -0.5
S_H = DH ** -0.5
LOG2E = 1.4426950408889634
KEY_CHUNK = 1152
ONES_ROWS = 16
VT_ROWS = DH + ONES_ROWS
GQA_HEAD_ORDER = tuple(h for j in range(N_HEADS // 2) for h in (j, j + N_HEADS // 2))


def _cparams(n_axes):
    return pltpu.CompilerParams(
        dimension_semantics=("arbitrary",) * n_axes, vmem_limit_bytes=VMEM_LIMIT_BYTES)


def _resident(shape):
    nd = len(shape)
    return pl.BlockSpec(shape, lambda *_: (0,) * nd, pipeline_mode=pl.Buffered(1))


def _tile_visits(m, tiles_per_batch, latent_only):
    lat = tiles_per_batch - 1
    if latent_only:
        return (m // TM // tiles_per_batch * lat,
                lambda j: (j // lat) * tiles_per_batch + 1 + j % lat,
                lambda j: (j // lat, 1))
    return (m // TM, lambda j: j,
            lambda j: (j // tiles_per_batch, jnp.minimum(j % tiles_per_batch, 1)))


class _StepPipeline:
    def __init__(self, n, row_of, mod_of, d):
        nxt = lambda i: jnp.minimum(i + 1, n - 1)
        prv = lambda i: jnp.maximum(i - 1, 0)
        mod_block = (None, None, MOD_CHUNKS, d)
        self.rows_next = pl.BlockSpec((TM, d), lambda i: (row_of(nxt(i)), 0))
        self.rows_prev = pl.BlockSpec((TM, d), lambda i: (row_of(prv(i)), 0))
        self.mod_next = pl.BlockSpec(mod_block, lambda i: (*mod_of(nxt(i)), 0, 0))
        self.mod_prev = pl.BlockSpec(mod_block, lambda i: (*mod_of(prv(i)), 0, 0))
        self.rows_out = pl.BlockSpec((TM, d), lambda i: (prv(i), 0))


def _mod_spec(d, tiles_per_batch):
    return pl.BlockSpec((None, None, MOD_CHUNKS, d),
                        lambda i: (i // tiles_per_batch, jnp.minimum(i % tiles_per_batch, 1), 0, 0))


def _dot(a, b):
    return jnp.dot(a, b, preferred_element_type=F32)


def _dot_nt(a, b):
    return lax.dot_general(a, b, (((1,), (1,)), ((), ())), preferred_element_type=F32)


def _rms(x, g):
    ms = jnp.mean(x * x, axis=-1, keepdims=True)
    return x * lax.rsqrt(ms + EPS) * g


def _sigmoid(x):
    return 1.0 / (1.0 + jnp.exp(-x))


def _zero_after(*tiles):
    r = None
    for t in tiles:
        b = pltpu.bitcast(t, UINT)
        acc = b[:, 0:LANES]
        for c in range(1, b.shape[1] // LANES):
            acc = acc | b[:, c * LANES:(c + 1) * LANES]
        fold = acc[0:F32_SUBLANES]
        for k in range(1, acc.shape[0] // F32_SUBLANES):
            fold = fold | acc[k * F32_SUBLANES:(k + 1) * F32_SUBLANES]
        r = fold if r is None else r | fold
    half = jnp.iinfo(UINT).bits // 2
    z = lax.shift_right_logical(lax.shift_right_logical(r, jnp.array(half, UINT)), jnp.array(half, UINT))
    return pltpu.bitcast(z, F32)


def _rope(x, tab_ref, shift):
    n = x.shape[-1]
    return (x * tab_ref[0] + pltpu.roll(x, n - shift, 1) * tab_ref[1]
            + pltpu.roll(x, shift, 1) * tab_ref[2])


def _head_rms(x, g, bd):
    x2 = x * x
    hi = x2.astype(BF16)
    lo = (x2 - hi.astype(F32)).astype(BF16)
    ss = _dot(hi, bd) + _dot(lo, bd)
    return x * lax.rsqrt(ss * (1.0 / DH) + EPS) * g


def _pair_scores(q_blk, parts, s_ref, slot):
    m = q_blk.shape[0]
    lane = lax.broadcasted_iota(jnp.int32, (m, LANES), 1)
    zero = jnp.zeros_like(q_blk)
    qs = jnp.concatenate([jnp.where(lane < DH, q_blk, zero), jnp.where(lane >= DH, q_blk, zero)], axis=0)
    layout = []
    off = 0
    for k, load_v, bias in parts:
        n = k.shape[0]
        s = _dot_nt(qs, k)
        s_ref[slot, 0:2 * m, off:off + n] = s if bias is None else s + bias
        layout.append((off, n, load_v))
        off += n
    return m, layout


def _pair_finish(handle, s_ref, slot, sinks=None):
    m, layout = handle
    width = sum(n for _, n, _ in layout)
    s = s_ref[slot, 0:2 * m, 0:width]
    mx = jnp.max(s, axis=1, keepdims=True)
    sink = None
    if sinks is not None:
        row = lax.broadcasted_iota(jnp.int32, (2 * m, 1), 0)
        sink = jnp.where(row < m, sinks[0], sinks[1]) * LOG2E
        mx = jnp.maximum(mx, sink)
    p = jnp.exp2(s - mx)
    l = jnp.sum(p, axis=1, keepdims=True)
    if sink is not None:
        l = l + jnp.exp2(sink - mx)
    pb = p.astype(BF16)
    acc = None
    for off, n, load_v in layout:
        a = _dot(pb[:, off:off + n], load_v())
        acc = a if acc is None else acc + a
    o = acc / l
    lane = lax.broadcasted_iota(jnp.int32, (m, LANES), 1)
    return jnp.where(lane < DH, o[:m], o[m:])


def _pipelined(items, scores_fn, finish_fn):
    cur = scores_fn(items[0], 0)
    for n, item in enumerate(items):
        nxt = scores_fn(items[n + 1], (n + 1) % 2) if n + 1 < len(items) else None
        finish_fn(item, cur, n % 2)
        cur = nxt


def _mod_kernel(c_ref, w_ref, b_ref, o_ref):
    c = c_ref[...]
    sc = (c * _sigmoid(c)).astype(BF16)
    o_ref[...] = _dot(sc, w_ref[...].astype(BF16)) + b_ref[...]


def _modulation(cc, w_mod, b_mod):
    n_layers, d, n = w_mod.shape
    r = cc.shape[0]
    tn = n // 4
    return pl.pallas_call(
        _mod_kernel,
        grid=(n_layers, n // tn),
        in_specs=[
            pl.BlockSpec((r, d), lambda l, j: (0, 0)),
            pl.BlockSpec((None, d, tn), lambda l, j: (l, 0, j)),
            pl.BlockSpec((None, 1, tn), lambda l, j: (l, 0, j)),
        ],
        out_specs=pl.BlockSpec((None, r, tn), lambda l, j: (l, 0, j)),
        out_shape=jax.ShapeDtypeStruct((n_layers, r, n), F32),
        compiler_params=_cparams(2),
        name="modulation",
    )(cc, w_mod, b_mod.reshape(n_layers, 1, n))


def _nbr_bias_kernel(rpb_ref, o_ref):
    pair = pl.program_id(0)
    n_dr = 2 * KH - 1
    n_dc = 2 * KW - 1
    qc = lax.broadcasted_iota(jnp.int32, (GRID_W, 2 * GRID_W), 0)
    lane = lax.broadcasted_iota(jnp.int32, (GRID_W, 2 * GRID_W), 1)
    second = lane >= GRID_W
    kc = jnp.where(second, lane - GRID_W, lane)
    c0 = jnp.clip(qc - KW // 2, 0, GRID_W - KW)
    ok = (kc >= c0) & (kc < c0 + KW)
    dc = kc - qc + (KW - 1)

    for e in range(2):
        h = 2 * pair + e

        def d2_body(d2, carry, h=h, e=e):
            def d_body(d, acc):
                v0 = rpb_ref[(h * n_dr + d2) * n_dc + d]
                v1 = rpb_ref[(h * n_dr + d2 + 1) * n_dc + d]
                return jnp.where(dc == d, jnp.where(second, v1, v0), acc)

            acc = lax.fori_loop(0, n_dc, d_body, jnp.zeros((GRID_W, 2 * GRID_W), F32))
            o_ref[d2, e * GRID_W:(e + 1) * GRID_W, :] = jnp.where(ok, acc * LOG2E, NEG)
            return carry

        lax.fori_loop(0, n_dr - 1, d2_body, 0)


def _nbr_bias(rpb):
    h = rpb.shape[0]
    n_pairs = 2 * KH - 2
    return pl.pallas_call(
        _nbr_bias_kernel,
        grid=(h // 2,),
        in_specs=[pl.BlockSpec(memory_space=pltpu.SMEM)],
        out_specs=pl.BlockSpec((None, n_pairs, 2 * GRID_W, 2 * GRID_W), lambda i: (i, 0, 0, 0)),
        out_shape=jax.ShapeDtypeStruct((h // 2, n_pairs, 2 * GRID_W, 2 * GRID_W), F32),
        compiler_params=_cparams(1),
        name="nbr_bias",
    )(rpb.reshape(-1))


def _store_vt(vt_ref, c, v):
    t = v.T.astype(BF16)
    ones = jnp.ones((ONES_ROWS, v.shape[0]), BF16)
    for e in range(LANES // DH):
        base = (2 * c + e) * VT_ROWS
        vt_ref[base:base + DH, :] = t[e * DH:(e + 1) * DH, :]
        vt_ref[base + DH:base + VT_ROWS, :] = ones


def _proj_kernel(h_ref, mod_ref, g_ref, wa_ref, wb_ref, wc_ref, wd_ref, wuq_ref, wk_ref, wv_ref,
                 gaq_ref, gakv_ref, gdq_ref, gdk_ref, ropea_ref, rope64_ref,
                 ka_ref, qat_ref, vat_ref, pb_ref, pc_ref, kd_ref, qdt_ref, vdt_ref):
    sh1 = mod_ref[0:1, :]
    sc1 = mod_ref[1:2, :]
    u = (_rms(h_ref[...], g_ref[...]) * (1.0 + sc1) + sh1).astype(BF16)

    rb = jnp.where(lax.broadcasted_iota(jnp.int32, (LANES, LANES), 0) < DH, 0, 1)
    cb = jnp.where(lax.broadcasted_iota(jnp.int32, (LANES, LANES), 1) < DH, 0, 1)
    bd = jnp.where(rb == cb, 1.0, 0.0).astype(BF16)

    ya = _dot(u, wa_ref[...])
    yd = _dot(u, wd_ref[...])
    cq = _rms(ya[:, :Q_LORA], gaq_ref[...]).astype(BF16)
    ckv = _rms(ya[:, Q_LORA:Q_LORA + KV_LORA], gakv_ref[...]).astype(BF16)
    kr = _rope(ya[:, Q_LORA + KV_LORA:], ropea_ref, ROPE_A // 4)
    yb = _dot(u, wb_ref[...])
    qa = _dot(cq, wuq_ref[...])
    ka = _dot(ckv, wk_ref[...])
    va = _dot(ckv, wv_ref[...])

    for c in range(HQ // LANES):
        lo, hi = c * LANES, (c + 1) * LANES
        qn = _head_rms(yd[:, lo:hi], gdq_ref[...], bd)
        qdt_ref[lo:hi, :] = (_rope(qn, rope64_ref, DH // 4) * (S_H * LOG2E)).T.astype(BF16)
    kn = _head_rms(yd[:, HQ:HQ + LANES], gdk_ref[...], bd)
    kd_ref[...] = _rope(kn, rope64_ref, DH // 4).astype(BF16)
    _store_vt(vdt_ref, 0, yd[:, HQ + LANES:])

    yc = _dot(u, wc_ref[...])

    for c in range(HQ // LANES):
        lo, hi = c * LANES, (c + 1) * LANES
        pb_ref[:, lo:hi] = (_rope(yb[:, lo:hi], rope64_ref, DH // 4) * (S_H * LOG2E)).astype(BF16)
    pb_ref[:, HQ:HQ + LANES] = _rope(yb[:, HQ:HQ + LANES], rope64_ref, DH // 4).astype(BF16)
    pb_ref[:, HQ + LANES:] = yb[:, HQ + LANES:].astype(BF16)

    for hh in range(N_HEADS):
        lo, hi = hh * HEAD_PAD_A, (hh + 1) * HEAD_PAD_A
        q = _rope(qa[:, lo:hi], ropea_ref, ROPE_A // 4) * (S_A * LOG2E)
        qat_ref[lo:hi, :] = q.T.astype(BF16)
        ka_ref[:, lo:hi] = (ka[:, lo:hi] + kr).astype(BF16)
    for c in range(HQ // LANES):
        lo, hi = c * LANES, (c + 1) * LANES
        _store_vt(vat_ref, c, va[:, lo:hi])

    pc_ref[:, :HQ] = (yc[:, :HQ] * (S_H * LOG2E)).astype(BF16)
    pc_ref[:, HQ:] = yc[:, HQ:].astype(BF16)


def _project(h, mod, g_pre, w, tabs, n_batch, tiles_per_batch):
    m, d = h.shape
    s = tiles_per_batch * TM
    wq_a = N_HEADS * HEAD_PAD_A
    row = lambda i: (i, 0)
    col = lambda i: (i // tiles_per_batch, 0, i % tiles_per_batch)
    tile_pos = lambda i: (0, i % tiles_per_batch, 0)
    in_specs = [
        pl.BlockSpec((TM, d), row),
        _mod_spec(d, tiles_per_batch),
        _resident((1, d)),
        _resident(w["wa"].shape), _resident(w["wb"].shape), _resident(w["wc"].shape),
        _resident(w["wd"].shape), _resident(w["wuq"].shape),
        _resident(w["wk"].shape), _resident(w["wv"].shape),
        _resident((1, Q_LORA)), _resident((1, KV_LORA)), _resident((1, LANES)), _resident((1, LANES)),
        pl.BlockSpec((3, TM, LANES), tile_pos),
        pl.BlockSpec((3, TM, LANES), tile_pos),
    ]

    def rows_out(c):
        return pl.BlockSpec((TM, c), row), jax.ShapeDtypeStruct((m, c), BF16)

    def cols_out(r):
        return pl.BlockSpec((None, r, TM), col), jax.ShapeDtypeStruct((n_batch, r, s), BF16)

    outs = [rows_out(wq_a), cols_out(wq_a), cols_out(N_HEADS * VT_ROWS),
            rows_out(HQ + 2 * LANES), rows_out(3 * HQ),
            rows_out(LANES), cols_out(HQ), cols_out(KV_D * VT_ROWS)]
    return pl.pallas_call(
        _proj_kernel,
        grid=(m // TM,),
        in_specs=in_specs,
        out_specs=[o[0] for o in outs],
        out_shape=[o[1] for o in outs],
        compiler_params=_cparams(1),
        name="proj",
    )(h, mod, g_pre, w["wa"], w["wb"], w["wc"], w["wd"], w["wuq"], w["wk"], w["wv"],
      w["gaq"], w["gakv"], w["gdq"], w["gdk"], tabs["rope_a"], tabs["rope_64"])


def _dense_attn_kernel(k_ref, qt_ref, vt_ref, o_ref, ot_ref, st_ref, *, n_ctx, dk, q_first):
    qi = pl.program_id(1) + q_first
    group = 1 if dk == LANES else N_HEADS // (LANES // DH)

    def attend(n_keys):
        n_chunks = -(-n_keys // KEY_CHUNK)
        bounds = [(i * n_keys // n_chunks, (i + 1) * n_keys // n_chunks) for i in range(n_chunks)]

        def q_operand(h):
            if dk == LANES:
                return qt_ref[h * LANES:(h + 1) * LANES, :]
            qh = qt_ref[h * DH:(h + 1) * DH, :]
            z = jnp.zeros_like(qh)
            return jnp.concatenate([qh, z] if h // group == 0 else [z, qh], axis=0)

        def scores(h, qt, i):
            lanes = slice(h * LANES, (h + 1) * LANES) if dk == LANES else slice(0, LANES)
            lo, hi = bounds[i]
            st_ref[h % 2, lo:hi, :] = _dot(k_ref[lo:hi, lanes], qt)

        def colmax(h):
            m = None
            for lo, hi in bounds:
                mi = jnp.max(st_ref[h % 2, lo:hi, :], axis=0, keepdims=True)
                m = mi if m is None else jnp.maximum(m, mi)
            return m

        qt = q_operand(0)
        for i in range(n_chunks):
            scores(0, qt, i)
        for h in range(N_HEADS):
            g = h // group
            m = colmax(h)
            qt = q_operand(h + 1) if h + 1 < N_HEADS else None
            acc = None
            for i in range(n_chunks):
                if qt is not None:
                    scores(h + 1, qt, i)
                lo, hi = bounds[i]
                p = jnp.exp2(st_ref[h % 2, lo:hi, :] - m).astype(BF16)
                ai = _dot(vt_ref[g * VT_ROWS:(g + 1) * VT_ROWS, lo:hi], p)
                acc = ai if acc is None else acc + ai
            ot_ref[h * DH:(h + 1) * DH, :] = acc[:DH] / acc[DH:DH + 1]
        o_ref[...] = ot_ref[...].T.astype(o_ref.dtype)

    @pl.when(qi < n_ctx // TM)
    def _():
        attend(n_ctx)

    @pl.when(qi >= n_ctx // TM)
    def _():
        attend(k_ref.shape[0])


def _dense_attention(k, qt, vt, n_ctx, dk, q_first):
    b, s, kw = k.shape
    return pl.pallas_call(
        functools.partial(_dense_attn_kernel, n_ctx=n_ctx, dk=dk, q_first=q_first),
        grid=(b, s // TM - q_first),
        in_specs=[
            pl.BlockSpec((None, s, kw), lambda bi, qi: (bi, 0, 0)),
            pl.BlockSpec((None, qt.shape[1], TM), lambda bi, qi: (bi, 0, qi + q_first)),
            pl.BlockSpec((None, vt.shape[1], s), lambda bi, qi: (bi, 0, 0)),
        ],
        out_specs=pl.BlockSpec((None, TM, HQ), lambda bi, qi: (bi, qi + q_first, 0)),
        out_shape=jax.ShapeDtypeStruct((b, s, HQ), BF16),
        scratch_shapes=[pltpu.VMEM((HQ, TM), F32), pltpu.VMEM((2, s, TM), F32)],
        compiler_params=_cparams(2),
        name="attn_dense",
    )(k, qt, vt)


def _window_attn_kernel(sink_ref, q_ref, k_ref, v_ref, o_ref, s_ref, *, n_ctx, q_first):
    qi = pl.program_id(1) + q_first
    n_lat = k_ref.shape[0] - n_ctx
    span = TM + 2 * WINDOW
    blocks = list(range(HQ // LANES))

    def ctx_part():
        return k_ref[0:n_ctx, :], lambda: v_ref[0:n_ctx, :], None

    def finish(j, handle, slot):
        sinks = sink_ref[GQA_HEAD_ORDER[2 * j]], sink_ref[GQA_HEAD_ORDER[2 * j + 1]]
        o_ref[:, j * LANES:(j + 1) * LANES] = _pair_finish(handle, s_ref, slot, sinks).astype(o_ref.dtype)

    @pl.when(qi < n_ctx // TM)
    def _():
        def scores(j, slot):
            return _pair_scores(q_ref[:, j * LANES:(j + 1) * LANES], [ctx_part()], s_ref, slot)

        _pipelined(blocks, scores, finish)

    @pl.when(qi >= n_ctx // TM)
    def _():
        q0 = (qi - n_ctx // TM) * TM
        start = pl.multiple_of(jnp.clip(q0 - WINDOW, 0, n_lat - span), WINDOW)
        dist = (lax.broadcasted_iota(jnp.int32, (2 * TM, span), 0)
                - lax.broadcasted_iota(jnp.int32, (2 * TM, span), 1) + (q0 - start))
        dist = jnp.where(lax.broadcasted_iota(jnp.int32, (2 * TM, span), 0) >= TM, dist - TM, dist)
        bias = jnp.where(jnp.abs(dist) <= WINDOW, 0.0, NEG)
        rows = pl.ds(n_ctx + start, span)

        def scores(j, slot):
            return _pair_scores(q_ref[:, j * LANES:(j + 1) * LANES],
                                [(k_ref[rows, :], lambda: v_ref[rows, :], bias), ctx_part()], s_ref, slot)

        _pipelined(blocks, scores, finish)


def _nbr_attn_kernel(q_ref, k_ref, v_ref, tb_ref, o_ref, s_ref, *, n_ctx, q_first):
    qi = pl.program_id(1) + q_first
    n_lat = k_ref.shape[0] - n_ctx
    grid_rows = n_lat // GRID_W
    rows_per_tile = TM // GRID_W
    span = KH * GRID_W
    n_blk = HQ // LANES

    def ctx_part(blk):
        return k_ref[0:n_ctx, blk], lambda: v_ref[0:n_ctx, blk], None

    @pl.when(qi < n_ctx // TM)
    def _():
        def scores(j, slot):
            blk = slice(j * LANES, (j + 1) * LANES)
            return _pair_scores(q_ref[:, blk], [ctx_part(blk)], s_ref, slot)

        def finish(j, handle, slot):
            o_ref[:, j * LANES:(j + 1) * LANES] = _pair_finish(handle, s_ref, slot).astype(o_ref.dtype)

        _pipelined(list(range(n_blk)), scores, finish)

    @pl.when(qi >= n_ctx // TM)
    def _():
        def scores(item, slot):
            rr, j = item
            r = (qi - n_ctx // TM) * rows_per_tile + rr
            r0 = jnp.clip(r - KH // 2, 0, grid_rows - KH)
            delta = r - r0
            krows = pl.ds(pl.multiple_of(n_ctx + r0 * GRID_W, GRID_W), span)
            blk = slice(j * LANES, (j + 1) * LANES)
            bias = jnp.concatenate(
                [tb_ref[j, 2 * jj - delta + (KH - 1)] for jj in range(KH // 2)], axis=1)
            return _pair_scores(q_ref[rr * GRID_W:(rr + 1) * GRID_W, blk],
                                [(k_ref[krows, blk], lambda: v_ref[krows, blk], bias), ctx_part(blk)],
                                s_ref, slot)

        def finish(item, handle, slot):
            rr, j = item
            o_ref[rr * GRID_W:(rr + 1) * GRID_W, j * LANES:(j + 1) * LANES] = (
                _pair_finish(handle, s_ref, slot).astype(o_ref.dtype))

        _pipelined([(rr, j) for rr in range(rows_per_tile) for j in range(n_blk)], scores, finish)


def _pair_attention(kind, p, n_ctx, q_first, *, k_w, k_blk, sink=None, bias_tab=None):
    b, s, _ = p.shape
    in_specs = [
        pl.BlockSpec((None, TM, HQ), lambda bi, qi: (bi, qi + q_first, 0)),
        pl.BlockSpec((None, s, k_w), lambda bi, qi: (bi, 0, k_blk)),
        pl.BlockSpec((None, s, k_w), lambda bi, qi: (bi, 0, k_blk + 1)),
    ]
    args = [p, p, p]
    if kind == "window":
        body = functools.partial(_window_attn_kernel, n_ctx=n_ctx, q_first=q_first)
        in_specs = [pl.BlockSpec(memory_space=pltpu.SMEM)] + in_specs
        args = [sink] + args
    else:
        body = functools.partial(_nbr_attn_kernel, n_ctx=n_ctx, q_first=q_first)
        in_specs = in_specs + [_resident(bias_tab.shape)]
        args = args + [bias_tab]
    return pl.pallas_call(
        body,
        grid=(b, s // TM - q_first),
        in_specs=in_specs,
        out_specs=pl.BlockSpec((None, TM, HQ), lambda bi, qi: (bi, qi + q_first, 0)),
        out_shape=jax.ShapeDtypeStruct((b, s, HQ), BF16),
        scratch_shapes=[pltpu.VMEM((2, 2 * TM, max(TM + 2 * WINDOW, KH * GRID_W) + n_ctx), F32)],
        compiler_params=_cparams(2),
        name="attn_" + kind,
    )(*args)


def _merge_kernel(oa_ref, ob_ref, oc_ref, od_ref, h_ref, mod_ref, gpre_ref, gpost_ref,
                  wg_ref, wbr_ref, wout_ref, o_ref, acc_ref):
    d_model = h_ref.shape[1]
    cw = 256
    h = h_ref[...]
    sh1 = mod_ref[0:1, :]
    sc1 = mod_ref[1:2, :]
    ga1 = mod_ref[2:3, :]
    branches = (oa_ref, ob_ref, oc_ref, od_ref)

    def up_proj(n, c):
        return _dot(branches[n][...], wbr_ref[n, :, c * cw:(c + 1) * cw])

    up0 = [up_proj(n, 0) for n in range(N_BRANCH)]
    u = (_rms(h, gpre_ref[...]) * (1.0 + sc1) + sh1).astype(BF16)
    for c in range(d_model // cw):
        acc = None
        for n in range(N_BRANCH):
            lo = n * d_model + c * cw
            gate = _dot(u, wg_ref[:, lo:lo + cw])
            up = up0[n] if c == 0 else up_proj(n, c)
            term = _sigmoid(gate) * up
            acc = term if acc is None else acc + term
        acc_ref[:, c * cw:(c + 1) * cw] = acc.astype(BF16)
    y = _dot(acc_ref[...], wout_ref[...])
    o_ref[...] = h + ga1 * _rms(y, gpost_ref[...])


def _merge(o_a, o_b, o_c, o_d, h, mod, g_pre, g_post, wg, wbr, wout, tiles_per_batch, latent_only):
    m, d = h.shape
    n, row_of, mod_of = _tile_visits(m, tiles_per_batch, latent_only)
    row = lambda i: (row_of(i), 0)
    return pl.pallas_call(
        _merge_kernel,
        grid=(n,),
        in_specs=[pl.BlockSpec((TM, HQ), row)] * N_BRANCH + [
            pl.BlockSpec((TM, d), row),
            pl.BlockSpec((None, None, MOD_CHUNKS, d), lambda i: (*mod_of(i), 0, 0)),
            _resident((1, d)), _resident((1, d)),
            _resident(wg.shape), _resident(wbr.shape), _resident(wout.shape),
        ],
        out_specs=pl.BlockSpec((TM, d), row),
        out_shape=jax.ShapeDtypeStruct((m, d), F32),
        scratch_shapes=[pltpu.VMEM((TM, d), BF16)],
        input_output_aliases={4: 0},
        compiler_params=_cparams(1),
        name="merge",
    )(o_a, o_b, o_c, o_d, h, mod, g_pre, g_post, wg, wbr, wout)


def _ffn_kernel(hn_ref, hp_ref, modn_ref, modp_ref, gpre_ref, gpost_ref, w1_ref, w2_ref, o_ref,
                u_ref, y_ref, act_ref):
    i = pl.program_id(0)
    d_ff = w2_ref.shape[0]
    cw = 256
    cur = i % 2
    oth = 1 - cur

    def pre_norm(h_ref, mod_ref):
        return _rms(h_ref[...], gpre_ref[...]) * (1.0 + mod_ref[4:5, :]) + mod_ref[3:4, :]

    @pl.when(i == 0)
    def _():
        u_ref[0] = pre_norm(hp_ref, modp_ref).astype(BF16)
        y_ref[1] = jnp.zeros(y_ref.shape[1:], F32)

    out = hp_ref[...] + modp_ref[5:6, :] * _rms(y_ref[oth], gpost_ref[...])
    o_ref[...] = out
    u_next = pre_norm(hn_ref, modn_ref)
    for c in range(d_ff // cw):
        a = _dot(u_ref[cur], w1_ref[:, c * cw:(c + 1) * cw])
        b = _dot(u_ref[cur], w1_ref[:, d_ff + c * cw:d_ff + (c + 1) * cw])
        act_ref[:, c * cw:(c + 1) * cw] = (a * _sigmoid(a) * b).astype(BF16)
    u_ref[oth] = u_next.astype(BF16)
    y = _dot(act_ref[...], w2_ref[...])
    y_ref[cur] = y
    r0, c0 = y.shape[0] - F32_SUBLANES, y.shape[1] - LANES
    y_ref[cur, r0:, c0:] = y[r0:, c0:] + _zero_after(u_next, out)


def _ffn(h, mod, g_pre, g_post, w1, w2, tiles_per_batch, latent_only):
    m, d = h.shape
    n, row_of, mod_of = _tile_visits(m, tiles_per_batch, latent_only)
    p = _StepPipeline(n, row_of, mod_of, d)
    return pl.pallas_call(
        _ffn_kernel,
        grid=(n + 1,),
        in_specs=[p.rows_next, p.rows_prev, p.mod_next, p.mod_prev,
                  _resident((1, d)), _resident((1, d)), _resident(w1.shape), _resident(w2.shape)],
        out_specs=p.rows_out,
        out_shape=jax.ShapeDtypeStruct((n * TM, d), F32),
        scratch_shapes=[pltpu.VMEM((2, TM, d), BF16), pltpu.VMEM((2, TM, d), F32),
                        pltpu.VMEM((TM, w2.shape[0]), BF16)],
        compiler_params=_cparams(1),
        name="ffn",
    )(h, h, mod, mod, g_pre, g_post, w1, w2)


def _rope_tables(n_ctx, n_lat):
    t = jnp.arange(n_lat)
    rows_pos = (t // GRID_W).astype(F32)
    cols_pos = (t % GRID_W).astype(F32)

    def block(d):
        q = d // 4
        inv = ROPE_THETA ** (-jnp.arange(q, dtype=F32) * 2.0 / (d // 2))
        ar = rows_pos[:, None] * inv[None, :]
        ac = cols_pos[:, None] * inv[None, :]
        z = jnp.zeros_like(ar)
        cos = jnp.concatenate([jnp.cos(ar), jnp.cos(ar), jnp.cos(ac), jnp.cos(ac)], axis=1)
        up = jnp.concatenate([-jnp.sin(ar), z, -jnp.sin(ac), z], axis=1)
        dn = jnp.concatenate([z, jnp.sin(ar), z, jnp.sin(ac)], axis=1)
        return cos, up, dn

    def with_ctx(tab, fill):
        return jnp.concatenate([jnp.full((n_ctx, tab.shape[1]), fill, F32), tab], axis=0)

    c64, u64, d64 = block(DH)
    rope_64 = jnp.stack([with_ctx(jnp.tile(c64, (1, LANES // DH)), 1.0),
                         with_ctx(jnp.tile(u64, (1, LANES // DH)), 0.0),
                         with_ctx(jnp.tile(d64, (1, LANES // DH)), 0.0)])
    ca, ua, da = block(ROPE_A)
    pad_lo = NOPE_A
    pad_hi = HEAD_PAD_A - NOPE_A - ROPE_A

    def widen(tab, fill):
        return jnp.concatenate([jnp.full((n_lat, pad_lo), fill, F32), tab,
                                jnp.full((n_lat, pad_hi), fill, F32)], axis=1)

    rope_a = jnp.stack([with_ctx(widen(ca, 1.0), 1.0), with_ctx(widen(ua, 0.0), 0.0),
                        with_ctx(widen(da, 0.0), 0.0)])
    return {"rope_a": rope_a, "rope_64": rope_64}


def _permute_heads(w, axis, order):
    shape = w.shape
    w = w.reshape(shape[:axis] + (N_HEADS, DH) + shape[axis + 1:])
    w = jnp.take(w, jnp.asarray(order), axis=axis)
    return w.reshape(shape)


def _layer_weights(l, w_in, g_a_q, g_a_kv, w_a_uq, w_a_ukv, g_d_q, g_d_k, w_branch):
    d = w_in.shape[1]
    wl = w_in[l]
    o = 0
    a_cq = wl[:, o:o + Q_LORA]; o += Q_LORA
    a_ckv = wl[:, o:o + KV_LORA]; o += KV_LORA
    a_kr = wl[:, o:o + ROPE_A]; o += ROPE_A
    w_bq = wl[:, o:o + HQ]; o += HQ
    w_bkv = wl[:, o:o + 2 * KV_B * DH]; o += 2 * KV_B * DH
    w_c = wl[:, o:o + 3 * HQ]; o += 3 * HQ
    w_d = wl[:, o:o + HQ + 2 * KV_D * DH]; o += HQ + 2 * KV_D * DH
    w_g = wl[:, o:]
    zeros = lambda n: jnp.zeros((d, n), wl.dtype)
    wa = jnp.concatenate([a_cq, a_ckv, zeros(NOPE_A), a_kr, zeros(HEAD_PAD_A - NOPE_A - ROPE_A)], axis=1)
    w_b = jnp.concatenate([_permute_heads(w_bq, 1, GQA_HEAD_ORDER), w_bkv], axis=1)
    uq = w_a_uq[l].reshape(Q_LORA, N_HEADS, NOPE_A + ROPE_A)
    uq = jnp.pad(uq, ((0, 0), (0, 0), (0, HEAD_PAD_A - NOPE_A - ROPE_A))).reshape(Q_LORA, N_HEADS * HEAD_PAD_A)
    ukv = w_a_ukv[l].reshape(KV_LORA, N_HEADS, NOPE_A + DH)
    wk = jnp.pad(ukv[:, :, :NOPE_A], ((0, 0), (0, 0), (0, HEAD_PAD_A - NOPE_A))).reshape(KV_LORA, N_HEADS * HEAD_PAD_A)
    wv = ukv[:, :, NOPE_A:].reshape(KV_LORA, HQ)
    wbr = w_branch[l]
    wbr = jnp.stack([wbr[0], _permute_heads(wbr[1], 0, GQA_HEAD_ORDER), wbr[2], wbr[3]])
    cast = lambda x: x.astype(BF16)
    return {
        "wa": cast(wa), "wb": cast(w_b), "wc": cast(w_c), "wd": cast(w_d), "wg": cast(w_g),
        "wuq": cast(uq), "wk": cast(wk), "wv": cast(wv), "wbr": cast(wbr),
        "gaq": g_a_q[l][None, :], "gakv": g_a_kv[l][None, :],
        "gdq": jnp.tile(g_d_q[l], LANES // DH)[None, :], "gdk": jnp.tile(g_d_k[l], LANES // DH)[None, :],
    }


def kernel(x, c, ctx, c_ctx, w_mod, b_mod, g_pre_mix, g_post_mix, g_pre_ffn, g_post_ffn,
           w_in, g_a_q, g_a_kv, w_a_uq, w_a_ukv, sink_b, rpb_c, g_d_q, g_d_k,
           w_branch, w_out, w_ffn_in, w_ffn_out):
    b, n_lat, d = x.shape
    n_ctx = ctx.shape[1]
    s = n_ctx + n_lat
    depth = w_mod.shape[0]
    assert n_ctx == TM and n_lat % TM == 0 and n_lat >= TM + 2 * WINDOW
    assert n_lat % GRID_W == 0 and n_lat // GRID_W >= KH
    assert KV_B * DH == LANES and KV_D * DH == LANES
    tiles_per_batch = s // TM

    n_rows = -(-(b + 1) // F32_SUBLANES) * F32_SUBLANES
    cc = jnp.zeros((n_rows, d), F32).at[:b].set(c).at[b].set(c_ctx)
    mods = _modulation(cc, w_mod, b_mod)
    mod_x = mods[:, :b].reshape(depth, b, 1, MOD_CHUNKS, d)
    mod_c = jnp.broadcast_to(mods[:, b].reshape(depth, 1, 1, MOD_CHUNKS, d), mod_x.shape)
    mod_tab = jnp.concatenate([mod_c, mod_x], axis=2)

    tabs = _rope_tables(n_ctx, n_lat)
    h = jnp.concatenate([ctx, x], axis=1).reshape(b * s, d)

    for l in range(depth):
        w = _layer_weights(l, w_in, g_a_q, g_a_kv, w_a_uq, w_a_ukv, g_d_q, g_d_k, w_branch)
        mod = mod_tab[l]
        g_pre = g_pre_mix[l][None, :]
        last = l == depth - 1
        q_first = n_ctx // TM if last else 0
        ka, qat, vat, pb, pc, kd, qdt, vdt = _project(h, mod, g_pre, w, tabs, b, tiles_per_batch)
        o_a = _dense_attention(ka.reshape(b, s, -1), qat, vat, n_ctx, HEAD_PAD_A, q_first)
        o_b = _pair_attention("window", pb.reshape(b, s, -1), n_ctx, q_first, k_w=LANES, k_blk=HQ // LANES,
                              sink=sink_b[l])
        o_c = _pair_attention("nbr", pc.reshape(b, s, -1), n_ctx, q_first, k_w=HQ, k_blk=1,
                              bias_tab=_nbr_bias(rpb_c[l]))
        o_d = _dense_attention(kd.reshape(b, s, -1), qdt, vdt, n_ctx, DH, q_first)
        flat = lambda t: t.reshape(b * s, HQ)
        h = _merge(flat(o_a), flat(o_b), flat(o_c), flat(o_d), h, mod, g_pre, g_post_mix[l][None, :],
                   w["wg"], w["wbr"], w_out[l].astype(BF16), tiles_per_batch, last)
        h = _ffn(h, mod, g_pre_ffn[l][None, :], g_post_ffn[l][None, :],
                 w_ffn_in[l].astype(BF16), w_ffn_out[l].astype(BF16), tiles_per_batch, last)
    return h.reshape(b, n_lat, d)
```

```python
import functools

import jax
import jax.numpy as jnp
from jax import lax
from jax.experimental import pallas as pl
from jax.experimental.pallas import tpu as pltpu

F32 = jnp.float32
BF16 = jnp.bfloat16
UINT = jnp.uint32

GRID_W = 64
ROPE_THETA = 10000.0
EPS = 1e-6
NEG = -1e30
DH = 64
N_HEADS = 8
N_BRANCH = 4
Q_LORA = 256
KV_LORA = 128
NOPE_A = 64
ROPE_A = 32
KV_B = 2
WINDOW = 128
KH = 8
KW = 16
KV_D = 2
MOD_CHUNKS = 6

LANES = 128
F32_SUBLANES = 8
VMEM_LIMIT_BYTES = 56 * 1024 * 1024

TM = 256
HEAD_PAD_A = 128
HQ = N_HEADS * DH
S_A = (NOPE_A + ROPE_A) ** -0.5
S_H = DH ** -0.5
LOG2E = 1.4426950408889634
KEY_CHUNK = 1152
ONES_ROWS = 16
VT_ROWS = DH + ONES_ROWS
GQA_HEAD_ORDER = tuple(h for j in range(N_HEADS // 2) for h in (j, j + N_HEADS // 2))


def _cparams(n_axes):
    return pltpu.CompilerParams(
        dimension_semantics=("arbitrary",) * n_axes, vmem_limit_bytes=VMEM_LIMIT_BYTES)


def _resident(shape):
    nd = len(shape)
    return pl.BlockSpec(shape, lambda *_: (0,) * nd, pipeline_mode=pl.Buffered(1))


def _tile_visits(m, tiles_per_batch, latent_only):
    lat = tiles_per_batch - 1
    if latent_only:
        return (m // TM // tiles_per_batch * lat,
                lambda j: (j // lat) * tiles_per_batch + 1 + j % lat,
                lambda j: (j // lat, 1))
    return (m // TM, lambda j: j,
            lambda j: (j // tiles_per_batch, jnp.minimum(j % tiles_per_batch, 1)))


class _StepPipeline:
    def __init__(self, n, row_of, mod_of, d):
        nxt = lambda i: jnp.minimum(i + 1, n - 1)
        prv = lambda i: jnp.maximum(i - 1, 0)
        mod_block = (None, None, MOD_CHUNKS, d)
        self.rows_next = pl.BlockSpec((TM, d), lambda i: (row_of(nxt(i)), 0))
        self.rows_prev = pl.BlockSpec((TM, d), lambda i: (row_of(prv(i)), 0))
        self.mod_next = pl.BlockSpec(mod_block, lambda i: (*mod_of(nxt(i)), 0, 0))
        self.mod_prev = pl.BlockSpec(mod_block, lambda i: (*mod_of(prv(i)), 0, 0))
        self.rows_out = pl.BlockSpec((TM, d), lambda i: (prv(i), 0))


def _mod_spec(d, tiles_per_batch):
    return pl.BlockSpec((None, None, MOD_CHUNKS, d),
                        lambda i: (i // tiles_per_batch, jnp.minimum(i % tiles_per_batch, 1), 0, 0))


def _dot(a, b):
    return jnp.dot(a, b, preferred_element_type=F32)


def _dot_nt(a, b):
    return lax.dot_general(a, b, (((1,), (1,)), ((), ())), preferred_element_type=F32)


def _rms(x, g):
    ms = jnp.mean(x * x, axis=-1, keepdims=True)
    return x * lax.rsqrt(ms + EPS) * g


def _sigmoid(x):
    return 1.0 / (1.0 + jnp.exp(-x))


def _zero_after(*tiles):
    r = None
    for t in tiles:
        b = pltpu.bitcast(t, UINT)
        acc = b[:, 0:LANES]
        for c in range(1, b.shape[1] // LANES):
            acc = acc | b[:, c * LANES:(c + 1) * LANES]
        fold = acc[0:F32_SUBLANES]
        for k in range(1, acc.shape[0] // F32_SUBLANES):
            fold = fold | acc[k * F32_SUBLANES:(k + 1) * F32_SUBLANES]
        r = fold if r is None else r | fold
    half = jnp.iinfo(UINT).bits // 2
    z = lax.shift_right_logical(lax.shift_right_logical(r, jnp.array(half, UINT)), jnp.array(half, UINT))
    return pltpu.bitcast(z, F32)


def _rope(x, tab_ref, shift):
    n = x.shape[-1]
    return (x * tab_ref[0] + pltpu.roll(x, n - shift, 1) * tab_ref[1]
            + pltpu.roll(x, shift, 1) * tab_ref[2])


def _head_rms(x, g, bd):
    x2 = x * x
    hi = x2.astype(BF16)
    lo = (x2 - hi.astype(F32)).astype(BF16)
    ss = _dot(hi, bd) + _dot(lo, bd)
    return x * lax.rsqrt(ss * (1.0 / DH) + EPS) * g


def _pair_scores(q_blk, parts, s_ref, slot):
    m = q_blk.shape[0]
    lane = lax.broadcasted_iota(jnp.int32, (m, LANES), 1)
    zero = jnp.zeros_like(q_blk)
    qs = jnp.concatenate([jnp.where(lane < DH, q_blk, zero), jnp.where(lane >= DH, q_blk, zero)], axis=0)
    layout = []
    off = 0
    for k, load_v, bias in parts:
        n = k.shape[0]
        s = _dot_nt(qs, k)
        s_ref[slot, 0:2 * m, off:off + n] = s if bias is None else s + bias
        layout.append((off, n, load_v))
        off += n
    return m, layout


def _pair_finish(handle, s_ref, slot, sinks=None):
    m, layout = handle
    width = sum(n for _, n, _ in layout)
    s = s_ref[slot, 0:2 * m, 0:width]
    mx = jnp.max(s, axis=1, keepdims=True)
    sink = None
    if sinks is not None:
        row = lax.broadcasted_iota(jnp.int32, (2 * m, 1), 0)
        sink = jnp.where(row < m, sinks[0], sinks[1]) * LOG2E
        mx = jnp.maximum(mx, sink)
    p = jnp.exp2(s - mx)
    l = jnp.sum(p, axis=1, keepdims=True)
    if sink is not None:
        l = l + jnp.exp2(sink - mx)
    pb = p.astype(BF16)
    acc = None
    for off, n, load_v in layout:
        a = _dot(pb[:, off:off + n], load_v())
        acc = a if acc is None else acc + a
    o = acc / l
    lane = lax.broadcasted_iota(jnp.int32, (m, LANES), 1)
    return jnp.where(lane < DH, o[:m], o[m:])


def _pipelined(items, scores_fn, finish_fn):
    cur = scores_fn(items[0], 0)
    for n, item in enumerate(items):
        nxt = scores_fn(items[n + 1], (n + 1) % 2) if n + 1 < len(items) else None
        finish_fn(item, cur, n % 2)
        cur = nxt


def _mod_kernel(c_ref, w_ref, b_ref, o_ref):
    c = c_ref[...]
    sc = (c * _sigmoid(c)).astype(BF16)
    o_ref[...] = _dot(sc, w_ref[...].astype(BF16)) + b_ref[...]


def _modulation(cc, w_mod, b_mod):
    n_layers, d, n = w_mod.shape
    r = cc.shape[0]
    tn = n // 4
    return pl.pallas_call(
        _mod_kernel,
        grid=(n_layers, n // tn),
        in_specs=[
            pl.BlockSpec((r, d), lambda l, j: (0, 0)),
            pl.BlockSpec((None, d, tn), lambda l, j: (l, 0, j)),
            pl.BlockSpec((None, 1, tn), lambda l, j: (l, 0, j)),
        ],
        out_specs=pl.BlockSpec((None, r, tn), lambda l, j: (l, 0, j)),
        out_shape=jax.ShapeDtypeStruct((n_layers, r, n), F32),
        compiler_params=_cparams(2),
        name="modulation",
    )(cc, w_mod, b_mod.reshape(n_layers, 1, n))


def _nbr_bias_kernel(rpb_ref, o_ref):
    pair = pl.program_id(0)
    n_dr = 2 * KH - 1
    n_dc = 2 * KW - 1
    qc = lax.broadcasted_iota(jnp.int32, (GRID_W, 2 * GRID_W), 0)
    lane = lax.broadcasted_iota(jnp.int32, (GRID_W, 2 * GRID_W), 1)
    second = lane >= GRID_W
    kc = jnp.where(second, lane - GRID_W, lane)
    c0 = jnp.clip(qc - KW // 2, 0, GRID_W - KW)
    ok = (kc >= c0) & (kc < c0 + KW)
    dc = kc - qc + (KW - 1)

    for e in range(2):
        h = 2 * pair + e

        def d2_body(d2, carry, h=h, e=e):
            def d_body(d, acc):
                v0 = rpb_ref[(h * n_dr + d2) * n_dc + d]
                v1 = rpb_ref[(h * n_dr + d2 + 1) * n_dc + d]
                return jnp.where(dc == d, jnp.where(second, v1, v0), acc)

            acc = lax.fori_loop(0, n_dc, d_body, jnp.zeros((GRID_W, 2 * GRID_W), F32), unroll=True)
            o_ref[d2, e * GRID_W:(e + 1) * GRID_W, :] = jnp.where(ok, acc * LOG2E, NEG)
            return carry

        lax.fori_loop(0, n_dr - 1, d2_body, 0)


def _nbr_bias(rpb):
    h = rpb.shape[0]
    n_pairs = 2 * KH - 2
    return pl.pallas_call(
        _nbr_bias_kernel,
        grid=(h // 2,),
        in_specs=[pl.BlockSpec(memory_space=pltpu.SMEM)],
        out_specs=pl.BlockSpec((None, n_pairs, 2 * GRID_W, 2 * GRID_W), lambda i: (i, 0, 0, 0)),
        out_shape=jax.ShapeDtypeStruct((h // 2, n_pairs, 2 * GRID_W, 2 * GRID_W), F32),
        compiler_params=_cparams(1),
        name="nbr_bias",
    )(rpb.reshape(-1))


def _store_vt(vt_ref, c, v):
    t = v.T.astype(BF16)
    ones = jnp.ones((ONES_ROWS, v.shape[0]), BF16)
    for e in range(LANES // DH):
        base = (2 * c + e) * VT_ROWS
        vt_ref[base:base + DH, :] = t[e * DH:(e + 1) * DH, :]
        vt_ref[base + DH:base + VT_ROWS, :] = ones


def _proj_kernel(h_ref, mod_ref, g_ref, wa_ref, wb_ref, wc_ref, wd_ref, wuq_ref, wk_ref, wv_ref,
                 gaq_ref, gakv_ref, gdq_ref, gdk_ref, ropea_ref, rope64_ref,
                 ka_ref, qat_ref, vat_ref, pb_ref, pc_ref, kd_ref, qdt_ref, vdt_ref):
    sh1 = mod_ref[0:1, :]
    sc1 = mod_ref[1:2, :]
    u = (_rms(h_ref[...], g_ref[...]) * (1.0 + sc1) + sh1).astype(BF16)

    rb = jnp.where(lax.broadcasted_iota(jnp.int32, (LANES, LANES), 0) < DH, 0, 1)
    cb = jnp.where(lax.broadcasted_iota(jnp.int32, (LANES, LANES), 1) < DH, 0, 1)
    bd = jnp.where(rb == cb, 1.0, 0.0).astype(BF16)

    ya = _dot(u, wa_ref[...])
    yd = _dot(u, wd_ref[...])
    cq = _rms(ya[:, :Q_LORA], gaq_ref[...]).astype(BF16)
    ckv = _rms(ya[:, Q_LORA:Q_LORA + KV_LORA], gakv_ref[...]).astype(BF16)
    kr = _rope(ya[:, Q_LORA + KV_LORA:], ropea_ref, ROPE_A // 4)
    yb = _dot(u, wb_ref[...])
    qa = _dot(cq, wuq_ref[...])
    ka = _dot(ckv, wk_ref[...])
    va = _dot(ckv, wv_ref[...])

    for c in range(HQ // LANES):
        lo, hi = c * LANES, (c + 1) * LANES
        qn = _head_rms(yd[:, lo:hi], gdq_ref[...], bd)
        qdt_ref[lo:hi, :] = (_rope(qn, rope64_ref, DH // 4) * (S_H * LOG2E)).T.astype(BF16)
    kn = _head_rms(yd[:, HQ:HQ + LANES], gdk_ref[...], bd)
    kd_ref[...] = _rope(kn, rope64_ref, DH // 4).astype(BF16)
    _store_vt(vdt_ref, 0, yd[:, HQ + LANES:])

    yc = _dot(u, wc_ref[...])

    for c in range(HQ // LANES):
        lo, hi = c * LANES, (c + 1) * LANES
        pb_ref[:, lo:hi] = (_rope(yb[:, lo:hi], rope64_ref, DH // 4) * (S_H * LOG2E)).astype(BF16)
    pb_ref[:, HQ:HQ + LANES] = _rope(yb[:, HQ:HQ + LANES], rope64_ref, DH // 4).astype(BF16)
    pb_ref[:, HQ + LANES:] = yb[:, HQ + LANES:].astype(BF16)

    for hh in range(N_HEADS):
        lo, hi = hh * HEAD_PAD_A, (hh + 1) * HEAD_PAD_A
        q = _rope(qa[:, lo:hi], ropea_ref, ROPE_A // 4) * (S_A * LOG2E)
        qat_ref[lo:hi, :] = q.T.astype(BF16)
        ka_ref[:, lo:hi] = (ka[:, lo:hi] + kr).astype(BF16)
    for c in range(HQ // LANES):
        lo, hi = c * LANES, (c + 1) * LANES
        _store_vt(vat_ref, c, va[:, lo:hi])

    pc_ref[:, :HQ] = (yc[:, :HQ] * (S_H * LOG2E)).astype(BF16)
    pc_ref[:, HQ:] = yc[:, HQ:].astype(BF16)


def _project(h, mod, g_pre, w, tabs, n_batch, tiles_per_batch):
    m, d = h.shape
    s = tiles_per_batch * TM
    wq_a = N_HEADS * HEAD_PAD_A
    row = lambda i: (i, 0)
    col = lambda i: (i // tiles_per_batch, 0, i % tiles_per_batch)
    tile_pos = lambda i: (0, i % tiles_per_batch, 0)
    in_specs = [
        pl.BlockSpec((TM, d), row),
        _mod_spec(d, tiles_per_batch),
        _resident((1, d)),
        _resident(w["wa"].shape), _resident(w["wb"].shape), _resident(w["wc"].shape),
        _resident(w["wd"].shape), _resident(w["wuq"].shape),
        _resident(w["wk"].shape), _resident(w["wv"].shape),
        _resident((1, Q_LORA)), _resident((1, KV_LORA)), _resident((1, LANES)), _resident((1, LANES)),
        pl.BlockSpec((3, TM, LANES), tile_pos),
        pl.BlockSpec((3, TM, LANES), tile_pos),
    ]

    def rows_out(c):
        return pl.BlockSpec((TM, c), row), jax.ShapeDtypeStruct((m, c), BF16)

    def cols_out(r):
        return pl.BlockSpec((None, r, TM), col), jax.ShapeDtypeStruct((n_batch, r, s), BF16)

    outs = [rows_out(wq_a), cols_out(wq_a), cols_out(N_HEADS * VT_ROWS),
            rows_out(HQ + 2 * LANES), rows_out(3 * HQ),
            rows_out(LANES), cols_out(HQ), cols_out(KV_D * VT_ROWS)]
    return pl.pallas_call(
        _proj_kernel,
        grid=(m // TM,),
        in_specs=in_specs,
        out_specs=[o[0] for o in outs],
        out_shape=[o[1] for o in outs],
        compiler_params=_cparams(1),
        name="proj",
    )(h, mod, g_pre, w["wa"], w["wb"], w["wc"], w["wd"], w["wuq"], w["wk"], w["wv"],
      w["gaq"], w["gakv"], w["gdq"], w["gdk"], tabs["rope_a"], tabs["rope_64"])


def _dense_attn_kernel(ka_ref, qat_ref, vat_ref, kd_ref, qdt_ref, vdt_ref, oa_ref, od_ref, ot_ref, st_ref,
                       *, n_ctx, q_first):
    qi = pl.program_id(1) + q_first
    group_d = N_HEADS // (LANES // DH)
    items = [(mixer, h) for mixer in range(2) for h in range(N_HEADS)]

    def attend(n_keys):
        n_chunks = -(-n_keys // KEY_CHUNK)
        bounds = [(i * n_keys // n_chunks, (i + 1) * n_keys // n_chunks) for i in range(n_chunks)]

        def q_operand(item):
            mixer, h = item
            if mixer == 0:
                return qat_ref[h * LANES:(h + 1) * LANES, :]
            qh = qdt_ref[h * DH:(h + 1) * DH, :]
            z = jnp.zeros_like(qh)
            return jnp.concatenate([qh, z] if h // group_d == 0 else [z, qh], axis=0)

        def scores(n, qt, i):
            mixer, h = items[n]
            lo, hi = bounds[i]
            k = ka_ref[lo:hi, h * LANES:(h + 1) * LANES] if mixer == 0 else kd_ref[lo:hi, :]
            st_ref[n % 2, lo:hi, :] = _dot(k, qt)

        def colmax(n):
            m = None
            for lo, hi in bounds:
                mi = jnp.max(st_ref[n % 2, lo:hi, :], axis=0, keepdims=True)
                m = mi if m is None else jnp.maximum(m, mi)
            return m

        qt = q_operand(items[0])
        for i in range(n_chunks):
            scores(0, qt, i)
        for n, (mixer, h) in enumerate(items):
            vt_ref, g = (vat_ref, h) if mixer == 0 else (vdt_ref, h // group_d)
            m = colmax(n)
            qt = q_operand(items[n + 1]) if n + 1 < len(items) else None
            acc = None
            for i in range(n_chunks):
                if qt is not None:
                    scores(n + 1, qt, i)
                lo, hi = bounds[i]
                p = jnp.exp2(st_ref[n % 2, lo:hi, :] - m).astype(BF16)
                ai = _dot(vt_ref[g * VT_ROWS:(g + 1) * VT_ROWS, lo:hi], p)
                acc = ai if acc is None else acc + ai
            ot_ref[mixer, h * DH:(h + 1) * DH, :] = acc[:DH] / acc[DH:DH + 1]
            if h == N_HEADS - 1:
                o_ref = oa_ref if mixer == 0 else od_ref
                o_ref[...] = ot_ref[mixer].T.astype(o_ref.dtype)

    @pl.when(qi < n_ctx // TM)
    def _():
        attend(n_ctx)

    @pl.when(qi >= n_ctx // TM)
    def _():
        attend(ka_ref.shape[0])


def _dense_attention(ka, qat, vat, kd, qdt, vdt, n_ctx, q_first):
    b, s, _ = ka.shape
    whole = lambda bi, qi: (bi, 0, 0)
    q_tile = lambda bi, qi: (bi, 0, qi + q_first)
    o_tile = lambda bi, qi: (bi, qi + q_first, 0)
    return pl.pallas_call(
        functools.partial(_dense_attn_kernel, n_ctx=n_ctx, q_first=q_first),
        grid=(b, s // TM - q_first),
        in_specs=[
            pl.BlockSpec((None, s, ka.shape[2]), whole),
            pl.BlockSpec((None, qat.shape[1], TM), q_tile),
            pl.BlockSpec((None, vat.shape[1], s), whole),
            pl.BlockSpec((None, s, kd.shape[2]), whole),
            pl.BlockSpec((None, qdt.shape[1], TM), q_tile),
            pl.BlockSpec((None, vdt.shape[1], s), whole),
        ],
        out_specs=[pl.BlockSpec((None, TM, HQ), o_tile)] * 2,
        out_shape=[jax.ShapeDtypeStruct((b, s, HQ), BF16)] * 2,
        scratch_shapes=[pltpu.VMEM((2, HQ, TM), F32), pltpu.VMEM((2, s, TM), F32)],
        compiler_params=_cparams(2),
        name="attn_dense",
    )(ka, qat, vat, kd, qdt, vdt)


def _window_attn_kernel(sink_ref, q_ref, k_ref, v_ref, o_ref, s_ref, *, n_ctx, q_first):
    qi = pl.program_id(1) + q_first
    n_lat = k_ref.shape[0] - n_ctx
    span = TM + 2 * WINDOW
    blocks = list(range(HQ // LANES))

    def ctx_part():
        return k_ref[0:n_ctx, :], lambda: v_ref[0:n_ctx, :], None

    def finish(j, handle, slot):
        sinks = sink_ref[GQA_HEAD_ORDER[2 * j]], sink_ref[GQA_HEAD_ORDER[2 * j + 1]]
        o_ref[:, j * LANES:(j + 1) * LANES] = _pair_finish(handle, s_ref, slot, sinks).astype(o_ref.dtype)

    @pl.when(qi < n_ctx // TM)
    def _():
        def scores(j, slot):
            return _pair_scores(q_ref[:, j * LANES:(j + 1) * LANES], [ctx_part()], s_ref, slot)

        _pipelined(blocks, scores, finish)

    @pl.when(qi >= n_ctx // TM)
    def _():
        q0 = (qi - n_ctx // TM) * TM
        start = pl.multiple_of(jnp.clip(q0 - WINDOW, 0, n_lat - span), WINDOW)
        dist = (lax.broadcasted_iota(jnp.int32, (2 * TM, span), 0)
                - lax.broadcasted_iota(jnp.int32, (2 * TM, span), 1) + (q0 - start))
        dist = jnp.where(lax.broadcasted_iota(jnp.int32, (2 * TM, span), 0) >= TM, dist - TM, dist)
        bias = jnp.where(jnp.abs(dist) <= WINDOW, 0.0, NEG)
        rows = pl.ds(n_ctx + start, span)

        def scores(j, slot):
            return _pair_scores(q_ref[:, j * LANES:(j + 1) * LANES],
                                [(k_ref[rows, :], lambda: v_ref[rows, :], bias), ctx_part()], s_ref, slot)

        _pipelined(blocks, scores, finish)


def _nbr_attn_kernel(q_ref, k_ref, v_ref, tb_ref, o_ref, s_ref, *, n_ctx, q_first):
    qi = pl.program_id(1) + q_first
    n_lat = k_ref.shape[0] - n_ctx
    grid_rows = n_lat // GRID_W
    rows_per_tile = TM // GRID_W
    span = KH * GRID_W
    n_blk = HQ // LANES

    def ctx_part(blk):
        return k_ref[0:n_ctx, blk], lambda: v_ref[0:n_ctx, blk], None

    @pl.when(qi < n_ctx // TM)
    def _():
        def scores(j, slot):
            blk = slice(j * LANES, (j + 1) * LANES)
            return _pair_scores(q_ref[:, blk], [ctx_part(blk)], s_ref, slot)

        def finish(j, handle, slot):
            o_ref[:, j * LANES:(j + 1) * LANES] = _pair_finish(handle, s_ref, slot).astype(o_ref.dtype)

        _pipelined(list(range(n_blk)), scores, finish)

    @pl.when(qi >= n_ctx // TM)
    def _():
        def scores(item, slot):
            rr, j = item
            r = (qi - n_ctx // TM) * rows_per_tile + rr
            r0 = jnp.clip(r - KH // 2, 0, grid_rows - KH)
            delta = r - r0
            krows = pl.ds(pl.multiple_of(n_ctx + r0 * GRID_W, GRID_W), span)
            blk = slice(j * LANES, (j + 1) * LANES)
            bias = jnp.concatenate(
                [tb_ref[j, 2 * jj - delta + (KH - 1)] for jj in range(KH // 2)], axis=1)
            return _pair_scores(q_ref[rr * GRID_W:(rr + 1) * GRID_W, blk],
                                [(k_ref[krows, blk], lambda: v_ref[krows, blk], bias), ctx_part(blk)],
                                s_ref, slot)

        def finish(item, handle, slot):
            rr, j = item
            o_ref[rr * GRID_W:(rr + 1) * GRID_W, j * LANES:(j + 1) * LANES] = (
                _pair_finish(handle, s_ref, slot).astype(o_ref.dtype))

        _pipelined([(rr, j) for rr in range(rows_per_tile) for j in range(n_blk)], scores, finish)


def _pair_attention(kind, p, n_ctx, q_first, *, k_w, k_blk, sink=None, bias_tab=None):
    b, s, _ = p.shape
    in_specs = [
        pl.BlockSpec((None, TM, HQ), lambda bi, qi: (bi, qi + q_first, 0)),
        pl.BlockSpec((None, s, k_w), lambda bi, qi: (bi, 0, k_blk)),
        pl.BlockSpec((None, s, k_w), lambda bi, qi: (bi, 0, k_blk + 1)),
    ]
    args = [p, p, p]
    if kind == "window":
        body = functools.partial(_window_attn_kernel, n_ctx=n_ctx, q_first=q_first)
        in_specs = [pl.BlockSpec(memory_space=pltpu.SMEM)] + in_specs
        args = [sink] + args
    else:
        body = functools.partial(_nbr_attn_kernel, n_ctx=n_ctx, q_first=q_first)
        in_specs = in_specs + [_resident(bias_tab.shape)]
        args = args + [bias_tab]
    return pl.pallas_call(
        body,
        grid=(b, s // TM - q_first),
        in_specs=in_specs,
        out_specs=pl.BlockSpec((None, TM, HQ), lambda bi, qi: (bi, qi + q_first, 0)),
        out_shape=jax.ShapeDtypeStruct((b, s, HQ), BF16),
        scratch_shapes=[pltpu.VMEM((2, 2 * TM, max(TM + 2 * WINDOW, KH * GRID_W) + n_ctx), F32)],
        compiler_params=_cparams(2),
        name="attn_" + kind,
    )(*args)


def _merge_kernel(oa_ref, ob_ref, oc_ref, od_ref, h_ref, mod_ref, gpre_ref, gpost_ref,
                  wg_ref, wbr_ref, wout_ref, o_ref, acc_ref):
    d_model = h_ref.shape[1]
    cw = 256
    h = h_ref[...]
    sh1 = mod_ref[0:1, :]
    sc1 = mod_ref[1:2, :]
    ga1 = mod_ref[2:3, :]
    branches = (oa_ref, ob_ref, oc_ref, od_ref)

    def up_proj(n, c):
        return _dot(branches[n][...], wbr_ref[n, :, c * cw:(c + 1) * cw])

    up0 = [up_proj(n, 0) for n in range(N_BRANCH)]
    u = (_rms(h, gpre_ref[...]) * (1.0 + sc1) + sh1).astype(BF16)
    for c in range(d_model // cw):
        acc = None
        for n in range(N_BRANCH):
            lo = n * d_model + c * cw
            gate = _dot(u, wg_ref[:, lo:lo + cw])
            up = up0[n] if c == 0 else up_proj(n, c)
            term = _sigmoid(gate) * up
            acc = term if acc is None else acc + term
        acc_ref[:, c * cw:(c + 1) * cw] = acc.astype(BF16)
    y = _dot(acc_ref[...], wout_ref[...])
    o_ref[...] = h + ga1 * _rms(y, gpost_ref[...])


def _merge(o_a, o_b, o_c, o_d, h, mod, g_pre, g_post, wg, wbr, wout, tiles_per_batch, latent_only):
    m, d = h.shape
    n, row_of, mod_of = _tile_visits(m, tiles_per_batch, latent_only)
    row = lambda i: (row_of(i), 0)
    return pl.pallas_call(
        _merge_kernel,
        grid=(n,),
        in_specs=[pl.BlockSpec((TM, HQ), row)] * N_BRANCH + [
            pl.BlockSpec((TM, d), row),
            pl.BlockSpec((None, None, MOD_CHUNKS, d), lambda i: (*mod_of(i), 0, 0)),
            _resident((1, d)), _resident((1, d)),
            _resident(wg.shape), _resident(wbr.shape), _resident(wout.shape),
        ],
        out_specs=pl.BlockSpec((TM, d), row),
        out_shape=jax.ShapeDtypeStruct((m, d), F32),
        scratch_shapes=[pltpu.VMEM((TM, d), BF16)],
        input_output_aliases={4: 0},
        compiler_params=_cparams(1),
        name="merge",
    )(o_a, o_b, o_c, o_d, h, mod, g_pre, g_post, wg, wbr, wout)


def _ffn_kernel(hn_ref, hp_ref, modn_ref, modp_ref, gpre_ref, gpost_ref, w1_ref, w2_ref, o_ref,
                u_ref, y_ref, act_ref):
    i = pl.program_id(0)
    d_ff = w2_ref.shape[0]
    cw = 256
    cur = i % 2
    oth = 1 - cur

    def pre_norm(h_ref, mod_ref):
        return _rms(h_ref[...], gpre_ref[...]) * (1.0 + mod_ref[4:5, :]) + mod_ref[3:4, :]

    @pl.when(i == 0)
    def _():
        u_ref[0] = pre_norm(hp_ref, modp_ref).astype(BF16)
        y_ref[1] = jnp.zeros(y_ref.shape[1:], F32)

    out = hp_ref[...] + modp_ref[5:6, :] * _rms(y_ref[oth], gpost_ref[...])
    o_ref[...] = out
    u_next = pre_norm(hn_ref, modn_ref)
    for c in range(d_ff // cw):
        a = _dot(u_ref[cur], w1_ref[:, c * cw:(c + 1) * cw])
        b = _dot(u_ref[cur], w1_ref[:, d_ff + c * cw:d_ff + (c + 1) * cw])
        act_ref[:, c * cw:(c + 1) * cw] = (a * _sigmoid(a) * b).astype(BF16)
    u_ref[oth] = u_next.astype(BF16)
    y = _dot(act_ref[...], w2_ref[...])
    y_ref[cur] = y
    r0, c0 = y.shape[0] - F32_SUBLANES, y.shape[1] - LANES
    y_ref[cur, r0:, c0:] = y[r0:, c0:] + _zero_after(u_next, out)


def _ffn(h, mod, g_pre, g_post, w1, w2, tiles_per_batch, latent_only):
    m, d = h.shape
    n, row_of, mod_of = _tile_visits(m, tiles_per_batch, latent_only)
    p = _StepPipeline(n, row_of, mod_of, d)
    return pl.pallas_call(
        _ffn_kernel,
        grid=(n + 1,),
        in_specs=[p.rows_next, p.rows_prev, p.mod_next, p.mod_prev,
                  _resident((1, d)), _resident((1, d)), _resident(w1.shape), _resident(w2.shape)],
        out_specs=p.rows_out,
        out_shape=jax.ShapeDtypeStruct((n * TM, d), F32),
        scratch_shapes=[pltpu.VMEM((2, TM, d), BF16), pltpu.VMEM((2, TM, d), F32),
                        pltpu.VMEM((TM, w2.shape[0]), BF16)],
        compiler_params=_cparams(1),
        name="ffn",
    )(h, h, mod, mod, g_pre, g_post, w1, w2)


def _rope_tables(n_ctx, n_lat):
    t = jnp.arange(n_lat)
    rows_pos = (t // GRID_W).astype(F32)
    cols_pos = (t % GRID_W).astype(F32)

    def block(d):
        q = d // 4
        inv = ROPE_THETA ** (-jnp.arange(q, dtype=F32) * 2.0 / (d // 2))
        ar = rows_pos[:, None] * inv[None, :]
        ac = cols_pos[:, None] * inv[None, :]
        z = jnp.zeros_like(ar)
        cos = jnp.concatenate([jnp.cos(ar), jnp.cos(ar), jnp.cos(ac), jnp.cos(ac)], axis=1)
        up = jnp.concatenate([-jnp.sin(ar), z, -jnp.sin(ac), z], axis=1)
        dn = jnp.concatenate([z, jnp.sin(ar), z, jnp.sin(ac)], axis=1)
        return cos, up, dn

    def with_ctx(tab, fill):
        return jnp.concatenate([jnp.full((n_ctx, tab.shape[1]), fill, F32), tab], axis=0)

    c64, u64, d64 = block(DH)
    rope_64 = jnp.stack([with_ctx(jnp.tile(c64, (1, LANES // DH)), 1.0),
                         with_ctx(jnp.tile(u64, (1, LANES // DH)), 0.0),
                         with_ctx(jnp.tile(d64, (1, LANES // DH)), 0.0)])
    ca, ua, da = block(ROPE_A)
    pad_lo = NOPE_A
    pad_hi = HEAD_PAD_A - NOPE_A - ROPE_A

    def widen(tab, fill):
        return jnp.concatenate([jnp.full((n_lat, pad_lo), fill, F32), tab,
                                jnp.full((n_lat, pad_hi), fill, F32)], axis=1)

    rope_a = jnp.stack([with_ctx(widen(ca, 1.0), 1.0), with_ctx(widen(ua, 0.0), 0.0),
                        with_ctx(widen(da, 0.0), 0.0)])
    return {"rope_a": rope_a, "rope_64": rope_64}


def _permute_heads(w, axis, order):
    shape = w.shape
    w = w.reshape(shape[:axis] + (N_HEADS, DH) + shape[axis + 1:])
    w = jnp.take(w, jnp.asarray(order), axis=axis)
    return w.reshape(shape)


def _layer_weights(l, w_in, g_a_q, g_a_kv, w_a_uq, w_a_ukv, g_d_q, g_d_k, w_branch):
    d = w_in.shape[1]
    wl = w_in[l]
    o = 0
    a_cq = wl[:, o:o + Q_LORA]; o += Q_LORA
    a_ckv = wl[:, o:o + KV_LORA]; o += KV_LORA
    a_kr = wl[:, o:o + ROPE_A]; o += ROPE_A
    w_bq = wl[:, o:o + HQ]; o += HQ
    w_bkv = wl[:, o:o + 2 * KV_B * DH]; o += 2 * KV_B * DH
    w_c = wl[:, o:o + 3 * HQ]; o += 3 * HQ
    w_d = wl[:, o:o + HQ + 2 * KV_D * DH]; o += HQ + 2 * KV_D * DH
    w_g = wl[:, o:]
    zeros = lambda n: jnp.zeros((d, n), wl.dtype)
    wa = jnp.concatenate([a_cq, a_ckv, zeros(NOPE_A), a_kr, zeros(HEAD_PAD_A - NOPE_A - ROPE_A)], axis=1)
    w_b = jnp.concatenate([_permute_heads(w_bq, 1, GQA_HEAD_ORDER), w_bkv], axis=1)
    uq = w_a_uq[l].reshape(Q_LORA, N_HEADS, NOPE_A + ROPE_A)
    uq = jnp.pad(uq, ((0, 0), (0, 0), (0, HEAD_PAD_A - NOPE_A - ROPE_A))).reshape(Q_LORA, N_HEADS * HEAD_PAD_A)
    ukv = w_a_ukv[l].reshape(KV_LORA, N_HEADS, NOPE_A + DH)
    wk = jnp.pad(ukv[:, :, :NOPE_A], ((0, 0), (0, 0), (0, HEAD_PAD_A - NOPE_A))).reshape(KV_LORA, N_HEADS * HEAD_PAD_A)
    wv = ukv[:, :, NOPE_A:].reshape(KV_LORA, HQ)
    wbr = w_branch[l]
    wbr = jnp.stack([wbr[0], _permute_heads(wbr[1], 0, GQA_HEAD_ORDER), wbr[2], wbr[3]])
    cast = lambda x: x.astype(BF16)
    return {
        "wa": cast(wa), "wb": cast(w_b), "wc": cast(w_c), "wd": cast(w_d), "wg": cast(w_g),
        "wuq": cast(uq), "wk": cast(wk), "wv": cast(wv), "wbr": cast(wbr),
        "gaq": g_a_q[l][None, :], "gakv": g_a_kv[l][None, :],
        "gdq": jnp.tile(g_d_q[l], LANES // DH)[None, :], "gdk": jnp.tile(g_d_k[l], LANES // DH)[None, :],
    }


def kernel(x, c, ctx, c_ctx, w_mod, b_mod, g_pre_mix, g_post_mix, g_pre_ffn, g_post_ffn,
           w_in, g_a_q, g_a_kv, w_a_uq, w_a_ukv, sink_b, rpb_c, g_d_q, g_d_k,
           w_branch, w_out, w_ffn_in, w_ffn_out):
    b, n_lat, d = x.shape
    n_ctx = ctx.shape[1]
    s = n_ctx + n_lat
    depth = w_mod.shape[0]
    assert n_ctx == TM and n_lat % TM == 0 and n_lat >= TM + 2 * WINDOW
    assert n_lat % GRID_W == 0 and n_lat // GRID_W >= KH
    assert KV_B * DH == LANES and KV_D * DH == LANES
    tiles_per_batch = s // TM

    n_rows = -(-(b + 1) // F32_SUBLANES) * F32_SUBLANES
    cc = jnp.zeros((n_rows, d), F32).at[:b].set(c).at[b].set(c_ctx)
    mods = _modulation(cc, w_mod, b_mod)
    mod_x = mods[:, :b].reshape(depth, b, 1, MOD_CHUNKS, d)
    mod_c = jnp.broadcast_to(mods[:, b].reshape(depth, 1, 1, MOD_CHUNKS, d), mod_x.shape)
    mod_tab = jnp.concatenate([mod_c, mod_x], axis=2)

    tabs = _rope_tables(n_ctx, n_lat)
    h = jnp.concatenate([ctx, x], axis=1).reshape(b * s, d)

    for l in range(depth):
        w = _layer_weights(l, w_in, g_a_q, g_a_kv, w_a_uq, w_a_ukv, g_d_q, g_d_k, w_branch)
        mod = mod_tab[l]
        g_pre = g_pre_mix[l][None, :]
        last = l == depth - 1
        q_first = n_ctx // TM if last else 0
        ka, qat, vat, pb, pc, kd, qdt, vdt = _project(h, mod, g_pre, w, tabs, b, tiles_per_batch)
        o_a, o_d = _dense_attention(ka.reshape(b, s, -1), qat, vat, kd.reshape(b, s, -1), qdt, vdt,
                                    n_ctx, q_first)
        o_b = _pair_attention("window", pb.reshape(b, s, -1), n_ctx, q_first, k_w=LANES, k_blk=HQ // LANES,
                              sink=sink_b[l])
        o_c = _pair_attention("nbr", pc.reshape(b, s, -1), n_ctx, q_first, k_w=HQ, k_blk=1,
                              bias_tab=_nbr_bias(rpb_c[l]))
        flat = lambda t: t.reshape(b * s, HQ)
        h = _merge(flat(o_a), flat(o_b), flat(o_c), flat(o_d), h, mod, g_pre, g_post_mix[l][None, :],
                   w["wg"], w["wbr"], w_out[l].astype(BF16), tiles_per_batch, last)
        h = _ffn(h, mod, g_pre_ffn[l][None, :], g_post_ffn[l][None, :],
                 w_ffn_in[l].astype(BF16), w_ffn_out[l].astype(BF16), tiles_per_batch, last)
    return h.reshape(b, n_lat, d)
```

```python
import functools

import jax
import jax.numpy as jnp
from jax import lax
from jax.experimental import pallas as pl
from jax.experimental.pallas import tpu as pltpu

F32 = jnp.float32
BF16 = jnp.bfloat16
UINT = jnp.uint32

GRID_W = 64
ROPE_THETA = 10000.0
EPS = 1e-6
NEG = -1e30
DH = 64
N_HEADS = 8
N_BRANCH = 4
Q_LORA = 256
KV_LORA = 128
NOPE_A = 64
ROPE_A = 32
KV_B = 2
WINDOW = 128
KH = 8
KW = 16
KV_D = 2
MOD_CHUNKS = 6

LANES = 128
F32_SUBLANES = 8
VMEM_LIMIT_BYTES = 56 * 1024 * 1024

TM = 256
HEAD_PAD_A = 128
HQ = N_HEADS * DH
S_A = (NOPE_A + ROPE_A) ** -0.5
S_H = DH ** -0.5
LOG2E = 1.4426950408889634
KEY_CHUNK = 1152
ONES_ROWS = 16
VT_ROWS = DH + ONES_ROWS
GQA_HEAD_ORDER = tuple(h for j in range(N_HEADS // 2) for h in (j, j + N_HEADS // 2))


def _cparams(n_axes):
    return pltpu.CompilerParams(
        dimension_semantics=("arbitrary",) * n_axes, vmem_limit_bytes=VMEM_LIMIT_BYTES)


def _resident(shape):
    nd = len(shape)
    return pl.BlockSpec(shape, lambda *_: (0,) * nd, pipeline_mode=pl.Buffered(1))


def _tile_visits(m, tiles_per_batch, latent_only):
    lat = tiles_per_batch - 1
    if latent_only:
        return (m // TM // tiles_per_batch * lat,
                lambda j: (j // lat) * tiles_per_batch + 1 + j % lat,
                lambda j: (j // lat, 1))
    return (m // TM, lambda j: j,
            lambda j: (j // tiles_per_batch, jnp.minimum(j % tiles_per_batch, 1)))


class _StepPipeline:
    def __init__(self, n, row_of, mod_of, d):
        nxt = lambda i: jnp.minimum(i + 1, n - 1)
        prv = lambda i: jnp.maximum(i - 1, 0)
        mod_block = (None, None, MOD_CHUNKS, d)
        self.rows_next = pl.BlockSpec((TM, d), lambda i: (row_of(nxt(i)), 0))
        self.rows_prev = pl.BlockSpec((TM, d), lambda i: (row_of(prv(i)), 0))
        self.mod_next = pl.BlockSpec(mod_block, lambda i: (*mod_of(nxt(i)), 0, 0))
        self.mod_prev = pl.BlockSpec(mod_block, lambda i: (*mod_of(prv(i)), 0, 0))
        self.rows_out = pl.BlockSpec((TM, d), lambda i: (prv(i), 0))


def _mod_spec(d, tiles_per_batch):
    return pl.BlockSpec((None, None, MOD_CHUNKS, d),
                        lambda i: (i // tiles_per_batch, jnp.minimum(i % tiles_per_batch, 1), 0, 0))


def _dot(a, b):
    return jnp.dot(a, b, preferred_element_type=F32)


def _dot_nt(a, b):
    return lax.dot_general(a, b, (((1,), (1,)), ((), ())), preferred_element_type=F32)


def _rms(x, g):
    ms = jnp.mean(x * x, axis=-1, keepdims=True)
    return x * lax.rsqrt(ms + EPS) * g


def _sigmoid(x):
    return 1.0 / (1.0 + jnp.exp(-x))


def _zero_after(*tiles):
    r = None
    for t in tiles:
        b = pltpu.bitcast(t, UINT)
        acc = b[:, 0:LANES]
        for c in range(1, b.shape[1] // LANES):
            acc = acc | b[:, c * LANES:(c + 1) * LANES]
        fold = acc[0:F32_SUBLANES]
        for k in range(1, acc.shape[0] // F32_SUBLANES):
            fold = fold | acc[k * F32_SUBLANES:(k + 1) * F32_SUBLANES]
        r = fold if r is None else r | fold
    half = jnp.iinfo(UINT).bits // 2
    z = lax.shift_right_logical(lax.shift_right_logical(r, jnp.array(half, UINT)), jnp.array(half, UINT))
    return pltpu.bitcast(z, F32)


def _rope(x, tab_ref, shift):
    n = x.shape[-1]
    return (x * tab_ref[0] + pltpu.roll(x, n - shift, 1) * tab_ref[1]
            + pltpu.roll(x, shift, 1) * tab_ref[2])


def _head_rms(x, g, bd):
    x2 = x * x
    hi = x2.astype(BF16)
    lo = (x2 - hi.astype(F32)).astype(BF16)
    ss = _dot(hi, bd) + _dot(lo, bd)
    return x * lax.rsqrt(ss * (1.0 / DH) + EPS) * g


def _pair_scores(q_blk, parts, s_ref, slot):
    m = q_blk.shape[0]
    lane = lax.broadcasted_iota(jnp.int32, (m, LANES), 1)
    zero = jnp.zeros_like(q_blk)
    qs = jnp.concatenate([jnp.where(lane < DH, q_blk, zero), jnp.where(lane >= DH, q_blk, zero)], axis=0)
    layout = []
    off = 0
    for k, load_v, bias in parts:
        n = k.shape[0]
        s = _dot_nt(qs, k)
        s_ref[slot, 0:2 * m, off:off + n] = s if bias is None else s + bias
        layout.append((off, n, load_v))
        off += n
    return m, layout


def _pair_finish(handle, s_ref, slot, sinks=None):
    m, layout = handle
    width = sum(n for _, n, _ in layout)
    s = s_ref[slot, 0:2 * m, 0:width]
    mx = jnp.max(s, axis=1, keepdims=True)
    sink = None
    if sinks is not None:
        row = lax.broadcasted_iota(jnp.int32, (2 * m, 1), 0)
        sink = jnp.where(row < m, sinks[0], sinks[1]) * LOG2E
        mx = jnp.maximum(mx, sink)
    p = jnp.exp2(s - mx)
    l = jnp.sum(p, axis=1, keepdims=True)
    if sink is not None:
        l = l + jnp.exp2(sink - mx)
    pb = p.astype(BF16)
    acc = None
    for off, n, load_v in layout:
        a = _dot(pb[:, off:off + n], load_v())
        acc = a if acc is None else acc + a
    o = acc / l
    lane = lax.broadcasted_iota(jnp.int32, (m, LANES), 1)
    return jnp.where(lane < DH, o[:m], o[m:])


def _pipelined(items, scores_fn, finish_fn):
    cur = scores_fn(items[0], 0)
    for n, item in enumerate(items):
        nxt = scores_fn(items[n + 1], (n + 1) % 2) if n + 1 < len(items) else None
        finish_fn(item, cur, n % 2)
        cur = nxt


def _mod_kernel(c_ref, w_ref, b_ref, o_ref):
    c = c_ref[...]
    sc = (c * _sigmoid(c)).astype(BF16)
    o_ref[...] = _dot(sc, w_ref[...].astype(BF16)) + b_ref[...]


def _modulation(cc, w_mod, b_mod):
    n_layers, d, n = w_mod.shape
    r = cc.shape[0]
    tn = n // 4
    return pl.pallas_call(
        _mod_kernel,
        grid=(n_layers, n // tn),
        in_specs=[
            pl.BlockSpec((r, d), lambda l, j: (0, 0)),
            pl.BlockSpec((None, d, tn), lambda l, j: (l, 0, j)),
            pl.BlockSpec((None, 1, tn), lambda l, j: (l, 0, j)),
        ],
        out_specs=pl.BlockSpec((None, r, tn), lambda l, j: (l, 0, j)),
        out_shape=jax.ShapeDtypeStruct((n_layers, r, n), F32),
        compiler_params=_cparams(2),
        name="modulation",
    )(cc, w_mod, b_mod.reshape(n_layers, 1, n))


def _nbr_bias_kernel(rpb_ref, o_ref):
    pair = pl.program_id(0)
    n_dr = 2 * KH - 1
    n_dc = 2 * KW - 1
    qc = lax.broadcasted_iota(jnp.int32, (GRID_W, 2 * GRID_W), 0)
    lane = lax.broadcasted_iota(jnp.int32, (GRID_W, 2 * GRID_W), 1)
    second = lane >= GRID_W
    kc = jnp.where(second, lane - GRID_W, lane)
    c0 = jnp.clip(qc - KW // 2, 0, GRID_W - KW)
    ok = (kc >= c0) & (kc < c0 + KW)
    dc = kc - qc + (KW - 1)

    for e in range(2):
        h = 2 * pair + e

        def d2_body(d2, carry, h=h, e=e):
            def d_body(d, acc):
                v0 = rpb_ref[(h * n_dr + d2) * n_dc + d]
                v1 = rpb_ref[(h * n_dr + d2 + 1) * n_dc + d]
                return jnp.where(dc == d, jnp.where(second, v1, v0), acc)

            acc = lax.fori_loop(0, n_dc, d_body, jnp.zeros((GRID_W, 2 * GRID_W), F32), unroll=True)
            o_ref[d2, e * GRID_W:(e + 1) * GRID_W, :] = jnp.where(ok, acc * LOG2E, NEG)
            return carry

        lax.fori_loop(0, n_dr - 1, d2_body, 0)


def _nbr_bias(rpb):
    h = rpb.shape[0]
    n_pairs = 2 * KH - 2
    return pl.pallas_call(
        _nbr_bias_kernel,
        grid=(h // 2,),
        in_specs=[pl.BlockSpec(memory_space=pltpu.SMEM)],
        out_specs=pl.BlockSpec((None, n_pairs, 2 * GRID_W, 2 * GRID_W), lambda i: (i, 0, 0, 0)),
        out_shape=jax.ShapeDtypeStruct((h // 2, n_pairs, 2 * GRID_W, 2 * GRID_W), F32),
        compiler_params=_cparams(1),
        name="nbr_bias",
    )(rpb.reshape(-1))


def _store_vt(vt_ref, c, v):
    t = v.T.astype(BF16)
    ones = jnp.ones((ONES_ROWS, v.shape[0]), BF16)
    for e in range(LANES // DH):
        base = (2 * c + e) * VT_ROWS
        vt_ref[base:base + DH, :] = t[e * DH:(e + 1) * DH, :]
        vt_ref[base + DH:base + VT_ROWS, :] = ones


def _proj_kernel(h_ref, mod_ref, g_ref, wa_ref, wb_ref, wc_ref, wd_ref, wuq_ref, wk_ref, wv_ref,
                 gaq_ref, gakv_ref, gdq_ref, gdk_ref, ropea_ref, rope64_ref,
                 ka_ref, qat_ref, vat_ref, pb_ref, pc_ref, kd_ref, qdt_ref, vdt_ref):
    sh1 = mod_ref[0:1, :]
    sc1 = mod_ref[1:2, :]
    u = (_rms(h_ref[...], g_ref[...]) * (1.0 + sc1) + sh1).astype(BF16)

    rb = jnp.where(lax.broadcasted_iota(jnp.int32, (LANES, LANES), 0) < DH, 0, 1)
    cb = jnp.where(lax.broadcasted_iota(jnp.int32, (LANES, LANES), 1) < DH, 0, 1)
    bd = jnp.where(rb == cb, 1.0, 0.0).astype(BF16)

    ya = _dot(u, wa_ref[...])
    yd = _dot(u, wd_ref[...])
    cq = _rms(ya[:, :Q_LORA], gaq_ref[...]).astype(BF16)
    ckv = _rms(ya[:, Q_LORA:Q_LORA + KV_LORA], gakv_ref[...]).astype(BF16)
    kr = _rope(ya[:, Q_LORA + KV_LORA:], ropea_ref, ROPE_A // 4)
    yb = _dot(u, wb_ref[...])
    qa = _dot(cq, wuq_ref[...])
    ka = _dot(ckv, wk_ref[...])
    va = _dot(ckv, wv_ref[...])

    for c in range(HQ // LANES):
        lo, hi = c * LANES, (c + 1) * LANES
        qn = _head_rms(yd[:, lo:hi], gdq_ref[...], bd)
        qdt_ref[lo:hi, :] = (_rope(qn, rope64_ref, DH // 4) * (S_H * LOG2E)).T.astype(BF16)
    kn = _head_rms(yd[:, HQ:HQ + LANES], gdk_ref[...], bd)
    kd_ref[...] = _rope(kn, rope64_ref, DH // 4).astype(BF16)
    _store_vt(vdt_ref, 0, yd[:, HQ + LANES:])

    yc = _dot(u, wc_ref[...])

    for c in range(HQ // LANES):
        lo, hi = c * LANES, (c + 1) * LANES
        pb_ref[:, lo:hi] = (_rope(yb[:, lo:hi], rope64_ref, DH // 4) * (S_H * LOG2E)).astype(BF16)
    pb_ref[:, HQ:HQ + LANES] = _rope(yb[:, HQ:HQ + LANES], rope64_ref, DH // 4).astype(BF16)
    pb_ref[:, HQ + LANES:] = yb[:, HQ + LANES:].astype(BF16)

    for hh in range(N_HEADS):
        lo, hi = hh * HEAD_PAD_A, (hh + 1) * HEAD_PAD_A
        q = _rope(qa[:, lo:hi], ropea_ref, ROPE_A // 4) * (S_A * LOG2E)
        qat_ref[lo:hi, :] = q.T.astype(BF16)
        ka_ref[:, lo:hi] = (ka[:, lo:hi] + kr).astype(BF16)
    for c in range(HQ // LANES):
        lo, hi = c * LANES, (c + 1) * LANES
        _store_vt(vat_ref, c, va[:, lo:hi])

    pc_ref[:, :HQ] = (yc[:, :HQ] * (S_H * LOG2E)).astype(BF16)
    pc_ref[:, HQ:] = yc[:, HQ:].astype(BF16)


def _project(h, mod, g_pre, w, tabs, n_batch, tiles_per_batch):
    m, d = h.shape
    s = tiles_per_batch * TM
    wq_a = N_HEADS * HEAD_PAD_A
    row = lambda i: (i, 0)
    col = lambda i: (i // tiles_per_batch, 0, i % tiles_per_batch)
    tile_pos = lambda i: (0, i % tiles_per_batch, 0)
    in_specs = [
        pl.BlockSpec((TM, d), row),
        _mod_spec(d, tiles_per_batch),
        _resident((1, d)),
        _resident(w["wa"].shape), _resident(w["wb"].shape), _resident(w["wc"].shape),
        _resident(w["wd"].shape), _resident(w["wuq"].shape),
        _resident(w["wk"].shape), _resident(w["wv"].shape),
        _resident((1, Q_LORA)), _resident((1, KV_LORA)), _resident((1, LANES)), _resident((1, LANES)),
        pl.BlockSpec((3, TM, LANES), tile_pos),
        pl.BlockSpec((3, TM, LANES), tile_pos),
    ]

    def rows_out(c):
        return pl.BlockSpec((TM, c), row), jax.ShapeDtypeStruct((m, c), BF16)

    def cols_out(r):
        return pl.BlockSpec((None, r, TM), col), jax.ShapeDtypeStruct((n_batch, r, s), BF16)

    outs = [rows_out(wq_a), cols_out(wq_a), cols_out(N_HEADS * VT_ROWS),
            rows_out(HQ + 2 * LANES), rows_out(3 * HQ),
            rows_out(LANES), cols_out(HQ), cols_out(KV_D * VT_ROWS)]
    return pl.pallas_call(
        _proj_kernel,
        grid=(m // TM,),
        in_specs=in_specs,
        out_specs=[o[0] for o in outs],
        out_shape=[o[1] for o in outs],
        compiler_params=_cparams(1),
        name="proj",
    )(h, mod, g_pre, w["wa"], w["wb"], w["wc"], w["wd"], w["wuq"], w["wk"], w["wv"],
      w["gaq"], w["gakv"], w["gdq"], w["gdk"], tabs["rope_a"], tabs["rope_64"])


def _dense_attn_kernel(ka_ref, qat_ref, vat_ref, kd_ref, qdt_ref, vdt_ref, oa_ref, od_ref, ot_ref, st_ref,
                       *, n_ctx, q_first):
    qi = pl.program_id(1) + q_first
    group_d = N_HEADS // (LANES // DH)
    items = [(mixer, h) for mixer in range(2) for h in range(N_HEADS)]

    def attend(n_keys):
        n_chunks = -(-n_keys // KEY_CHUNK)
        bounds = [(i * n_keys // n_chunks, (i + 1) * n_keys // n_chunks) for i in range(n_chunks)]

        def q_operand(item):
            mixer, h = item
            if mixer == 0:
                return qat_ref[h * LANES:(h + 1) * LANES, :]
            qh = qdt_ref[h * DH:(h + 1) * DH, :]
            z = jnp.zeros_like(qh)
            return jnp.concatenate([qh, z] if h // group_d == 0 else [z, qh], axis=0)

        def scores(n, qt, i):
            mixer, h = items[n]
            lo, hi = bounds[i]
            k = ka_ref[lo:hi, h * LANES:(h + 1) * LANES] if mixer == 0 else kd_ref[lo:hi, :]
            st_ref[n % 2, lo:hi, :] = _dot(k, qt)

        def colmax(n):
            m = None
            for lo, hi in bounds:
                mi = jnp.max(st_ref[n % 2, lo:hi, :], axis=0, keepdims=True)
                m = mi if m is None else jnp.maximum(m, mi)
            return m

        qt = q_operand(items[0])
        for i in range(n_chunks):
            scores(0, qt, i)
        for n, (mixer, h) in enumerate(items):
            vt_ref, g = (vat_ref, h) if mixer == 0 else (vdt_ref, h // group_d)
            m = colmax(n)
            qt = q_operand(items[n + 1]) if n + 1 < len(items) else None
            acc = None
            for i in range(n_chunks):
                if qt is not None:
                    scores(n + 1, qt, i)
                lo, hi = bounds[i]
                p = jnp.exp2(st_ref[n % 2, lo:hi, :] - m).astype(BF16)
                ai = _dot(vt_ref[g * VT_ROWS:(g + 1) * VT_ROWS, lo:hi], p)
                acc = ai if acc is None else acc + ai
            ot_ref[mixer, h * DH:(h + 1) * DH, :] = acc[:DH] / acc[DH:DH + 1]
            if h == N_HEADS - 1:
                o_ref = oa_ref if mixer == 0 else od_ref
                o_ref[...] = ot_ref[mixer].T.astype(o_ref.dtype)

    @pl.when(qi < n_ctx // TM)
    def _():
        attend(n_ctx)

    @pl.when(qi >= n_ctx // TM)
    def _():
        attend(ka_ref.shape[0])


def _dense_attention(ka, qat, vat, kd, qdt, vdt, n_ctx, q_first):
    b, s, _ = ka.shape
    whole = lambda bi, qi: (bi, 0, 0)
    q_tile = lambda bi, qi: (bi, 0, qi + q_first)
    o_tile = lambda bi, qi: (bi, qi + q_first, 0)
    return pl.pallas_call(
        functools.partial(_dense_attn_kernel, n_ctx=n_ctx, q_first=q_first),
        grid=(b, s // TM - q_first),
        in_specs=[
            pl.BlockSpec((None, s, ka.shape[2]), whole),
            pl.BlockSpec((None, qat.shape[1], TM), q_tile),
            pl.BlockSpec((None, vat.shape[1], s), whole),
            pl.BlockSpec((None, s, kd.shape[2]), whole),
            pl.BlockSpec((None, qdt.shape[1], TM), q_tile),
            pl.BlockSpec((None, vdt.shape[1], s), whole),
        ],
        out_specs=[pl.BlockSpec((None, TM, HQ), o_tile)] * 2,
        out_shape=[jax.ShapeDtypeStruct((b, s, HQ), BF16)] * 2,
        scratch_shapes=[pltpu.VMEM((2, HQ, TM), F32), pltpu.VMEM((2, s, TM), F32)],
        compiler_params=_cparams(2),
        name="attn_dense",
    )(ka, qat, vat, kd, qdt, vdt)


def _pair_attn_kernel(sink_ref, qb_ref, kb_ref, vb_ref, qc_ref, kc_ref, vc_ref, tb_ref, ob_ref, oc_ref, s_ref,
                      *, n_ctx, q_first):
    qi = pl.program_id(1) + q_first
    n_lat = kb_ref.shape[0] - n_ctx
    n_blk = HQ // LANES
    lanes = lambda j: slice(j * LANES, (j + 1) * LANES)

    def ctx_b():
        return kb_ref[0:n_ctx, :], lambda: vb_ref[0:n_ctx, :], None

    def ctx_c(j):
        return kc_ref[0:n_ctx, lanes(j)], lambda: vc_ref[0:n_ctx, lanes(j)], None

    def sinks(j):
        return sink_ref[GQA_HEAD_ORDER[2 * j]], sink_ref[GQA_HEAD_ORDER[2 * j + 1]]

    @pl.when(qi < n_ctx // TM)
    def _():
        def scores(item, slot):
            mixer, j = item
            if mixer == "b":
                return _pair_scores(qb_ref[:, lanes(j)], [ctx_b()], s_ref, slot)
            return _pair_scores(qc_ref[:, lanes(j)], [ctx_c(j)], s_ref, slot)

        def finish(item, handle, slot):
            mixer, j = item
            if mixer == "b":
                ob_ref[:, lanes(j)] = _pair_finish(handle, s_ref, slot, sinks(j)).astype(ob_ref.dtype)
            else:
                oc_ref[:, lanes(j)] = _pair_finish(handle, s_ref, slot).astype(oc_ref.dtype)

        _pipelined([(mixer, j) for mixer in "bc" for j in range(n_blk)], scores, finish)

    @pl.when(qi >= n_ctx // TM)
    def _():
        span_b = TM + 2 * WINDOW
        q0 = (qi - n_ctx // TM) * TM
        start = pl.multiple_of(jnp.clip(q0 - WINDOW, 0, n_lat - span_b), WINDOW)
        dist = (lax.broadcasted_iota(jnp.int32, (2 * TM, span_b), 0)
                - lax.broadcasted_iota(jnp.int32, (2 * TM, span_b), 1) + (q0 - start))
        dist = jnp.where(lax.broadcasted_iota(jnp.int32, (2 * TM, span_b), 0) >= TM, dist - TM, dist)
        bias_b = jnp.where(jnp.abs(dist) <= WINDOW, 0.0, NEG)
        rows_b = pl.ds(n_ctx + start, span_b)
        grid_rows = n_lat // GRID_W
        rows_per_tile = TM // GRID_W
        span_c = KH * GRID_W

        def scores(item, slot):
            if item[0] == "b":
                j = item[1]
                return _pair_scores(qb_ref[:, lanes(j)],
                                    [(kb_ref[rows_b, :], lambda: vb_ref[rows_b, :], bias_b), ctx_b()], s_ref, slot)
            _, rr, j = item
            r = (qi - n_ctx // TM) * rows_per_tile + rr
            r0 = jnp.clip(r - KH // 2, 0, grid_rows - KH)
            delta = r - r0
            krows = pl.ds(pl.multiple_of(n_ctx + r0 * GRID_W, GRID_W), span_c)
            bias = jnp.concatenate(
                [tb_ref[j, 2 * jj - delta + (KH - 1)] for jj in range(KH // 2)], axis=1)
            return _pair_scores(qc_ref[rr * GRID_W:(rr + 1) * GRID_W, lanes(j)],
                                [(kc_ref[krows, lanes(j)], lambda: vc_ref[krows, lanes(j)], bias), ctx_c(j)],
                                s_ref, slot)

        def finish(item, handle, slot):
            if item[0] == "b":
                j = item[1]
                ob_ref[:, lanes(j)] = _pair_finish(handle, s_ref, slot, sinks(j)).astype(ob_ref.dtype)
            else:
                _, rr, j = item
                oc_ref[rr * GRID_W:(rr + 1) * GRID_W, lanes(j)] = (
                    _pair_finish(handle, s_ref, slot).astype(oc_ref.dtype))

        items = [("b", j) for j in range(n_blk)]
        items += [("c", rr, j) for rr in range(rows_per_tile) for j in range(n_blk)]
        _pipelined(items, scores, finish)


def _pair_attention(pb, pc, sink, bias_tab, n_ctx, q_first):
    b, s, _ = pb.shape
    q_tile = lambda bi, qi: (bi, qi + q_first, 0)
    col_blk = lambda c: (lambda bi, qi: (bi, 0, c))
    return pl.pallas_call(
        functools.partial(_pair_attn_kernel, n_ctx=n_ctx, q_first=q_first),
        grid=(b, s // TM - q_first),
        in_specs=[
            pl.BlockSpec(memory_space=pltpu.SMEM),
            pl.BlockSpec((None, TM, HQ), q_tile),
            pl.BlockSpec((None, s, LANES), col_blk(HQ // LANES)),
            pl.BlockSpec((None, s, LANES), col_blk(HQ // LANES + 1)),
            pl.BlockSpec((None, TM, HQ), q_tile),
            pl.BlockSpec((None, s, HQ), col_blk(1)),
            pl.BlockSpec((None, s, HQ), col_blk(2)),
            _resident(bias_tab.shape),
        ],
        out_specs=[pl.BlockSpec((None, TM, HQ), q_tile)] * 2,
        out_shape=[jax.ShapeDtypeStruct((b, s, HQ), BF16)] * 2,
        scratch_shapes=[pltpu.VMEM((2, 2 * TM, max(TM + 2 * WINDOW, KH * GRID_W) + n_ctx), F32)],
        compiler_params=_cparams(2),
        name="attn_pair",
    )(sink, pb, pb, pb, pc, pc, pc, bias_tab)


def _merge_kernel(oa_ref, ob_ref, oc_ref, od_ref, h_ref, mod_ref, gpre_ref, gpost_ref,
                  wg_ref, wbr_ref, wout_ref, o_ref, acc_ref):
    d_model = h_ref.shape[1]
    cw = 256
    h = h_ref[...]
    sh1 = mod_ref[0:1, :]
    sc1 = mod_ref[1:2, :]
    ga1 = mod_ref[2:3, :]
    branches = (oa_ref, ob_ref, oc_ref, od_ref)

    def up_proj(n, c):
        return _dot(branches[n][...], wbr_ref[n, :, c * cw:(c + 1) * cw])

    up0 = [up_proj(n, 0) for n in range(N_BRANCH)]
    u = (_rms(h, gpre_ref[...]) * (1.0 + sc1) + sh1).astype(BF16)
    for c in range(d_model // cw):
        acc = None
        for n in range(N_BRANCH):
            lo = n * d_model + c * cw
            gate = _dot(u, wg_ref[:, lo:lo + cw])
            up = up0[n] if c == 0 else up_proj(n, c)
            term = _sigmoid(gate) * up
            acc = term if acc is None else acc + term
        acc_ref[:, c * cw:(c + 1) * cw] = acc.astype(BF16)
    y = _dot(acc_ref[...], wout_ref[...])
    o_ref[...] = h + ga1 * _rms(y, gpost_ref[...])


def _merge(o_a, o_b, o_c, o_d, h, mod, g_pre, g_post, wg, wbr, wout, tiles_per_batch, latent_only):
    m, d = h.shape
    n, row_of, mod_of = _tile_visits(m, tiles_per_batch, latent_only)
    row = lambda i: (row_of(i), 0)
    return pl.pallas_call(
        _merge_kernel,
        grid=(n,),
        in_specs=[pl.BlockSpec((TM, HQ), row)] * N_BRANCH + [
            pl.BlockSpec((TM, d), row),
            pl.BlockSpec((None, None, MOD_CHUNKS, d), lambda i: (*mod_of(i), 0, 0)),
            _resident((1, d)), _resident((1, d)),
            _resident(wg.shape), _resident(wbr.shape), _resident(wout.shape),
        ],
        out_specs=pl.BlockSpec((TM, d), row),
        out_shape=jax.ShapeDtypeStruct((m, d), F32),
        scratch_shapes=[pltpu.VMEM((TM, d), BF16)],
        input_output_aliases={4: 0},
        compiler_params=_cparams(1),
        name="merge",
    )(o_a, o_b, o_c, o_d, h, mod, g_pre, g_post, wg, wbr, wout)


def _ffn_kernel(hn_ref, hp_ref, modn_ref, modp_ref, gpre_ref, gpost_ref, w1_ref, w2_ref, o_ref,
                u_ref, y_ref, act_ref):
    i = pl.program_id(0)
    d_ff = w2_ref.shape[0]
    cw = 256
    cur = i % 2
    oth = 1 - cur

    def pre_norm(h_ref, mod_ref):
        return _rms(h_ref[...], gpre_ref[...]) * (1.0 + mod_ref[4:5, :]) + mod_ref[3:4, :]

    @pl.when(i == 0)
    def _():
        u_ref[0] = pre_norm(hp_ref, modp_ref).astype(BF16)
        y_ref[1] = jnp.zeros(y_ref.shape[1:], F32)

    out = hp_ref[...] + modp_ref[5:6, :] * _rms(y_ref[oth], gpost_ref[...])
    o_ref[...] = out
    u_next = pre_norm(hn_ref, modn_ref)
    for c in range(d_ff // cw):
        a = _dot(u_ref[cur], w1_ref[:, c * cw:(c + 1) * cw])
        b = _dot(u_ref[cur], w1_ref[:, d_ff + c * cw:d_ff + (c + 1) * cw])
        act_ref[:, c * cw:(c + 1) * cw] = (a * _sigmoid(a) * b).astype(BF16)
    u_ref[oth] = u_next.astype(BF16)
    y = _dot(act_ref[...], w2_ref[...])
    y_ref[cur] = y
    r0, c0 = y.shape[0] - F32_SUBLANES, y.shape[1] - LANES
    y_ref[cur, r0:, c0:] = y[r0:, c0:] + _zero_after(u_next, out)


def _ffn(h, mod, g_pre, g_post, w1, w2, tiles_per_batch, latent_only):
    m, d = h.shape
    n, row_of, mod_of = _tile_visits(m, tiles_per_batch, latent_only)
    p = _StepPipeline(n, row_of, mod_of, d)
    return pl.pallas_call(
        _ffn_kernel,
        grid=(n + 1,),
        in_specs=[p.rows_next, p.rows_prev, p.mod_next, p.mod_prev,
                  _resident((1, d)), _resident((1, d)), _resident(w1.shape), _resident(w2.shape)],
        out_specs=p.rows_out,
        out_shape=jax.ShapeDtypeStruct((n * TM, d), F32),
        scratch_shapes=[pltpu.VMEM((2, TM, d), BF16), pltpu.VMEM((2, TM, d), F32),
                        pltpu.VMEM((TM, w2.shape[0]), BF16)],
        compiler_params=_cparams(1),
        name="ffn",
    )(h, h, mod, mod, g_pre, g_post, w1, w2)


def _rope_tables(n_ctx, n_lat):
    t = jnp.arange(n_lat)
    rows_pos = (t // GRID_W).astype(F32)
    cols_pos = (t % GRID_W).astype(F32)

    def block(d):
        q = d // 4
        inv = ROPE_THETA ** (-jnp.arange(q, dtype=F32) * 2.0 / (d // 2))
        ar = rows_pos[:, None] * inv[None, :]
        ac = cols_pos[:, None] * inv[None, :]
        z = jnp.zeros_like(ar)
        cos = jnp.concatenate([jnp.cos(ar), jnp.cos(ar), jnp.cos(ac), jnp.cos(ac)], axis=1)
        up = jnp.concatenate([-jnp.sin(ar), z, -jnp.sin(ac), z], axis=1)
        dn = jnp.concatenate([z, jnp.sin(ar), z, jnp.sin(ac)], axis=1)
        return cos, up, dn

    def with_ctx(tab, fill):
        return jnp.concatenate([jnp.full((n_ctx, tab.shape[1]), fill, F32), tab], axis=0)

    c64, u64, d64 = block(DH)
    rope_64 = jnp.stack([with_ctx(jnp.tile(c64, (1, LANES // DH)), 1.0),
                         with_ctx(jnp.tile(u64, (1, LANES // DH)), 0.0),
                         with_ctx(jnp.tile(d64, (1, LANES // DH)), 0.0)])
    ca, ua, da = block(ROPE_A)
    pad_lo = NOPE_A
    pad_hi = HEAD_PAD_A - NOPE_A - ROPE_A

    def widen(tab, fill):
        return jnp.concatenate([jnp.full((n_lat, pad_lo), fill, F32), tab,
                                jnp.full((n_lat, pad_hi), fill, F32)], axis=1)

    rope_a = jnp.stack([with_ctx(widen(ca, 1.0), 1.0), with_ctx(widen(ua, 0.0), 0.0),
                        with_ctx(widen(da, 0.0), 0.0)])
    return {"rope_a": rope_a, "rope_64": rope_64}


def _permute_heads(w, axis, order):
    shape = w.shape
    w = w.reshape(shape[:axis] + (N_HEADS, DH) + shape[axis + 1:])
    w = jnp.take(w, jnp.asarray(order), axis=axis)
    return w.reshape(shape)


def _layer_weights(l, w_in, g_a_q, g_a_kv, w_a_uq, w_a_ukv, g_d_q, g_d_k, w_branch):
    d = w_in.shape[1]
    wl = w_in[l]
    o = 0
    a_cq = wl[:, o:o + Q_LORA]; o += Q_LORA
    a_ckv = wl[:, o:o + KV_LORA]; o += KV_LORA
    a_kr = wl[:, o:o + ROPE_A]; o += ROPE_A
    w_bq = wl[:, o:o + HQ]; o += HQ
    w_bkv = wl[:, o:o + 2 * KV_B * DH]; o += 2 * KV_B * DH
    w_c = wl[:, o:o + 3 * HQ]; o += 3 * HQ
    w_d = wl[:, o:o + HQ + 2 * KV_D * DH]; o += HQ + 2 * KV_D * DH
    w_g = wl[:, o:]
    zeros = lambda n: jnp.zeros((d, n), wl.dtype)
    wa = jnp.concatenate([a_cq, a_ckv, zeros(NOPE_A), a_kr, zeros(HEAD_PAD_A - NOPE_A - ROPE_A)], axis=1)
    w_b = jnp.concatenate([_permute_heads(w_bq, 1, GQA_HEAD_ORDER), w_bkv], axis=1)
    uq = w_a_uq[l].reshape(Q_LORA, N_HEADS, NOPE_A + ROPE_A)
    uq = jnp.pad(uq, ((0, 0), (0, 0), (0, HEAD_PAD_A - NOPE_A - ROPE_A))).reshape(Q_LORA, N_HEADS * HEAD_PAD_A)
    ukv = w_a_ukv[l].reshape(KV_LORA, N_HEADS, NOPE_A + DH)
    wk = jnp.pad(ukv[:, :, :NOPE_A], ((0, 0), (0, 0), (0, HEAD_PAD_A - NOPE_A))).reshape(KV_LORA, N_HEADS * HEAD_PAD_A)
    wv = ukv[:, :, NOPE_A:].reshape(KV_LORA, HQ)
    wbr = w_branch[l]
    wbr = jnp.stack([wbr[0], _permute_heads(wbr[1], 0, GQA_HEAD_ORDER), wbr[2], wbr[3]])
    cast = lambda x: x.astype(BF16)
    return {
        "wa": cast(wa), "wb": cast(w_b), "wc": cast(w_c), "wd": cast(w_d), "wg": cast(w_g),
        "wuq": cast(uq), "wk": cast(wk), "wv": cast(wv), "wbr": cast(wbr),
        "gaq": g_a_q[l][None, :], "gakv": g_a_kv[l][None, :],
        "gdq": jnp.tile(g_d_q[l], LANES // DH)[None, :], "gdk": jnp.tile(g_d_k[l], LANES // DH)[None, :],
    }


def kernel(x, c, ctx, c_ctx, w_mod, b_mod, g_pre_mix, g_post_mix, g_pre_ffn, g_post_ffn,
           w_in, g_a_q, g_a_kv, w_a_uq, w_a_ukv, sink_b, rpb_c, g_d_q, g_d_k,
           w_branch, w_out, w_ffn_in, w_ffn_out):
    b, n_lat, d = x.shape
    n_ctx = ctx.shape[1]
    s = n_ctx + n_lat
    depth = w_mod.shape[0]
    assert n_ctx == TM and n_lat % TM == 0 and n_lat >= TM + 2 * WINDOW
    assert n_lat % GRID_W == 0 and n_lat // GRID_W >= KH
    assert KV_B * DH == LANES and KV_D * DH == LANES
    tiles_per_batch = s // TM

    n_rows = -(-(b + 1) // F32_SUBLANES) * F32_SUBLANES
    cc = jnp.zeros((n_rows, d), F32).at[:b].set(c).at[b].set(c_ctx)
    mods = _modulation(cc, w_mod, b_mod)
    mod_x = mods[:, :b].reshape(depth, b, 1, MOD_CHUNKS, d)
    mod_c = jnp.broadcast_to(mods[:, b].reshape(depth, 1, 1, MOD_CHUNKS, d), mod_x.shape)
    mod_tab = jnp.concatenate([mod_c, mod_x], axis=2)

    tabs = _rope_tables(n_ctx, n_lat)
    h = jnp.concatenate([ctx, x], axis=1).reshape(b * s, d)

    for l in range(depth):
        w = _layer_weights(l, w_in, g_a_q, g_a_kv, w_a_uq, w_a_ukv, g_d_q, g_d_k, w_branch)
        mod = mod_tab[l]
        g_pre = g_pre_mix[l][None, :]
        last = l == depth - 1
        q_first = n_ctx // TM if last else 0
        ka, qat, vat, pb, pc, kd, qdt, vdt = _project(h, mod, g_pre, w, tabs, b, tiles_per_batch)
        o_a, o_d = _dense_attention(ka.reshape(b, s, -1), qat, vat, kd.reshape(b, s, -1), qdt, vdt,
                                    n_ctx, q_first)
        o_b, o_c = _pair_attention(pb.reshape(b, s, -1), pc.reshape(b, s, -1), sink_b[l], _nbr_bias(rpb_c[l]),
                                   n_ctx, q_first)
        flat = lambda t: t.reshape(b * s, HQ)
        h = _merge(flat(o_a), flat(o_b), flat(o_c), flat(o_d), h, mod, g_pre, g_post_mix[l][None, :],
                   w["wg"], w["wbr"], w_out[l].astype(BF16), tiles_per_batch, last)
        h = _ffn(h, mod, g_pre_ffn[l][None, :], g_post_ffn[l][None, :],
                 w_ffn_in[l].astype(BF16), w_ffn_out[l].astype(BF16), tiles_per_batch, last)
    return h.reshape(b, n_lat, d)
```

```python
import functools

import jax
import jax.numpy as jnp
from jax import lax
from jax.experimental import pallas as pl
from jax.experimental.pallas import tpu as pltpu

F32 = jnp.float32
BF16 = jnp.bfloat16
UINT = jnp.uint32

GRID_W = 64
ROPE_THETA = 10000.0
EPS = 1e-6
NEG = -1e30
DH = 64
N_HEADS = 8
N_BRANCH = 4
Q_LORA = 256
KV_LORA = 128
NOPE_A = 64
ROPE_A = 32
KV_B = 2
WINDOW = 128
KH = 8
KW = 16
KV_D = 2
MOD_CHUNKS = 6

LANES = 128
F32_SUBLANES = 8
VMEM_LIMIT_BYTES = 56 * 1024 * 1024

TM = 256
HEAD_PAD_A = 128
HQ = N_HEADS * DH
S_A = (NOPE_A + ROPE_A) ** -0.5
S_H = DH ** -0.5
LOG2E = 1.4426950408889634
KEY_CHUNK = 1152
DENSE_SLOTS = 2
PAIR_SLOTS = 3
ONES_ROWS = 16
VT_ROWS = DH + ONES_ROWS
GQA_HEAD_ORDER = tuple(h for j in range(N_HEADS // 2) for h in (j, j + N_HEADS // 2))


def _cparams(n_axes):
    return pltpu.CompilerParams(
        dimension_semantics=("arbitrary",) * n_axes, vmem_limit_bytes=VMEM_LIMIT_BYTES)


def _resident(shape):
    nd = len(shape)
    return pl.BlockSpec(shape, lambda *_: (0,) * nd, pipeline_mode=pl.Buffered(1))


def _tile_visits(m, tiles_per_batch, latent_only):
    lat = tiles_per_batch - 1
    if latent_only:
        return (m // TM // tiles_per_batch * lat,
                lambda j: (j // lat) * tiles_per_batch + 1 + j % lat,
                lambda j: (j // lat, 1))
    return (m // TM, lambda j: j,
            lambda j: (j // tiles_per_batch, jnp.minimum(j % tiles_per_batch, 1)))


class _StepPipeline:
    def __init__(self, n, row_of, mod_of, d):
        nxt = lambda i: jnp.minimum(i + 1, n - 1)
        prv = lambda i: jnp.maximum(i - 1, 0)
        mod_block = (None, None, MOD_CHUNKS, d)
        self.rows_next = pl.BlockSpec((TM, d), lambda i: (row_of(nxt(i)), 0))
        self.rows_prev = pl.BlockSpec((TM, d), lambda i: (row_of(prv(i)), 0))
        self.mod_next = pl.BlockSpec(mod_block, lambda i: (*mod_of(nxt(i)), 0, 0))
        self.mod_prev = pl.BlockSpec(mod_block, lambda i: (*mod_of(prv(i)), 0, 0))
        self.rows_out = pl.BlockSpec((TM, d), lambda i: (prv(i), 0))


def _mod_spec(d, tiles_per_batch):
    return pl.BlockSpec((None, None, MOD_CHUNKS, d),
                        lambda i: (i // tiles_per_batch, jnp.minimum(i % tiles_per_batch, 1), 0, 0))


def _dot(a, b):
    return jnp.dot(a, b, preferred_element_type=F32)


def _dot_nt(a, b):
    return lax.dot_general(a, b, (((1,), (1,)), ((), ())), preferred_element_type=F32)


def _rms(x, g):
    ms = jnp.mean(x * x, axis=-1, keepdims=True)
    return x * lax.rsqrt(ms + EPS) * g


def _sigmoid(x):
    return 1.0 / (1.0 + jnp.exp(-x))


def _zero_after(*tiles):
    r = None
    for t in tiles:
        b = pltpu.bitcast(t, UINT)
        acc = b[:, 0:LANES]
        for c in range(1, b.shape[1] // LANES):
            acc = acc | b[:, c * LANES:(c + 1) * LANES]
        fold = acc[0:F32_SUBLANES]
        for k in range(1, acc.shape[0] // F32_SUBLANES):
            fold = fold | acc[k * F32_SUBLANES:(k + 1) * F32_SUBLANES]
        r = fold if r is None else r | fold
    half = jnp.iinfo(UINT).bits // 2
    z = lax.shift_right_logical(lax.shift_right_logical(r, jnp.array(half, UINT)), jnp.array(half, UINT))
    return pltpu.bitcast(z, F32)


def _rope(x, tab_ref, shift):
    n = x.shape[-1]
    return (x * tab_ref[0] + pltpu.roll(x, n - shift, 1) * tab_ref[1]
            + pltpu.roll(x, shift, 1) * tab_ref[2])


def _head_rms(x, g, bd):
    x2 = x * x
    hi = x2.astype(BF16)
    lo = (x2 - hi.astype(F32)).astype(BF16)
    ss = _dot(hi, bd) + _dot(lo, bd)
    return x * lax.rsqrt(ss * (1.0 / DH) + EPS) * g


def _pair_scores(q_blk, parts, s_ref, slot):
    m = q_blk.shape[0]
    lane = lax.broadcasted_iota(jnp.int32, (m, LANES), 1)
    zero = jnp.zeros_like(q_blk)
    qs = jnp.concatenate([jnp.where(lane < DH, q_blk, zero), jnp.where(lane >= DH, q_blk, zero)], axis=0)
    layout = []
    off = 0
    for k, load_v, bias in parts:
        n = k.shape[0]
        s = _dot_nt(qs, k)
        s_ref[slot, 0:2 * m, off:off + n] = s if bias is None else s + bias
        layout.append((off, n, load_v))
        off += n
    return m, layout


def _pair_finish(handle, s_ref, slot, sinks=None):
    m, layout = handle
    width = sum(n for _, n, _ in layout)
    s = s_ref[slot, 0:2 * m, 0:width]
    mx = jnp.max(s, axis=1, keepdims=True)
    sink = None
    if sinks is not None:
        row = lax.broadcasted_iota(jnp.int32, (2 * m, 1), 0)
        sink = jnp.where(row < m, sinks[0], sinks[1]) * LOG2E
        mx = jnp.maximum(mx, sink)
    p = jnp.exp2(s - mx)
    l = jnp.sum(p, axis=1, keepdims=True)
    if sink is not None:
        l = l + jnp.exp2(sink - mx)
    pb = p.astype(BF16)
    acc = None
    for off, n, load_v in layout:
        a = _dot(pb[:, off:off + n], load_v())
        acc = a if acc is None else acc + a
    o = acc / l
    lane = lax.broadcasted_iota(jnp.int32, (m, LANES), 1)
    return jnp.where(lane < DH, o[:m], o[m:])


def _pipelined(items, scores_fn, finish_fn):
    ahead = PAIR_SLOTS - 1
    handles = [scores_fn(items[n], n % PAIR_SLOTS) for n in range(min(ahead, len(items)))]
    for n, item in enumerate(items):
        if n + ahead < len(items):
            handles.append(scores_fn(items[n + ahead], (n + ahead) % PAIR_SLOTS))
        finish_fn(item, handles[n], n % PAIR_SLOTS)


def _mod_kernel(c_ref, w_ref, b_ref, o_ref):
    c = c_ref[...]
    sc = (c * _sigmoid(c)).astype(BF16)
    o_ref[...] = _dot(sc, w_ref[...].astype(BF16)) + b_ref[...]


def _modulation(cc, w_mod, b_mod):
    n_layers, d, n = w_mod.shape
    r = cc.shape[0]
    tn = n // 4
    return pl.pallas_call(
        _mod_kernel,
        grid=(n_layers, n // tn),
        in_specs=[
            pl.BlockSpec((r, d), lambda l, j: (0, 0)),
            pl.BlockSpec((None, d, tn), lambda l, j: (l, 0, j)),
            pl.BlockSpec((None, 1, tn), lambda l, j: (l, 0, j)),
        ],
        out_specs=pl.BlockSpec((None, r, tn), lambda l, j: (l, 0, j)),
        out_shape=jax.ShapeDtypeStruct((n_layers, r, n), F32),
        compiler_params=_cparams(2),
        name="modulation",
    )(cc, w_mod, b_mod.reshape(n_layers, 1, n))


def _nbr_bias_kernel(rpb_ref, o_ref):
    pair = pl.program_id(0)
    n_dr = 2 * KH - 1
    n_dc = 2 * KW - 1
    qc = lax.broadcasted_iota(jnp.int32, (GRID_W, 2 * GRID_W), 0)
    lane = lax.broadcasted_iota(jnp.int32, (GRID_W, 2 * GRID_W), 1)
    second = lane >= GRID_W
    kc = jnp.where(second, lane - GRID_W, lane)
    c0 = jnp.clip(qc - KW // 2, 0, GRID_W - KW)
    ok = (kc >= c0) & (kc < c0 + KW)
    dc = kc - qc + (KW - 1)

    for e in range(2):
        h = 2 * pair + e

        def d2_body(d2, carry, h=h, e=e):
            def d_body(d, acc):
                v0 = rpb_ref[(h * n_dr + d2) * n_dc + d]
                v1 = rpb_ref[(h * n_dr + d2 + 1) * n_dc + d]
                return jnp.where(dc == d, jnp.where(second, v1, v0), acc)

            acc = lax.fori_loop(0, n_dc, d_body, jnp.zeros((GRID_W, 2 * GRID_W), F32), unroll=True)
            o_ref[d2, e * GRID_W:(e + 1) * GRID_W, :] = jnp.where(ok, acc * LOG2E, NEG)
            return carry

        lax.fori_loop(0, n_dr - 1, d2_body, 0)


def _nbr_bias(rpb):
    h = rpb.shape[0]
    n_pairs = 2 * KH - 2
    return pl.pallas_call(
        _nbr_bias_kernel,
        grid=(h // 2,),
        in_specs=[pl.BlockSpec(memory_space=pltpu.SMEM)],
        out_specs=pl.BlockSpec((None, n_pairs, 2 * GRID_W, 2 * GRID_W), lambda i: (i, 0, 0, 0)),
        out_shape=jax.ShapeDtypeStruct((h // 2, n_pairs, 2 * GRID_W, 2 * GRID_W), F32),
        compiler_params=_cparams(1),
        name="nbr_bias",
    )(rpb.reshape(-1))


def _store_vt(vt_ref, c, v):
    t = v.T.astype(BF16)
    ones = jnp.ones((ONES_ROWS, v.shape[0]), BF16)
    for e in range(LANES // DH):
        base = (2 * c + e) * VT_ROWS
        vt_ref[base:base + DH, :] = t[e * DH:(e + 1) * DH, :]
        vt_ref[base + DH:base + VT_ROWS, :] = ones


def _proj_kernel(h_ref, mod_ref, g_ref, wa_ref, wb_ref, wc_ref, wd_ref, wuq_ref, wk_ref, wv_ref,
                 gaq_ref, gakv_ref, gdq_ref, gdk_ref, ropea_ref, rope64_ref,
                 ka_ref, qat_ref, vat_ref, pb_ref, pc_ref, kd_ref, qdt_ref, vdt_ref):
    sh1 = mod_ref[0:1, :]
    sc1 = mod_ref[1:2, :]
    u = (_rms(h_ref[...], g_ref[...]) * (1.0 + sc1) + sh1).astype(BF16)

    rb = jnp.where(lax.broadcasted_iota(jnp.int32, (LANES, LANES), 0) < DH, 0, 1)
    cb = jnp.where(lax.broadcasted_iota(jnp.int32, (LANES, LANES), 1) < DH, 0, 1)
    bd = jnp.where(rb == cb, 1.0, 0.0).astype(BF16)

    ya = _dot(u, wa_ref[...])
    yd = _dot(u, wd_ref[...])
    cq = _rms(ya[:, :Q_LORA], gaq_ref[...]).astype(BF16)
    ckv = _rms(ya[:, Q_LORA:Q_LORA + KV_LORA], gakv_ref[...]).astype(BF16)
    kr = _rope(ya[:, Q_LORA + KV_LORA:], ropea_ref, ROPE_A // 4)
    yb = _dot(u, wb_ref[...])
    qa = _dot(cq, wuq_ref[...])
    ka = _dot(ckv, wk_ref[...])
    va = _dot(ckv, wv_ref[...])

    for c in range(HQ // LANES):
        lo, hi = c * LANES, (c + 1) * LANES
        qn = _head_rms(yd[:, lo:hi], gdq_ref[...], bd)
        qdt_ref[lo:hi, :] = (_rope(qn, rope64_ref, DH // 4) * (S_H * LOG2E)).T.astype(BF16)
    kn = _head_rms(yd[:, HQ:HQ + LANES], gdk_ref[...], bd)
    kd_ref[...] = _rope(kn, rope64_ref, DH // 4).astype(BF16)
    _store_vt(vdt_ref, 0, yd[:, HQ + LANES:])

    yc = _dot(u, wc_ref[...])

    for c in range(HQ // LANES):
        lo, hi = c * LANES, (c + 1) * LANES
        pb_ref[:, lo:hi] = (_rope(yb[:, lo:hi], rope64_ref, DH // 4) * (S_H * LOG2E)).astype(BF16)
    pb_ref[:, HQ:HQ + LANES] = _rope(yb[:, HQ:HQ + LANES], rope64_ref, DH // 4).astype(BF16)
    pb_ref[:, HQ + LANES:] = yb[:, HQ + LANES:].astype(BF16)

    for hh in range(N_HEADS):
        lo, hi = hh * HEAD_PAD_A, (hh + 1) * HEAD_PAD_A
        q = _rope(qa[:, lo:hi], ropea_ref, ROPE_A // 4) * (S_A * LOG2E)
        qat_ref[lo:hi, :] = q.T.astype(BF16)
        ka_ref[:, lo:hi] = (ka[:, lo:hi] + kr).astype(BF16)
    for c in range(HQ // LANES):
        lo, hi = c * LANES, (c + 1) * LANES
        _store_vt(vat_ref, c, va[:, lo:hi])

    pc_ref[:, :HQ] = (yc[:, :HQ] * (S_H * LOG2E)).astype(BF16)
    pc_ref[:, HQ:] = yc[:, HQ:].astype(BF16)


def _project(h, mod, g_pre, w, tabs, n_batch, tiles_per_batch):
    m, d = h.shape
    s = tiles_per_batch * TM
    wq_a = N_HEADS * HEAD_PAD_A
    row = lambda i: (i, 0)
    col = lambda i: (i // tiles_per_batch, 0, i % tiles_per_batch)
    tile_pos = lambda i: (0, i % tiles_per_batch, 0)
    in_specs = [
        pl.BlockSpec((TM, d), row),
        _mod_spec(d, tiles_per_batch),
        _resident((1, d)),
        _resident(w["wa"].shape), _resident(w["wb"].shape), _resident(w["wc"].shape),
        _resident(w["wd"].shape), _resident(w["wuq"].shape),
        _resident(w["wk"].shape), _resident(w["wv"].shape),
        _resident((1, Q_LORA)), _resident((1, KV_LORA)), _resident((1, LANES)), _resident((1, LANES)),
        pl.BlockSpec((3, TM, LANES), tile_pos),
        pl.BlockSpec((3, TM, LANES), tile_pos),
    ]

    def rows_out(c):
        return pl.BlockSpec((TM, c), row), jax.ShapeDtypeStruct((m, c), BF16)

    def cols_out(r):
        return pl.BlockSpec((None, r, TM), col), jax.ShapeDtypeStruct((n_batch, r, s), BF16)

    outs = [rows_out(wq_a), cols_out(wq_a), cols_out(N_HEADS * VT_ROWS),
            rows_out(HQ + 2 * LANES), rows_out(3 * HQ),
            rows_out(LANES), cols_out(HQ), cols_out(KV_D * VT_ROWS)]
    return pl.pallas_call(
        _proj_kernel,
        grid=(m // TM,),
        in_specs=in_specs,
        out_specs=[o[0] for o in outs],
        out_shape=[o[1] for o in outs],
        compiler_params=_cparams(1),
        name="proj",
    )(h, mod, g_pre, w["wa"], w["wb"], w["wc"], w["wd"], w["wuq"], w["wk"], w["wv"],
      w["gaq"], w["gakv"], w["gdq"], w["gdk"], tabs["rope_a"], tabs["rope_64"])


def _dense_attn_kernel(ka_ref, qat_ref, vat_ref, kd_ref, qdt_ref, vdt_ref, oa_ref, od_ref, ot_ref, st_ref,
                       *, n_ctx, q_first):
    qi = pl.program_id(1) + q_first
    group_d = N_HEADS // (LANES // DH)
    items = [(mixer, h) for mixer in range(2) for h in range(N_HEADS)]

    def attend(n_keys):
        n_chunks = -(-n_keys // KEY_CHUNK)
        bounds = [(i * n_keys // n_chunks, (i + 1) * n_keys // n_chunks) for i in range(n_chunks)]

        def q_operand(item):
            mixer, h = item
            if mixer == 0:
                return qat_ref[h * LANES:(h + 1) * LANES, :]
            qh = qdt_ref[h * DH:(h + 1) * DH, :]
            z = jnp.zeros_like(qh)
            return jnp.concatenate([qh, z] if h // group_d == 0 else [z, qh], axis=0)

        def scores(n, qt, i):
            mixer, h = items[n]
            lo, hi = bounds[i]
            k = ka_ref[lo:hi, h * LANES:(h + 1) * LANES] if mixer == 0 else kd_ref[lo:hi, :]
            st_ref[n % DENSE_SLOTS, lo:hi, :] = _dot(k, qt)

        def colmax(n):
            m = None
            for lo, hi in bounds:
                mi = jnp.max(st_ref[n % DENSE_SLOTS, lo:hi, :], axis=0, keepdims=True)
                m = mi if m is None else jnp.maximum(m, mi)
            return m

        ahead = DENSE_SLOTS - 1
        for n0 in range(ahead):
            qt = q_operand(items[n0])
            for i in range(n_chunks):
                scores(n0, qt, i)
        for n, (mixer, h) in enumerate(items):
            vt_ref, g = (vat_ref, h) if mixer == 0 else (vdt_ref, h // group_d)
            m = colmax(n)
            qt = q_operand(items[n + ahead]) if n + ahead < len(items) else None
            acc = None
            for i in range(n_chunks):
                if qt is not None:
                    scores(n + ahead, qt, i)
                lo, hi = bounds[i]
                p = jnp.exp2(st_ref[n % DENSE_SLOTS, lo:hi, :] - m).astype(BF16)
                ai = _dot(vt_ref[g * VT_ROWS:(g + 1) * VT_ROWS, lo:hi], p)
                acc = ai if acc is None else acc + ai
            ot_ref[mixer, h * DH:(h + 1) * DH, :] = acc[:DH] / acc[DH:DH + 1]
            if h == N_HEADS - 1:
                o_ref = oa_ref if mixer == 0 else od_ref
                o_ref[...] = ot_ref[mixer].T.astype(o_ref.dtype)

    @pl.when(qi < n_ctx // TM)
    def _():
        attend(n_ctx)

    @pl.when(qi >= n_ctx // TM)
    def _():
        attend(ka_ref.shape[0])


def _dense_attention(ka, qat, vat, kd, qdt, vdt, n_ctx, q_first):
    b, s, _ = ka.shape
    whole = lambda bi, qi: (bi, 0, 0)
    q_tile = lambda bi, qi: (bi, 0, qi + q_first)
    o_tile = lambda bi, qi: (bi, qi + q_first, 0)
    return pl.pallas_call(
        functools.partial(_dense_attn_kernel, n_ctx=n_ctx, q_first=q_first),
        grid=(b, s // TM - q_first),
        in_specs=[
            pl.BlockSpec((None, s, ka.shape[2]), whole),
            pl.BlockSpec((None, qat.shape[1], TM), q_tile),
            pl.BlockSpec((None, vat.shape[1], s), whole),
            pl.BlockSpec((None, s, kd.shape[2]), whole),
            pl.BlockSpec((None, qdt.shape[1], TM), q_tile),
            pl.BlockSpec((None, vdt.shape[1], s), whole),
        ],
        out_specs=[pl.BlockSpec((None, TM, HQ), o_tile)] * 2,
        out_shape=[jax.ShapeDtypeStruct((b, s, HQ), BF16)] * 2,
        scratch_shapes=[pltpu.VMEM((2, HQ, TM), F32), pltpu.VMEM((DENSE_SLOTS, s, TM), F32)],
        compiler_params=_cparams(2),
        name="attn_dense",
    )(ka, qat, vat, kd, qdt, vdt)


def _pair_attn_kernel(sink_ref, qb_ref, kb_ref, vb_ref, qc_ref, kc_ref, vc_ref, tb_ref, ob_ref, oc_ref, s_ref,
                      *, n_ctx, q_first):
    qi = pl.program_id(1) + q_first
    n_lat = kb_ref.shape[0] - n_ctx
    n_blk = HQ // LANES
    lanes = lambda j: slice(j * LANES, (j + 1) * LANES)

    def ctx_b():
        return kb_ref[0:n_ctx, :], lambda: vb_ref[0:n_ctx, :], None

    def ctx_c(j):
        return kc_ref[0:n_ctx, lanes(j)], lambda: vc_ref[0:n_ctx, lanes(j)], None

    def sinks(j):
        return sink_ref[GQA_HEAD_ORDER[2 * j]], sink_ref[GQA_HEAD_ORDER[2 * j + 1]]

    @pl.when(qi < n_ctx // TM)
    def _():
        def scores(item, slot):
            mixer, j = item
            if mixer == "b":
                return _pair_scores(qb_ref[:, lanes(j)], [ctx_b()], s_ref, slot)
            return _pair_scores(qc_ref[:, lanes(j)], [ctx_c(j)], s_ref, slot)

        def finish(item, handle, slot):
            mixer, j = item
            if mixer == "b":
                ob_ref[:, lanes(j)] = _pair_finish(handle, s_ref, slot, sinks(j)).astype(ob_ref.dtype)
            else:
                oc_ref[:, lanes(j)] = _pair_finish(handle, s_ref, slot).astype(oc_ref.dtype)

        _pipelined([(mixer, j) for mixer in "bc" for j in range(n_blk)], scores, finish)

    @pl.when(qi >= n_ctx // TM)
    def _():
        span_b = TM + 2 * WINDOW
        q0 = (qi - n_ctx // TM) * TM
        start = pl.multiple_of(jnp.clip(q0 - WINDOW, 0, n_lat - span_b), WINDOW)
        dist = (lax.broadcasted_iota(jnp.int32, (2 * TM, span_b), 0)
                - lax.broadcasted_iota(jnp.int32, (2 * TM, span_b), 1) + (q0 - start))
        dist = jnp.where(lax.broadcasted_iota(jnp.int32, (2 * TM, span_b), 0) >= TM, dist - TM, dist)
        bias_b = jnp.where(jnp.abs(dist) <= WINDOW, 0.0, NEG)
        rows_b = pl.ds(n_ctx + start, span_b)
        grid_rows = n_lat // GRID_W
        rows_per_tile = TM // GRID_W
        span_c = KH * GRID_W

        def scores(item, slot):
            if item[0] == "b":
                j = item[1]
                return _pair_scores(qb_ref[:, lanes(j)],
                                    [(kb_ref[rows_b, :], lambda: vb_ref[rows_b, :], bias_b), ctx_b()], s_ref, slot)
            _, rr, j = item
            r = (qi - n_ctx // TM) * rows_per_tile + rr
            r0 = jnp.clip(r - KH // 2, 0, grid_rows - KH)
            delta = r - r0
            krows = pl.ds(pl.multiple_of(n_ctx + r0 * GRID_W, GRID_W), span_c)
            bias = jnp.concatenate(
                [tb_ref[j, 2 * jj - delta + (KH - 1)] for jj in range(KH // 2)], axis=1)
            return _pair_scores(qc_ref[rr * GRID_W:(rr + 1) * GRID_W, lanes(j)],
                                [(kc_ref[krows, lanes(j)], lambda: vc_ref[krows, lanes(j)], bias), ctx_c(j)],
                                s_ref, slot)

        def finish(item, handle, slot):
            if item[0] == "b":
                j = item[1]
                ob_ref[:, lanes(j)] = _pair_finish(handle, s_ref, slot, sinks(j)).astype(ob_ref.dtype)
            else:
                _, rr, j = item
                oc_ref[rr * GRID_W:(rr + 1) * GRID_W, lanes(j)] = (
                    _pair_finish(handle, s_ref, slot).astype(oc_ref.dtype))

        items = [("b", j) for j in range(n_blk)]
        items += [("c", rr, j) for rr in range(rows_per_tile) for j in range(n_blk)]
        _pipelined(items, scores, finish)


def _pair_attention(pb, pc, sink, bias_tab, n_ctx, q_first):
    b, s, _ = pb.shape
    q_tile = lambda bi, qi: (bi, qi + q_first, 0)
    col_blk = lambda c: (lambda bi, qi: (bi, 0, c))
    return pl.pallas_call(
        functools.partial(_pair_attn_kernel, n_ctx=n_ctx, q_first=q_first),
        grid=(b, s // TM - q_first),
        in_specs=[
            pl.BlockSpec(memory_space=pltpu.SMEM),
            pl.BlockSpec((None, TM, HQ), q_tile),
            pl.BlockSpec((None, s, LANES), col_blk(HQ // LANES)),
            pl.BlockSpec((None, s, LANES), col_blk(HQ // LANES + 1)),
            pl.BlockSpec((None, TM, HQ), q_tile),
            pl.BlockSpec((None, s, HQ), col_blk(1)),
            pl.BlockSpec((None, s, HQ), col_blk(2)),
            _resident(bias_tab.shape),
        ],
        out_specs=[pl.BlockSpec((None, TM, HQ), q_tile)] * 2,
        out_shape=[jax.ShapeDtypeStruct((b, s, HQ), BF16)] * 2,
        scratch_shapes=[pltpu.VMEM((PAIR_SLOTS, 2 * TM, max(TM + 2 * WINDOW, KH * GRID_W) + n_ctx), F32)],
        compiler_params=_cparams(2),
        name="attn_pair",
    )(sink, pb, pb, pb, pc, pc, pc, bias_tab)


def _merge_kernel(oa_ref, ob_ref, oc_ref, od_ref, h_ref, mod_ref, gpre_ref, gpost_ref,
                  wg_ref, wbr_ref, wout_ref, o_ref, acc_ref):
    d_model = h_ref.shape[1]
    cw = 256
    h = h_ref[...]
    sh1 = mod_ref[0:1, :]
    sc1 = mod_ref[1:2, :]
    ga1 = mod_ref[2:3, :]
    branches = (oa_ref, ob_ref, oc_ref, od_ref)

    def up_proj(n, c):
        return _dot(branches[n][...], wbr_ref[n, :, c * cw:(c + 1) * cw])

    up0 = [up_proj(n, 0) for n in range(N_BRANCH)]
    u = (_rms(h, gpre_ref[...]) * (1.0 + sc1) + sh1).astype(BF16)
    for c in range(d_model // cw):
        acc = None
        for n in range(N_BRANCH):
            lo = n * d_model + c * cw
            gate = _dot(u, wg_ref[:, lo:lo + cw])
            up = up0[n] if c == 0 else up_proj(n, c)
            term = _sigmoid(gate) * up
            acc = term if acc is None else acc + term
        acc_ref[:, c * cw:(c + 1) * cw] = acc.astype(BF16)
    y = _dot(acc_ref[...], wout_ref[...])
    o_ref[...] = h + ga1 * _rms(y, gpost_ref[...])


def _merge(o_a, o_b, o_c, o_d, h, mod, g_pre, g_post, wg, wbr, wout, tiles_per_batch, latent_only):
    m, d = h.shape
    n, row_of, mod_of = _tile_visits(m, tiles_per_batch, latent_only)
    row = lambda i: (row_of(i), 0)
    return pl.pallas_call(
        _merge_kernel,
        grid=(n,),
        in_specs=[pl.BlockSpec((TM, HQ), row)] * N_BRANCH + [
            pl.BlockSpec((TM, d), row),
            pl.BlockSpec((None, None, MOD_CHUNKS, d), lambda i: (*mod_of(i), 0, 0)),
            _resident((1, d)), _resident((1, d)),
            _resident(wg.shape), _resident(wbr.shape), _resident(wout.shape),
        ],
        out_specs=pl.BlockSpec((TM, d), row),
        out_shape=jax.ShapeDtypeStruct((m, d), F32),
        scratch_shapes=[pltpu.VMEM((TM, d), BF16)],
        input_output_aliases={4: 0},
        compiler_params=_cparams(1),
        name="merge",
    )(o_a, o_b, o_c, o_d, h, mod, g_pre, g_post, wg, wbr, wout)


def _ffn_kernel(hn_ref, hp_ref, modn_ref, modp_ref, gpre_ref, gpost_ref, w1_ref, w2_ref, o_ref,
                u_ref, y_ref, act_ref):
    i = pl.program_id(0)
    d_ff = w2_ref.shape[0]
    cw = 256
    cur = i % 2
    oth = 1 - cur

    def pre_norm(h_ref, mod_ref):
        return _rms(h_ref[...], gpre_ref[...]) * (1.0 + mod_ref[4:5, :]) + mod_ref[3:4, :]

    @pl.when(i == 0)
    def _():
        u_ref[0] = pre_norm(hp_ref, modp_ref).astype(BF16)
        y_ref[1] = jnp.zeros(y_ref.shape[1:], F32)

    out = hp_ref[...] + modp_ref[5:6, :] * _rms(y_ref[oth], gpost_ref[...])
    o_ref[...] = out
    u_next = pre_norm(hn_ref, modn_ref)
    for c in range(d_ff // cw):
        a = _dot(u_ref[cur], w1_ref[:, c * cw:(c + 1) * cw])
        b = _dot(u_ref[cur], w1_ref[:, d_ff + c * cw:d_ff + (c + 1) * cw])
        act_ref[:, c * cw:(c + 1) * cw] = (a * _sigmoid(a) * b).astype(BF16)
    u_ref[oth] = u_next.astype(BF16)
    y = _dot(act_ref[...], w2_ref[...])
    y_ref[cur] = y
    r0, c0 = y.shape[0] - F32_SUBLANES, y.shape[1] - LANES
    y_ref[cur, r0:, c0:] = y[r0:, c0:] + _zero_after(u_next, out)


def _ffn(h, mod, g_pre, g_post, w1, w2, tiles_per_batch, latent_only):
    m, d = h.shape
    n, row_of, mod_of = _tile_visits(m, tiles_per_batch, latent_only)
    p = _StepPipeline(n, row_of, mod_of, d)
    return pl.pallas_call(
        _ffn_kernel,
        grid=(n + 1,),
        in_specs=[p.rows_next, p.rows_prev, p.mod_next, p.mod_prev,
                  _resident((1, d)), _resident((1, d)), _resident(w1.shape), _resident(w2.shape)],
        out_specs=p.rows_out,
        out_shape=jax.ShapeDtypeStruct((n * TM, d), F32),
        scratch_shapes=[pltpu.VMEM((2, TM, d), BF16), pltpu.VMEM((2, TM, d), F32),
                        pltpu.VMEM((TM, w2.shape[0]), BF16)],
        compiler_params=_cparams(1),
        name="ffn",
    )(h, h, mod, mod, g_pre, g_post, w1, w2)


def _rope_tables(n_ctx, n_lat):
    t = jnp.arange(n_lat)
    rows_pos = (t // GRID_W).astype(F32)
    cols_pos = (t % GRID_W).astype(F32)

    def block(d):
        q = d // 4
        inv = ROPE_THETA ** (-jnp.arange(q, dtype=F32) * 2.0 / (d // 2))
        ar = rows_pos[:, None] * inv[None, :]
        ac = cols_pos[:, None] * inv[None, :]
        z = jnp.zeros_like(ar)
        cos = jnp.concatenate([jnp.cos(ar), jnp.cos(ar), jnp.cos(ac), jnp.cos(ac)], axis=1)
        up = jnp.concatenate([-jnp.sin(ar), z, -jnp.sin(ac), z], axis=1)
        dn = jnp.concatenate([z, jnp.sin(ar), z, jnp.sin(ac)], axis=1)
        return cos, up, dn

    def with_ctx(tab, fill):
        return jnp.concatenate([jnp.full((n_ctx, tab.shape[1]), fill, F32), tab], axis=0)

    c64, u64, d64 = block(DH)
    rope_64 = jnp.stack([with_ctx(jnp.tile(c64, (1, LANES // DH)), 1.0),
                         with_ctx(jnp.tile(u64, (1, LANES // DH)), 0.0),
                         with_ctx(jnp.tile(d64, (1, LANES // DH)), 0.0)])
    ca, ua, da = block(ROPE_A)
    pad_lo = NOPE_A
    pad_hi = HEAD_PAD_A - NOPE_A - ROPE_A

    def widen(tab, fill):
        return jnp.concatenate([jnp.full((n_lat, pad_lo), fill, F32), tab,
                                jnp.full((n_lat, pad_hi), fill, F32)], axis=1)

    rope_a = jnp.stack([with_ctx(widen(ca, 1.0), 1.0), with_ctx(widen(ua, 0.0), 0.0),
                        with_ctx(widen(da, 0.0), 0.0)])
    return {"rope_a": rope_a, "rope_64": rope_64}


def _permute_heads(w, axis, order):
    shape = w.shape
    w = w.reshape(shape[:axis] + (N_HEADS, DH) + shape[axis + 1:])
    w = jnp.take(w, jnp.asarray(order), axis=axis)
    return w.reshape(shape)


def _layer_weights(l, w_in, g_a_q, g_a_kv, w_a_uq, w_a_ukv, g_d_q, g_d_k, w_branch):
    d = w_in.shape[1]
    wl = w_in[l]
    o = 0
    a_cq = wl[:, o:o + Q_LORA]; o += Q_LORA
    a_ckv = wl[:, o:o + KV_LORA]; o += KV_LORA
    a_kr = wl[:, o:o + ROPE_A]; o += ROPE_A
    w_bq = wl[:, o:o + HQ]; o += HQ
    w_bkv = wl[:, o:o + 2 * KV_B * DH]; o += 2 * KV_B * DH
    w_c = wl[:, o:o + 3 * HQ]; o += 3 * HQ
    w_d = wl[:, o:o + HQ + 2 * KV_D * DH]; o += HQ + 2 * KV_D * DH
    w_g = wl[:, o:]
    zeros = lambda n: jnp.zeros((d, n), wl.dtype)
    wa = jnp.concatenate([a_cq, a_ckv, zeros(NOPE_A), a_kr, zeros(HEAD_PAD_A - NOPE_A - ROPE_A)], axis=1)
    w_b = jnp.concatenate([_permute_heads(w_bq, 1, GQA_HEAD_ORDER), w_bkv], axis=1)
    uq = w_a_uq[l].reshape(Q_LORA, N_HEADS, NOPE_A + ROPE_A)
    uq = jnp.pad(uq, ((0, 0), (0, 0), (0, HEAD_PAD_A - NOPE_A - ROPE_A))).reshape(Q_LORA, N_HEADS * HEAD_PAD_A)
    ukv = w_a_ukv[l].reshape(KV_LORA, N_HEADS, NOPE_A + DH)
    wk = jnp.pad(ukv[:, :, :NOPE_A], ((0, 0), (0, 0), (0, HEAD_PAD_A - NOPE_A))).reshape(KV_LORA, N_HEADS * HEAD_PAD_A)
    wv = ukv[:, :, NOPE_A:].reshape(KV_LORA, HQ)
    wbr = w_branch[l]
    wbr = jnp.stack([wbr[0], _permute_heads(wbr[1], 0, GQA_HEAD_ORDER), wbr[2], wbr[3]])
    cast = lambda x: x.astype(BF16)
    return {
        "wa": cast(wa), "wb": cast(w_b), "wc": cast(w_c), "wd": cast(w_d), "wg": cast(w_g),
        "wuq": cast(uq), "wk": cast(wk), "wv": cast(wv), "wbr": cast(wbr),
        "gaq": g_a_q[l][None, :], "gakv": g_a_kv[l][None, :],
        "gdq": jnp.tile(g_d_q[l], LANES // DH)[None, :], "gdk": jnp.tile(g_d_k[l], LANES // DH)[None, :],
    }


def kernel(x, c, ctx, c_ctx, w_mod, b_mod, g_pre_mix, g_post_mix, g_pre_ffn, g_post_ffn,
           w_in, g_a_q, g_a_kv, w_a_uq, w_a_ukv, sink_b, rpb_c, g_d_q, g_d_k,
           w_branch, w_out, w_ffn_in, w_ffn_out):
    b, n_lat, d = x.shape
    n_ctx = ctx.shape[1]
    s = n_ctx + n_lat
    depth = w_mod.shape[0]
    assert n_ctx == TM and n_lat % TM == 0 and n_lat >= TM + 2 * WINDOW
    assert n_lat % GRID_W == 0 and n_lat // GRID_W >= KH
    assert KV_B * DH == LANES and KV_D * DH == LANES
    tiles_per_batch = s // TM

    n_rows = -(-(b + 1) // F32_SUBLANES) * F32_SUBLANES
    cc = jnp.zeros((n_rows, d), F32).at[:b].set(c).at[b].set(c_ctx)
    mods = _modulation(cc, w_mod, b_mod)
    mod_x = mods[:, :b].reshape(depth, b, 1, MOD_CHUNKS, d)
    mod_c = jnp.broadcast_to(mods[:, b].reshape(depth, 1, 1, MOD_CHUNKS, d), mod_x.shape)
    mod_tab = jnp.concatenate([mod_c, mod_x], axis=2)

    tabs = _rope_tables(n_ctx, n_lat)
    h = jnp.concatenate([ctx, x], axis=1).reshape(b * s, d)

    for l in range(depth):
        w = _layer_weights(l, w_in, g_a_q, g_a_kv, w_a_uq, w_a_ukv, g_d_q, g_d_k, w_branch)
        mod = mod_tab[l]
        g_pre = g_pre_mix[l][None, :]
        last = l == depth - 1
        q_first = n_ctx // TM if last else 0
        ka, qat, vat, pb, pc, kd, qdt, vdt = _project(h, mod, g_pre, w, tabs, b, tiles_per_batch)
        o_a, o_d = _dense_attention(ka.reshape(b, s, -1), qat, vat, kd.reshape(b, s, -1), qdt, vdt,
                                    n_ctx, q_first)
        o_b, o_c = _pair_attention(pb.reshape(b, s, -1), pc.reshape(b, s, -1), sink_b[l], _nbr_bias(rpb_c[l]),
                                   n_ctx, q_first)
        flat = lambda t: t.reshape(b * s, HQ)
        h = _merge(flat(o_a), flat(o_b), flat(o_c), flat(o_d), h, mod, g_pre, g_post_mix[l][None, :],
                   w["wg"], w["wbr"], w_out[l].astype(BF16), tiles_per_batch, last)
        h = _ffn(h, mod, g_pre_ffn[l][None, :], g_post_ffn[l][None, :],
                 w_ffn_in[l].astype(BF16), w_ffn_out[l].astype(BF16), tiles_per_batch, last)
    return h.reshape(b, n_lat, d)
```

```python
import functools

import jax
import jax.numpy as jnp
from jax import lax
from jax.experimental import pallas as pl
from jax.experimental.pallas import tpu as pltpu

F32 = jnp.float32
BF16 = jnp.bfloat16
UINT = jnp.uint32

GRID_W = 64
ROPE_THETA = 10000.0
EPS = 1e-6
NEG = -1e30
DH = 64
N_HEADS = 8
N_BRANCH = 4
Q_LORA = 256
KV_LORA = 128
NOPE_A = 64
ROPE_A = 32
KV_B = 2
WINDOW = 128
KH = 8
KW = 16
KV_D = 2
MOD_CHUNKS = 6

LANES = 128
F32_SUBLANES = 8
VMEM_LIMIT_BYTES = 56 * 1024 * 1024

TM = 256
HEAD_PAD_A = 128
HQ = N_HEADS * DH
S_A = (NOPE_A + ROPE_A) ** -0.5
S_H = DH ** -0.5
LOG2E = 1.4426950408889634
KEY_CHUNK = 1152
DENSE_SLOTS = 2
PAIR_SLOTS = 3
ONES_ROWS = 16
VT_ROWS = DH + ONES_ROWS
GQA_HEAD_ORDER = tuple(h for j in range(N_HEADS // 2) for h in (j, j + N_HEADS // 2))


def _cparams(n_axes):
    return pltpu.CompilerParams(
        dimension_semantics=("arbitrary",) * n_axes, vmem_limit_bytes=VMEM_LIMIT_BYTES)


def _resident(shape):
    nd = len(shape)
    return pl.BlockSpec(shape, lambda *_: (0,) * nd, pipeline_mode=pl.Buffered(1))


def _tile_visits(m, tiles_per_batch, latent_only):
    lat = tiles_per_batch - 1
    if latent_only:
        return (m // TM // tiles_per_batch * lat,
                lambda j: (j // lat) * tiles_per_batch + 1 + j % lat,
                lambda j: (j // lat, 1))
    return (m // TM, lambda j: j,
            lambda j: (j // tiles_per_batch, jnp.minimum(j % tiles_per_batch, 1)))


class _StepPipeline:
    def __init__(self, n, row_of, mod_of, d):
        nxt = lambda i: jnp.minimum(i + 1, n - 1)
        prv = lambda i: jnp.maximum(i - 1, 0)
        mod_block = (None, None, MOD_CHUNKS, d)
        self.rows_next = pl.BlockSpec((TM, d), lambda i: (row_of(nxt(i)), 0))
        self.rows_prev = pl.BlockSpec((TM, d), lambda i: (row_of(prv(i)), 0))
        self.mod_next = pl.BlockSpec(mod_block, lambda i: (*mod_of(nxt(i)), 0, 0))
        self.mod_prev = pl.BlockSpec(mod_block, lambda i: (*mod_of(prv(i)), 0, 0))
        self.rows_out = pl.BlockSpec((TM, d), lambda i: (prv(i), 0))


def _mod_spec(d, tiles_per_batch):
    return pl.BlockSpec((None, None, MOD_CHUNKS, d),
                        lambda i: (i // tiles_per_batch, jnp.minimum(i % tiles_per_batch, 1), 0, 0))


def _dot(a, b):
    return jnp.dot(a, b, preferred_element_type=F32)


def _dot_nt(a, b):
    return lax.dot_general(a, b, (((1,), (1,)), ((), ())), preferred_element_type=F32)


def _rms(x, g):
    ms = jnp.mean(x * x, axis=-1, keepdims=True)
    return x * lax.rsqrt(ms + EPS) * g


def _sigmoid(x):
    return 1.0 / (1.0 + jnp.exp(-x))


def _zero_after(*tiles):
    r = None
    for t in tiles:
        b = pltpu.bitcast(t, UINT)
        acc = b[:, 0:LANES]
        for c in range(1, b.shape[1] // LANES):
            acc = acc | b[:, c * LANES:(c + 1) * LANES]
        fold = acc[0:F32_SUBLANES]
        for k in range(1, acc.shape[0] // F32_SUBLANES):
            fold = fold | acc[k * F32_SUBLANES:(k + 1) * F32_SUBLANES]
        r = fold if r is None else r | fold
    half = jnp.iinfo(UINT).bits // 2
    z = lax.shift_right_logical(lax.shift_right_logical(r, jnp.array(half, UINT)), jnp.array(half, UINT))
    return pltpu.bitcast(z, F32)


def _rope(x, tab_ref, shift):
    n = x.shape[-1]
    return (x * tab_ref[0] + pltpu.roll(x, n - shift, 1) * tab_ref[1]
            + pltpu.roll(x, shift, 1) * tab_ref[2])


def _head_rms(x, g, bd):
    x2 = x * x
    hi = x2.astype(BF16)
    lo = (x2 - hi.astype(F32)).astype(BF16)
    ss = _dot(hi, bd) + _dot(lo, bd)
    return x * lax.rsqrt(ss * (1.0 / DH) + EPS) * g


def _pair_scores(q_blk, parts, s_ref, slot):
    m = q_blk.shape[0]
    lane = lax.broadcasted_iota(jnp.int32, (m, LANES), 1)
    zero = jnp.zeros_like(q_blk)
    qs = jnp.concatenate([jnp.where(lane < DH, q_blk, zero), jnp.where(lane >= DH, q_blk, zero)], axis=0)
    layout = []
    off = 0
    for k, load_v, bias in parts:
        n = k.shape[0]
        s = _dot_nt(qs, k)
        s_ref[slot, 0:2 * m, off:off + n] = s if bias is None else s + bias
        layout.append((off, n, load_v))
        off += n
    return m, layout


def _pair_finish(handle, s_ref, slot, sinks=None):
    m, layout = handle
    width = sum(n for _, n, _ in layout)
    s = s_ref[slot, 0:2 * m, 0:width]
    mx = jnp.max(s, axis=1, keepdims=True)
    sink = None
    if sinks is not None:
        row = lax.broadcasted_iota(jnp.int32, (2 * m, 1), 0)
        sink = jnp.where(row < m, sinks[0], sinks[1]) * LOG2E
        mx = jnp.maximum(mx, sink)
    p = jnp.exp2(s - mx)
    l = jnp.sum(p, axis=1, keepdims=True)
    if sink is not None:
        l = l + jnp.exp2(sink - mx)
    pb = p.astype(BF16)
    acc = None
    for off, n, load_v in layout:
        a = _dot(pb[:, off:off + n], load_v())
        acc = a if acc is None else acc + a
    o = acc / l
    lane = lax.broadcasted_iota(jnp.int32, (m, LANES), 1)
    return jnp.where(lane < DH, o[:m], o[m:])


def _pipelined(items, scores_fn, finish_fn):
    ahead = PAIR_SLOTS - 1
    handles = [scores_fn(items[n], n % PAIR_SLOTS) for n in range(min(ahead, len(items)))]
    for n, item in enumerate(items):
        if n + ahead < len(items):
            handles.append(scores_fn(items[n + ahead], (n + ahead) % PAIR_SLOTS))
        finish_fn(item, handles[n], n % PAIR_SLOTS)


def _mod_kernel(c_ref, w_ref, b_ref, o_ref):
    c = c_ref[...]
    sc = (c * _sigmoid(c)).astype(BF16)
    o_ref[...] = _dot(sc, w_ref[...].astype(BF16)) + b_ref[...]


def _modulation(cc, w_mod, b_mod):
    n_layers, d, n = w_mod.shape
    r = cc.shape[0]
    tn = n // 4
    return pl.pallas_call(
        _mod_kernel,
        grid=(n_layers, n // tn),
        in_specs=[
            pl.BlockSpec((r, d), lambda l, j: (0, 0)),
            pl.BlockSpec((None, d, tn), lambda l, j: (l, 0, j)),
            pl.BlockSpec((None, 1, tn), lambda l, j: (l, 0, j)),
        ],
        out_specs=pl.BlockSpec((None, r, tn), lambda l, j: (l, 0, j)),
        out_shape=jax.ShapeDtypeStruct((n_layers, r, n), F32),
        compiler_params=_cparams(2),
        name="modulation",
    )(cc, w_mod, b_mod.reshape(n_layers, 1, n))


def _nbr_bias_kernel(rpb_ref, o_ref):
    pair = pl.program_id(0)
    n_dr = 2 * KH - 1
    n_dc = 2 * KW - 1
    qc = lax.broadcasted_iota(jnp.int32, (GRID_W, 2 * GRID_W), 0)
    lane = lax.broadcasted_iota(jnp.int32, (GRID_W, 2 * GRID_W), 1)
    second = lane >= GRID_W
    kc = jnp.where(second, lane - GRID_W, lane)
    c0 = jnp.clip(qc - KW // 2, 0, GRID_W - KW)
    ok = (kc >= c0) & (kc < c0 + KW)
    dc = kc - qc + (KW - 1)

    for e in range(2):
        h = 2 * pair + e

        def d2_body(d2, carry, h=h, e=e):
            def d_body(d, acc):
                v0 = rpb_ref[(h * n_dr + d2) * n_dc + d]
                v1 = rpb_ref[(h * n_dr + d2 + 1) * n_dc + d]
                return jnp.where(dc == d, jnp.where(second, v1, v0), acc)

            acc = lax.fori_loop(0, n_dc, d_body, jnp.zeros((GRID_W, 2 * GRID_W), F32), unroll=True)
            o_ref[d2, e * GRID_W:(e + 1) * GRID_W, :] = jnp.where(ok, acc * LOG2E, NEG)
            return carry

        lax.fori_loop(0, n_dr - 1, d2_body, 0)


def _nbr_bias(rpb):
    h = rpb.shape[0]
    n_pairs = 2 * KH - 2
    return pl.pallas_call(
        _nbr_bias_kernel,
        grid=(h // 2,),
        in_specs=[pl.BlockSpec(memory_space=pltpu.SMEM)],
        out_specs=pl.BlockSpec((None, n_pairs, 2 * GRID_W, 2 * GRID_W), lambda i: (i, 0, 0, 0)),
        out_shape=jax.ShapeDtypeStruct((h // 2, n_pairs, 2 * GRID_W, 2 * GRID_W), F32),
        compiler_params=_cparams(1),
        name="nbr_bias",
    )(rpb.reshape(-1))


def _store_vt(vt_ref, c, v):
    t = v.T.astype(BF16)
    ones = jnp.ones((ONES_ROWS, v.shape[0]), BF16)
    for e in range(LANES // DH):
        base = (2 * c + e) * VT_ROWS
        vt_ref[base:base + DH, :] = t[e * DH:(e + 1) * DH, :]
        vt_ref[base + DH:base + VT_ROWS, :] = ones


def _load_h(refs, tiles_per_batch):
    if len(refs) == 1:
        return refs[0][...]
    x_ref, c_ref = refs
    return jnp.where(pl.program_id(0) % tiles_per_batch == 0, c_ref[...], x_ref[...])


def _proj_kernel(*refs, n_h, tiles_per_batch):
    h = _load_h(refs[:n_h], tiles_per_batch)
    (mod_ref, g_ref, wa_ref, wb_ref, wc_ref, wd_ref, wuq_ref, wk_ref, wv_ref,
     gaq_ref, gakv_ref, gdq_ref, gdk_ref, ropea_ref, rope64_ref,
     ka_ref, qat_ref, vat_ref, pb_ref, pc_ref, kd_ref, qdt_ref, vdt_ref) = refs[n_h:]
    sh1 = mod_ref[0:1, :]
    sc1 = mod_ref[1:2, :]
    u = (_rms(h, g_ref[...]) * (1.0 + sc1) + sh1).astype(BF16)

    rb = jnp.where(lax.broadcasted_iota(jnp.int32, (LANES, LANES), 0) < DH, 0, 1)
    cb = jnp.where(lax.broadcasted_iota(jnp.int32, (LANES, LANES), 1) < DH, 0, 1)
    bd = jnp.where(rb == cb, 1.0, 0.0).astype(BF16)

    ya = _dot(u, wa_ref[...])
    yd = _dot(u, wd_ref[...])
    cq = _rms(ya[:, :Q_LORA], gaq_ref[...]).astype(BF16)
    ckv = _rms(ya[:, Q_LORA:Q_LORA + KV_LORA], gakv_ref[...]).astype(BF16)
    kr = _rope(ya[:, Q_LORA + KV_LORA:], ropea_ref, ROPE_A // 4)
    yb = _dot(u, wb_ref[...])
    qa = _dot(cq, wuq_ref[...])
    ka = _dot(ckv, wk_ref[...])
    va = _dot(ckv, wv_ref[...])

    for c in range(HQ // LANES):
        lo, hi = c * LANES, (c + 1) * LANES
        qn = _head_rms(yd[:, lo:hi], gdq_ref[...], bd)
        qdt_ref[lo:hi, :] = (_rope(qn, rope64_ref, DH // 4) * (S_H * LOG2E)).T.astype(BF16)
    kn = _head_rms(yd[:, HQ:HQ + LANES], gdk_ref[...], bd)
    kd_ref[...] = _rope(kn, rope64_ref, DH // 4).astype(BF16)
    _store_vt(vdt_ref, 0, yd[:, HQ + LANES:])

    yc = _dot(u, wc_ref[...])

    for c in range(HQ // LANES):
        lo, hi = c * LANES, (c + 1) * LANES
        pb_ref[:, lo:hi] = (_rope(yb[:, lo:hi], rope64_ref, DH // 4) * (S_H * LOG2E)).astype(BF16)
    pb_ref[:, HQ:HQ + LANES] = _rope(yb[:, HQ:HQ + LANES], rope64_ref, DH // 4).astype(BF16)
    pb_ref[:, HQ + LANES:] = yb[:, HQ + LANES:].astype(BF16)

    for hh in range(N_HEADS):
        lo, hi = hh * HEAD_PAD_A, (hh + 1) * HEAD_PAD_A
        q = _rope(qa[:, lo:hi], ropea_ref, ROPE_A // 4) * (S_A * LOG2E)
        qat_ref[lo:hi, :] = q.T.astype(BF16)
        ka_ref[:, lo:hi] = (ka[:, lo:hi] + kr).astype(BF16)
    for c in range(HQ // LANES):
        lo, hi = c * LANES, (c + 1) * LANES
        _store_vt(vat_ref, c, va[:, lo:hi])

    pc_ref[:, :HQ] = (yc[:, :HQ] * (S_H * LOG2E)).astype(BF16)
    pc_ref[:, HQ:] = yc[:, HQ:].astype(BF16)


def _h_specs(h, d, tiles_per_batch):
    if not isinstance(h, tuple):
        return [pl.BlockSpec((TM, d), lambda i: (i, 0))]
    lat = tiles_per_batch - 1
    return [pl.BlockSpec((TM, d), lambda i: ((i // tiles_per_batch) * lat
                                             + jnp.maximum(i % tiles_per_batch - 1, 0), 0)),
            pl.BlockSpec((TM, d), lambda i: (i // tiles_per_batch, 0))]


def _project(h, mod, g_pre, w, tabs, n_batch, tiles_per_batch):
    hs = h if isinstance(h, tuple) else (h,)
    d = hs[0].shape[1]
    m = n_batch * tiles_per_batch * TM
    s = tiles_per_batch * TM
    wq_a = N_HEADS * HEAD_PAD_A
    row = lambda i: (i, 0)
    col = lambda i: (i // tiles_per_batch, 0, i % tiles_per_batch)
    tile_pos = lambda i: (0, i % tiles_per_batch, 0)
    in_specs = _h_specs(h, d, tiles_per_batch) + [
        _mod_spec(d, tiles_per_batch),
        _resident((1, d)),
        _resident(w["wa"].shape), _resident(w["wb"].shape), _resident(w["wc"].shape),
        _resident(w["wd"].shape), _resident(w["wuq"].shape),
        _resident(w["wk"].shape), _resident(w["wv"].shape),
        _resident((1, Q_LORA)), _resident((1, KV_LORA)), _resident((1, LANES)), _resident((1, LANES)),
        pl.BlockSpec((3, TM, LANES), tile_pos),
        pl.BlockSpec((3, TM, LANES), tile_pos),
    ]

    def rows_out(c):
        return pl.BlockSpec((TM, c), row), jax.ShapeDtypeStruct((m, c), BF16)

    def cols_out(r):
        return pl.BlockSpec((None, r, TM), col), jax.ShapeDtypeStruct((n_batch, r, s), BF16)

    outs = [rows_out(wq_a), cols_out(wq_a), cols_out(N_HEADS * VT_ROWS),
            rows_out(HQ + 2 * LANES), rows_out(3 * HQ),
            rows_out(LANES), cols_out(HQ), cols_out(KV_D * VT_ROWS)]
    return pl.pallas_call(
        functools.partial(_proj_kernel, n_h=len(hs), tiles_per_batch=tiles_per_batch),
        grid=(m // TM,),
        in_specs=in_specs,
        out_specs=[o[0] for o in outs],
        out_shape=[o[1] for o in outs],
        compiler_params=_cparams(1),
        name="proj",
    )(*hs, mod, g_pre, w["wa"], w["wb"], w["wc"], w["wd"], w["wuq"], w["wk"], w["wv"],
      w["gaq"], w["gakv"], w["gdq"], w["gdk"], tabs["rope_a"], tabs["rope_64"])


def _dense_attn_kernel(ka_ref, qat_ref, vat_ref, kd_ref, qdt_ref, vdt_ref, oa_ref, od_ref, ot_ref, st_ref,
                       *, n_ctx, q_first):
    qi = pl.program_id(1) + q_first
    group_d = N_HEADS // (LANES // DH)
    items = [(mixer, h) for mixer in range(2) for h in range(N_HEADS)]

    def attend(n_keys):
        n_chunks = -(-n_keys // KEY_CHUNK)
        bounds = [(i * n_keys // n_chunks, (i + 1) * n_keys // n_chunks) for i in range(n_chunks)]

        def q_operand(item):
            mixer, h = item
            if mixer == 0:
                return qat_ref[h * LANES:(h + 1) * LANES, :]
            qh = qdt_ref[h * DH:(h + 1) * DH, :]
            z = jnp.zeros_like(qh)
            return jnp.concatenate([qh, z] if h // group_d == 0 else [z, qh], axis=0)

        def scores(n, qt, i):
            mixer, h = items[n]
            lo, hi = bounds[i]
            k = ka_ref[lo:hi, h * LANES:(h + 1) * LANES] if mixer == 0 else kd_ref[lo:hi, :]
            st_ref[n % DENSE_SLOTS, lo:hi, :] = _dot(k, qt)

        def colmax(n):
            m = None
            for lo, hi in bounds:
                mi = jnp.max(st_ref[n % DENSE_SLOTS, lo:hi, :], axis=0, keepdims=True)
                m = mi if m is None else jnp.maximum(m, mi)
            return m

        ahead = DENSE_SLOTS - 1
        for n0 in range(ahead):
            qt = q_operand(items[n0])
            for i in range(n_chunks):
                scores(n0, qt, i)
        for n, (mixer, h) in enumerate(items):
            vt_ref, g = (vat_ref, h) if mixer == 0 else (vdt_ref, h // group_d)
            m = colmax(n)
            qt = q_operand(items[n + ahead]) if n + ahead < len(items) else None
            acc = None
            for i in range(n_chunks):
                if qt is not None:
                    scores(n + ahead, qt, i)
                lo, hi = bounds[i]
                p = jnp.exp2(st_ref[n % DENSE_SLOTS, lo:hi, :] - m).astype(BF16)
                ai = _dot(vt_ref[g * VT_ROWS:(g + 1) * VT_ROWS, lo:hi], p)
                acc = ai if acc is None else acc + ai
            ot_ref[mixer, h * DH:(h + 1) * DH, :] = acc[:DH] / acc[DH:DH + 1]
            if h == N_HEADS - 1:
                o_ref = oa_ref if mixer == 0 else od_ref
                o_ref[...] = ot_ref[mixer].T.astype(o_ref.dtype)

    @pl.when(qi < n_ctx // TM)
    def _():
        attend(n_ctx)

    @pl.when(qi >= n_ctx // TM)
    def _():
        attend(ka_ref.shape[0])


def _dense_attention(ka, qat, vat, kd, qdt, vdt, n_ctx, q_first):
    b, s, _ = ka.shape
    whole = lambda bi, qi: (bi, 0, 0)
    q_tile = lambda bi, qi: (bi, 0, qi + q_first)
    o_tile = lambda bi, qi: (bi, qi + q_first, 0)
    return pl.pallas_call(
        functools.partial(_dense_attn_kernel, n_ctx=n_ctx, q_first=q_first),
        grid=(b, s // TM - q_first),
        in_specs=[
            pl.BlockSpec((None, s, ka.shape[2]), whole),
            pl.BlockSpec((None, qat.shape[1], TM), q_tile),
            pl.BlockSpec((None, vat.shape[1], s), whole),
            pl.BlockSpec((None, s, kd.shape[2]), whole),
            pl.BlockSpec((None, qdt.shape[1], TM), q_tile),
            pl.BlockSpec((None, vdt.shape[1], s), whole),
        ],
        out_specs=[pl.BlockSpec((None, TM, HQ), o_tile)] * 2,
        out_shape=[jax.ShapeDtypeStruct((b, s, HQ), BF16)] * 2,
        scratch_shapes=[pltpu.VMEM((2, HQ, TM), F32), pltpu.VMEM((DENSE_SLOTS, s, TM), F32)],
        compiler_params=_cparams(2),
        name="attn_dense",
    )(ka, qat, vat, kd, qdt, vdt)


def _pair_attn_kernel(sink_ref, qb_ref, kb_ref, vb_ref, qc_ref, kc_ref, vc_ref, tb_ref, ob_ref, oc_ref, s_ref,
                      *, n_ctx, q_first):
    qi = pl.program_id(1) + q_first
    n_lat = kb_ref.shape[0] - n_ctx
    n_blk = HQ // LANES
    lanes = lambda j: slice(j * LANES, (j + 1) * LANES)

    def ctx_b():
        return kb_ref[0:n_ctx, :], lambda: vb_ref[0:n_ctx, :], None

    def ctx_c(j):
        return kc_ref[0:n_ctx, lanes(j)], lambda: vc_ref[0:n_ctx, lanes(j)], None

    def sinks(j):
        return sink_ref[GQA_HEAD_ORDER[2 * j]], sink_ref[GQA_HEAD_ORDER[2 * j + 1]]

    @pl.when(qi < n_ctx // TM)
    def _():
        def scores(item, slot):
            mixer, j = item
            if mixer == "b":
                return _pair_scores(qb_ref[:, lanes(j)], [ctx_b()], s_ref, slot)
            return _pair_scores(qc_ref[:, lanes(j)], [ctx_c(j)], s_ref, slot)

        def finish(item, handle, slot):
            mixer, j = item
            if mixer == "b":
                ob_ref[:, lanes(j)] = _pair_finish(handle, s_ref, slot, sinks(j)).astype(ob_ref.dtype)
            else:
                oc_ref[:, lanes(j)] = _pair_finish(handle, s_ref, slot).astype(oc_ref.dtype)

        _pipelined([(mixer, j) for mixer in "bc" for j in range(n_blk)], scores, finish)

    @pl.when(qi >= n_ctx // TM)
    def _():
        span_b = TM + 2 * WINDOW
        q0 = (qi - n_ctx // TM) * TM
        start = pl.multiple_of(jnp.clip(q0 - WINDOW, 0, n_lat - span_b), WINDOW)
        dist = (lax.broadcasted_iota(jnp.int32, (2 * TM, span_b), 0)
                - lax.broadcasted_iota(jnp.int32, (2 * TM, span_b), 1) + (q0 - start))
        dist = jnp.where(lax.broadcasted_iota(jnp.int32, (2 * TM, span_b), 0) >= TM, dist - TM, dist)
        bias_b = jnp.where(jnp.abs(dist) <= WINDOW, 0.0, NEG)
        rows_b = pl.ds(n_ctx + start, span_b)
        grid_rows = n_lat // GRID_W
        rows_per_tile = TM // GRID_W
        span_c = KH * GRID_W

        def scores(item, slot):
            if item[0] == "b":
                j = item[1]
                return _pair_scores(qb_ref[:, lanes(j)],
                                    [(kb_ref[rows_b, :], lambda: vb_ref[rows_b, :], bias_b), ctx_b()], s_ref, slot)
            _, rr, j = item
            r = (qi - n_ctx // TM) * rows_per_tile + rr
            r0 = jnp.clip(r - KH // 2, 0, grid_rows - KH)
            delta = r - r0
            krows = pl.ds(pl.multiple_of(n_ctx + r0 * GRID_W, GRID_W), span_c)
            bias = jnp.concatenate(
                [tb_ref[j, 2 * jj - delta + (KH - 1)] for jj in range(KH // 2)], axis=1)
            return _pair_scores(qc_ref[rr * GRID_W:(rr + 1) * GRID_W, lanes(j)],
                                [(kc_ref[krows, lanes(j)], lambda: vc_ref[krows, lanes(j)], bias), ctx_c(j)],
                                s_ref, slot)

        def finish(item, handle, slot):
            if item[0] == "b":
                j = item[1]
                ob_ref[:, lanes(j)] = _pair_finish(handle, s_ref, slot, sinks(j)).astype(ob_ref.dtype)
            else:
                _, rr, j = item
                oc_ref[rr * GRID_W:(rr + 1) * GRID_W, lanes(j)] = (
                    _pair_finish(handle, s_ref, slot).astype(oc_ref.dtype))

        items = [("b", j) for j in range(n_blk)]
        items += [("c", rr, j) for rr in range(rows_per_tile) for j in range(n_blk)]
        _pipelined(items, scores, finish)


def _pair_attention(pb, pc, sink, bias_tab, n_ctx, q_first):
    b, s, _ = pb.shape
    q_tile = lambda bi, qi: (bi, qi + q_first, 0)
    col_blk = lambda c: (lambda bi, qi: (bi, 0, c))
    return pl.pallas_call(
        functools.partial(_pair_attn_kernel, n_ctx=n_ctx, q_first=q_first),
        grid=(b, s // TM - q_first),
        in_specs=[
            pl.BlockSpec(memory_space=pltpu.SMEM),
            pl.BlockSpec((None, TM, HQ), q_tile),
            pl.BlockSpec((None, s, LANES), col_blk(HQ // LANES)),
            pl.BlockSpec((None, s, LANES), col_blk(HQ // LANES + 1)),
            pl.BlockSpec((None, TM, HQ), q_tile),
            pl.BlockSpec((None, s, HQ), col_blk(1)),
            pl.BlockSpec((None, s, HQ), col_blk(2)),
            _resident(bias_tab.shape),
        ],
        out_specs=[pl.BlockSpec((None, TM, HQ), q_tile)] * 2,
        out_shape=[jax.ShapeDtypeStruct((b, s, HQ), BF16)] * 2,
        scratch_shapes=[pltpu.VMEM((PAIR_SLOTS, 2 * TM, max(TM + 2 * WINDOW, KH * GRID_W) + n_ctx), F32)],
        compiler_params=_cparams(2),
        name="attn_pair",
    )(sink, pb, pb, pb, pc, pc, pc, bias_tab)


def _merge_kernel(*refs, n_h, tiles_per_batch):
    h = _load_h(refs[:n_h], tiles_per_batch)
    (oa_ref, ob_ref, oc_ref, od_ref, mod_ref, gpre_ref, gpost_ref,
     wg_ref, wbr_ref, wout_ref, o_ref, acc_ref) = refs[n_h:]
    d_model = h.shape[1]
    cw = 256
    sh1 = mod_ref[0:1, :]
    sc1 = mod_ref[1:2, :]
    ga1 = mod_ref[2:3, :]
    branches = (oa_ref, ob_ref, oc_ref, od_ref)

    def up_proj(n, c):
        return _dot(branches[n][...], wbr_ref[n, :, c * cw:(c + 1) * cw])

    up0 = [up_proj(n, 0) for n in range(N_BRANCH)]
    u = (_rms(h, gpre_ref[...]) * (1.0 + sc1) + sh1).astype(BF16)
    for c in range(d_model // cw):
        acc = None
        for n in range(N_BRANCH):
            lo = n * d_model + c * cw
            gate = _dot(u, wg_ref[:, lo:lo + cw])
            up = up0[n] if c == 0 else up_proj(n, c)
            term = _sigmoid(gate) * up
            acc = term if acc is None else acc + term
        acc_ref[:, c * cw:(c + 1) * cw] = acc.astype(BF16)
    y = _dot(acc_ref[...], wout_ref[...])
    o_ref[...] = h + ga1 * _rms(y, gpost_ref[...])


def _merge(o_a, o_b, o_c, o_d, h, mod, g_pre, g_post, wg, wbr, wout, tiles_per_batch, latent_only):
    split = isinstance(h, tuple)
    hs = h if split else (h,)
    m, d = o_a.shape[0], hs[0].shape[1]
    n, row_of, mod_of = _tile_visits(m, tiles_per_batch, latent_only)
    assert not (split and latent_only)
    row = lambda i: (row_of(i), 0)
    h_specs = _h_specs(h, d, tiles_per_batch) if split else [pl.BlockSpec((TM, d), row)]
    return pl.pallas_call(
        functools.partial(_merge_kernel, n_h=len(hs), tiles_per_batch=tiles_per_batch),
        grid=(n,),
        in_specs=h_specs + [pl.BlockSpec((TM, HQ), row)] * N_BRANCH + [
            pl.BlockSpec((None, None, MOD_CHUNKS, d), lambda i: (*mod_of(i), 0, 0)),
            _resident((1, d)), _resident((1, d)),
            _resident(wg.shape), _resident(wbr.shape), _resident(wout.shape),
        ],
        out_specs=pl.BlockSpec((TM, d), row),
        out_shape=jax.ShapeDtypeStruct((m, d), F32),
        scratch_shapes=[pltpu.VMEM((TM, d), BF16)],
        input_output_aliases={} if split else {0: 0},
        compiler_params=_cparams(1),
        name="merge",
    )(*hs, o_a, o_b, o_c, o_d, mod, g_pre, g_post, wg, wbr, wout)


def _ffn_kernel(hn_ref, hp_ref, modn_ref, modp_ref, gpre_ref, gpost_ref, w1_ref, w2_ref, o_ref,
                u_ref, y_ref, act_ref):
    i = pl.program_id(0)
    d_ff = w2_ref.shape[0]
    cw = 256
    cur = i % 2
    oth = 1 - cur

    def pre_norm(h_ref, mod_ref):
        return _rms(h_ref[...], gpre_ref[...]) * (1.0 + mod_ref[4:5, :]) + mod_ref[3:4, :]

    @pl.when(i == 0)
    def _():
        u_ref[0] = pre_norm(hp_ref, modp_ref).astype(BF16)
        y_ref[1] = jnp.zeros(y_ref.shape[1:], F32)

    out = hp_ref[...] + modp_ref[5:6, :] * _rms(y_ref[oth], gpost_ref[...])
    o_ref[...] = out
    u_next = pre_norm(hn_ref, modn_ref)
    for c in range(d_ff // cw):
        a = _dot(u_ref[cur], w1_ref[:, c * cw:(c + 1) * cw])
        b = _dot(u_ref[cur], w1_ref[:, d_ff + c * cw:d_ff + (c + 1) * cw])
        act_ref[:, c * cw:(c + 1) * cw] = (a * _sigmoid(a) * b).astype(BF16)
    u_ref[oth] = u_next.astype(BF16)
    y = _dot(act_ref[...], w2_ref[...])
    y_ref[cur] = y
    r0, c0 = y.shape[0] - F32_SUBLANES, y.shape[1] - LANES
    y_ref[cur, r0:, c0:] = y[r0:, c0:] + _zero_after(u_next, out)


def _ffn(h, mod, g_pre, g_post, w1, w2, tiles_per_batch, latent_only):
    m, d = h.shape
    n, row_of, mod_of = _tile_visits(m, tiles_per_batch, latent_only)
    p = _StepPipeline(n, row_of, mod_of, d)
    return pl.pallas_call(
        _ffn_kernel,
        grid=(n + 1,),
        in_specs=[p.rows_next, p.rows_prev, p.mod_next, p.mod_prev,
                  _resident((1, d)), _resident((1, d)), _resident(w1.shape), _resident(w2.shape)],
        out_specs=p.rows_out,
        out_shape=jax.ShapeDtypeStruct((n * TM, d), F32),
        scratch_shapes=[pltpu.VMEM((2, TM, d), BF16), pltpu.VMEM((2, TM, d), F32),
                        pltpu.VMEM((TM, w2.shape[0]), BF16)],
        compiler_params=_cparams(1),
        name="ffn",
    )(h, h, mod, mod, g_pre, g_post, w1, w2)


def _rope_tables(n_ctx, n_lat):
    t = jnp.arange(n_lat)
    rows_pos = (t // GRID_W).astype(F32)
    cols_pos = (t % GRID_W).astype(F32)

    def block(d):
        q = d // 4
        inv = ROPE_THETA ** (-jnp.arange(q, dtype=F32) * 2.0 / (d // 2))
        ar = rows_pos[:, None] * inv[None, :]
        ac = cols_pos[:, None] * inv[None, :]
        z = jnp.zeros_like(ar)
        cos = jnp.concatenate([jnp.cos(ar), jnp.cos(ar), jnp.cos(ac), jnp.cos(ac)], axis=1)
        up = jnp.concatenate([-jnp.sin(ar), z, -jnp.sin(ac), z], axis=1)
        dn = jnp.concatenate([z, jnp.sin(ar), z, jnp.sin(ac)], axis=1)
        return cos, up, dn

    def with_ctx(tab, fill):
        return jnp.concatenate([jnp.full((n_ctx, tab.shape[1]), fill, F32), tab], axis=0)

    c64, u64, d64 = block(DH)
    rope_64 = jnp.stack([with_ctx(jnp.tile(c64, (1, LANES // DH)), 1.0),
                         with_ctx(jnp.tile(u64, (1, LANES // DH)), 0.0),
                         with_ctx(jnp.tile(d64, (1, LANES // DH)), 0.0)])
    ca, ua, da = block(ROPE_A)
    pad_lo = NOPE_A
    pad_hi = HEAD_PAD_A - NOPE_A - ROPE_A

    def widen(tab, fill):
        return jnp.concatenate([jnp.full((n_lat, pad_lo), fill, F32), tab,
                                jnp.full((n_lat, pad_hi), fill, F32)], axis=1)

    rope_a = jnp.stack([with_ctx(widen(ca, 1.0), 1.0), with_ctx(widen(ua, 0.0), 0.0),
                        with_ctx(widen(da, 0.0), 0.0)])
    return {"rope_a": rope_a, "rope_64": rope_64}


def _permute_heads(w, axis, order):
    shape = w.shape
    w = w.reshape(shape[:axis] + (N_HEADS, DH) + shape[axis + 1:])
    w = jnp.take(w, jnp.asarray(order), axis=axis)
    return w.reshape(shape)


def _layer_weights(l, w_in, g_a_q, g_a_kv, w_a_uq, w_a_ukv, g_d_q, g_d_k, w_branch):
    d = w_in.shape[1]
    wl = w_in[l]
    o = 0
    a_cq = wl[:, o:o + Q_LORA]; o += Q_LORA
    a_ckv = wl[:, o:o + KV_LORA]; o += KV_LORA
    a_kr = wl[:, o:o + ROPE_A]; o += ROPE_A
    w_bq = wl[:, o:o + HQ]; o += HQ
    w_bkv = wl[:, o:o + 2 * KV_B * DH]; o += 2 * KV_B * DH
    w_c = wl[:, o:o + 3 * HQ]; o += 3 * HQ
    w_d = wl[:, o:o + HQ + 2 * KV_D * DH]; o += HQ + 2 * KV_D * DH
    w_g = wl[:, o:]
    zeros = lambda n: jnp.zeros((d, n), wl.dtype)
    wa = jnp.concatenate([a_cq, a_ckv, zeros(NOPE_A), a_kr, zeros(HEAD_PAD_A - NOPE_A - ROPE_A)], axis=1)
    w_b = jnp.concatenate([_permute_heads(w_bq, 1, GQA_HEAD_ORDER), w_bkv], axis=1)
    uq = w_a_uq[l].reshape(Q_LORA, N_HEADS, NOPE_A + ROPE_A)
    uq = jnp.pad(uq, ((0, 0), (0, 0), (0, HEAD_PAD_A - NOPE_A - ROPE_A))).reshape(Q_LORA, N_HEADS * HEAD_PAD_A)
    ukv = w_a_ukv[l].reshape(KV_LORA, N_HEADS, NOPE_A + DH)
    wk = jnp.pad(ukv[:, :, :NOPE_A], ((0, 0), (0, 0), (0, HEAD_PAD_A - NOPE_A))).reshape(KV_LORA, N_HEADS * HEAD_PAD_A)
    wv = ukv[:, :, NOPE_A:].reshape(KV_LORA, HQ)
    wbr = w_branch[l]
    wbr = jnp.stack([wbr[0], _permute_heads(wbr[1], 0, GQA_HEAD_ORDER), wbr[2], wbr[3]])
    cast = lambda x: x.astype(BF16)
    return {
        "wa": cast(wa), "wb": cast(w_b), "wc": cast(w_c), "wd": cast(w_d), "wg": cast(w_g),
        "wuq": cast(uq), "wk": cast(wk), "wv": cast(wv), "wbr": cast(wbr),
        "gaq": g_a_q[l][None, :], "gakv": g_a_kv[l][None, :],
        "gdq": jnp.tile(g_d_q[l], LANES // DH)[None, :], "gdk": jnp.tile(g_d_k[l], LANES // DH)[None, :],
    }


def kernel(x, c, ctx, c_ctx, w_mod, b_mod, g_pre_mix, g_post_mix, g_pre_ffn, g_post_ffn,
           w_in, g_a_q, g_a_kv, w_a_uq, w_a_ukv, sink_b, rpb_c, g_d_q, g_d_k,
           w_branch, w_out, w_ffn_in, w_ffn_out):
    b, n_lat, d = x.shape
    n_ctx = ctx.shape[1]
    s = n_ctx + n_lat
    depth = w_mod.shape[0]
    assert n_ctx == TM and n_lat % TM == 0 and n_lat >= TM + 2 * WINDOW
    assert n_lat % GRID_W == 0 and n_lat // GRID_W >= KH
    assert KV_B * DH == LANES and KV_D * DH == LANES
    tiles_per_batch = s // TM

    n_rows = -(-(b + 1) // F32_SUBLANES) * F32_SUBLANES
    cc = jnp.zeros((n_rows, d), F32).at[:b].set(c).at[b].set(c_ctx)
    mods = _modulation(cc, w_mod, b_mod)
    mod_x = mods[:, :b].reshape(depth, b, 1, MOD_CHUNKS, d)
    mod_c = jnp.broadcast_to(mods[:, b].reshape(depth, 1, 1, MOD_CHUNKS, d), mod_x.shape)
    mod_tab = jnp.concatenate([mod_c, mod_x], axis=2)

    tabs = _rope_tables(n_ctx, n_lat)
    h = (x.reshape(b * n_lat, d), ctx.reshape(b * n_ctx, d))
    if depth == 1:
        h = jnp.concatenate([ctx, x], axis=1).reshape(b * s, d)

    for l in range(depth):
        w = _layer_weights(l, w_in, g_a_q, g_a_kv, w_a_uq, w_a_ukv, g_d_q, g_d_k, w_branch)
        mod = mod_tab[l]
        g_pre = g_pre_mix[l][None, :]
        last = l == depth - 1
        q_first = n_ctx // TM if last else 0
        ka, qat, vat, pb, pc, kd, qdt, vdt = _project(h, mod, g_pre, w, tabs, b, tiles_per_batch)
        o_a, o_d = _dense_attention(ka.reshape(b, s, -1), qat, vat, kd.reshape(b, s, -1), qdt, vdt,
                                    n_ctx, q_first)
        o_b, o_c = _pair_attention(pb.reshape(b, s, -1), pc.reshape(b, s, -1), sink_b[l], _nbr_bias(rpb_c[l]),
                                   n_ctx, q_first)
        flat = lambda t: t.reshape(b * s, HQ)
        h = _merge(flat(o_a), flat(o_b), flat(o_c), flat(o_d), h, mod, g_pre, g_post_mix[l][None, :],
                   w["wg"], w["wbr"], w_out[l].astype(BF16), tiles_per_batch, last)
        h = _ffn(h, mod, g_pre_ffn[l][None, :], g_post_ffn[l][None, :],
                 w_ffn_in[l].astype(BF16), w_ffn_out[l].astype(BF16), tiles_per_batch, last)
    return h.reshape(b, n_lat, d)
```

```python
import functools

import jax
import jax.numpy as jnp
from jax import lax
from jax.experimental import pallas as pl
from jax.experimental.pallas import tpu as pltpu

F32 = jnp.float32
BF16 = jnp.bfloat16
UINT = jnp.uint32

GRID_W = 64
ROPE_THETA = 10000.0
EPS = 1e-6
NEG = -1e30
DH = 64
N_HEADS = 8
N_BRANCH = 4
Q_LORA = 256
KV_LORA = 128
NOPE_A = 64
ROPE_A = 32
KV_B = 2
WINDOW = 128
KH = 8
KW = 16
KV_D = 2
MOD_CHUNKS = 6

LANES = 128
F32_SUBLANES = 8
VMEM_LIMIT_BYTES = 56 * 1024 * 1024

TM = 256
HEAD_PAD_A = 128
HQ = N_HEADS * DH
S_A = (NOPE_A + ROPE_A) ** -0.5
S_H = DH ** -0.5
LOG2E = 1.4426950408889634
KEY_CHUNK = 1152
DENSE_SLOTS = 3
PAIR_SLOTS = 3
ONES_ROWS = 16
VT_ROWS = DH + ONES_ROWS
GQA_HEAD_ORDER = tuple(h for j in range(N_HEADS // 2) for h in (j, j + N_HEADS // 2))


def _cparams(n_axes):
    return pltpu.CompilerParams(
        dimension_semantics=("arbitrary",) * n_axes, vmem_limit_bytes=VMEM_LIMIT_BYTES)


def _resident(shape):
    nd = len(shape)
    return pl.BlockSpec(shape, lambda *_: (0,) * nd, pipeline_mode=pl.Buffered(1))


def _tile_visits(m, tiles_per_batch, latent_only):
    lat = tiles_per_batch - 1
    if latent_only:
        return (m // TM // tiles_per_batch * lat,
                lambda j: (j // lat) * tiles_per_batch + 1 + j % lat,
                lambda j: (j // lat, 1))
    return (m // TM, lambda j: j,
            lambda j: (j // tiles_per_batch, jnp.minimum(j % tiles_per_batch, 1)))


class _StepPipeline:
    def __init__(self, n, row_of, mod_of, d):
        nxt = lambda i: jnp.minimum(i + 1, n - 1)
        prv = lambda i: jnp.maximum(i - 1, 0)
        mod_block = (None, None, MOD_CHUNKS, d)
        self.rows_next = pl.BlockSpec((TM, d), lambda i: (row_of(nxt(i)), 0))
        self.rows_prev = pl.BlockSpec((TM, d), lambda i: (row_of(prv(i)), 0))
        self.mod_next = pl.BlockSpec(mod_block, lambda i: (*mod_of(nxt(i)), 0, 0))
        self.mod_prev = pl.BlockSpec(mod_block, lambda i: (*mod_of(prv(i)), 0, 0))
        self.rows_out = pl.BlockSpec((TM, d), lambda i: (prv(i), 0))


def _mod_spec(d, tiles_per_batch):
    return pl.BlockSpec((None, None, MOD_CHUNKS, d),
                        lambda i: (i // tiles_per_batch, jnp.minimum(i % tiles_per_batch, 1), 0, 0))


def _dot(a, b):
    return jnp.dot(a, b, preferred_element_type=F32)


def _dot_nt(a, b):
    return lax.dot_general(a, b, (((1,), (1,)), ((), ())), preferred_element_type=F32)


def _rms(x, g):
    ms = jnp.mean(x * x, axis=-1, keepdims=True)
    return x * lax.rsqrt(ms + EPS) * g


def _sigmoid(x):
    return 1.0 / (1.0 + jnp.exp(-x))


def _zero_after(*tiles):
    r = None
    for t in tiles:
        b = pltpu.bitcast(t, UINT)
        acc = b[:, 0:LANES]
        for c in range(1, b.shape[1] // LANES):
            acc = acc | b[:, c * LANES:(c + 1) * LANES]
        fold = acc[0:F32_SUBLANES]
        for k in range(1, acc.shape[0] // F32_SUBLANES):
            fold = fold | acc[k * F32_SUBLANES:(k + 1) * F32_SUBLANES]
        r = fold if r is None else r | fold
    half = jnp.iinfo(UINT).bits // 2
    z = lax.shift_right_logical(lax.shift_right_logical(r, jnp.array(half, UINT)), jnp.array(half, UINT))
    return pltpu.bitcast(z, F32)


def _rope(x, tab_ref, shift):
    n = x.shape[-1]
    return (x * tab_ref[0] + pltpu.roll(x, n - shift, 1) * tab_ref[1]
            + pltpu.roll(x, shift, 1) * tab_ref[2])


def _head_rms(x, g, bd):
    x2 = x * x
    hi = x2.astype(BF16)
    lo = (x2 - hi.astype(F32)).astype(BF16)
    ss = _dot(hi, bd) + _dot(lo, bd)
    return x * lax.rsqrt(ss * (1.0 / DH) + EPS) * g


def _pair_scores(q_blk, parts, s_ref, slot):
    m = q_blk.shape[0]
    lane = lax.broadcasted_iota(jnp.int32, (m, LANES), 1)
    zero = jnp.zeros_like(q_blk)
    qs = jnp.concatenate([jnp.where(lane < DH, q_blk, zero), jnp.where(lane >= DH, q_blk, zero)], axis=0)
    layout = []
    off = 0
    for k, load_v, bias in parts:
        n = k.shape[0]
        s = _dot_nt(qs, k)
        s_ref[slot, 0:2 * m, off:off + n] = s if bias is None else s + bias
        layout.append((off, n, load_v))
        off += n
    return m, layout


def _pair_finish(handle, s_ref, slot, sinks=None):
    m, layout = handle
    width = sum(n for _, n, _ in layout)
    s = s_ref[slot, 0:2 * m, 0:width]
    mx = jnp.max(s, axis=1, keepdims=True)
    sink = None
    if sinks is not None:
        row = lax.broadcasted_iota(jnp.int32, (2 * m, 1), 0)
        sink = jnp.where(row < m, sinks[0], sinks[1]) * LOG2E
        mx = jnp.maximum(mx, sink)
    p = jnp.exp2(s - mx)
    l = jnp.sum(p, axis=1, keepdims=True)
    if sink is not None:
        l = l + jnp.exp2(sink - mx)
    pb = p.astype(BF16)
    acc = None
    for off, n, load_v in layout:
        a = _dot(pb[:, off:off + n], load_v())
        acc = a if acc is None else acc + a
    o = acc / l
    lane = lax.broadcasted_iota(jnp.int32, (m, LANES), 1)
    return jnp.where(lane < DH, o[:m], o[m:])


def _pipelined(items, scores_fn, finish_fn):
    ahead = PAIR_SLOTS - 1
    handles = [scores_fn(items[n], n % PAIR_SLOTS) for n in range(min(ahead, len(items)))]
    for n, item in enumerate(items):
        if n + ahead < len(items):
            handles.append(scores_fn(items[n + ahead], (n + ahead) % PAIR_SLOTS))
        finish_fn(item, handles[n], n % PAIR_SLOTS)


def _mod_kernel(c_ref, w_ref, b_ref, o_ref):
    c = c_ref[...]
    sc = (c * _sigmoid(c)).astype(BF16)
    o_ref[...] = _dot(sc, w_ref[...].astype(BF16)) + b_ref[...]


def _modulation(cc, w_mod, b_mod):
    n_layers, d, n = w_mod.shape
    r = cc.shape[0]
    tn = n // 4
    return pl.pallas_call(
        _mod_kernel,
        grid=(n_layers, n // tn),
        in_specs=[
            pl.BlockSpec((r, d), lambda l, j: (0, 0)),
            pl.BlockSpec((None, d, tn), lambda l, j: (l, 0, j)),
            pl.BlockSpec((None, 1, tn), lambda l, j: (l, 0, j)),
        ],
        out_specs=pl.BlockSpec((None, r, tn), lambda l, j: (l, 0, j)),
        out_shape=jax.ShapeDtypeStruct((n_layers, r, n), F32),
        compiler_params=_cparams(2),
        name="modulation",
    )(cc, w_mod, b_mod.reshape(n_layers, 1, n))


def _nbr_bias_kernel(rpb_ref, o_ref):
    pair = pl.program_id(0)
    n_dr = 2 * KH - 1
    n_dc = 2 * KW - 1
    qc = lax.broadcasted_iota(jnp.int32, (GRID_W, 2 * GRID_W), 0)
    lane = lax.broadcasted_iota(jnp.int32, (GRID_W, 2 * GRID_W), 1)
    second = lane >= GRID_W
    kc = jnp.where(second, lane - GRID_W, lane)
    c0 = jnp.clip(qc - KW // 2, 0, GRID_W - KW)
    ok = (kc >= c0) & (kc < c0 + KW)
    dc = kc - qc + (KW - 1)

    for e in range(2):
        h = 2 * pair + e

        def d2_body(d2, carry, h=h, e=e):
            def d_body(d, acc):
                v0 = rpb_ref[(h * n_dr + d2) * n_dc + d]
                v1 = rpb_ref[(h * n_dr + d2 + 1) * n_dc + d]
                return jnp.where(dc == d, jnp.where(second, v1, v0), acc)

            acc = lax.fori_loop(0, n_dc, d_body, jnp.zeros((GRID_W, 2 * GRID_W), F32), unroll=True)
            o_ref[d2, e * GRID_W:(e + 1) * GRID_W, :] = jnp.where(ok, acc * LOG2E, NEG)
            return carry

        lax.fori_loop(0, n_dr - 1, d2_body, 0)


def _nbr_bias(rpb):
    h = rpb.shape[0]
    n_pairs = 2 * KH - 2
    return pl.pallas_call(
        _nbr_bias_kernel,
        grid=(h // 2,),
        in_specs=[pl.BlockSpec(memory_space=pltpu.SMEM)],
        out_specs=pl.BlockSpec((None, n_pairs, 2 * GRID_W, 2 * GRID_W), lambda i: (i, 0, 0, 0)),
        out_shape=jax.ShapeDtypeStruct((h // 2, n_pairs, 2 * GRID_W, 2 * GRID_W), F32),
        compiler_params=_cparams(1),
        name="nbr_bias",
    )(rpb.reshape(-1))


def _store_vt(vt_ref, c, v):
    t = v.T.astype(BF16)
    ones = jnp.ones((ONES_ROWS, v.shape[0]), BF16)
    for e in range(LANES // DH):
        base = (2 * c + e) * VT_ROWS
        vt_ref[base:base + DH, :] = t[e * DH:(e + 1) * DH, :]
        vt_ref[base + DH:base + VT_ROWS, :] = ones


def _load_h(refs, tiles_per_batch):
    if len(refs) == 1:
        return refs[0][...]
    x_ref, c_ref = refs
    return jnp.where(pl.program_id(0) % tiles_per_batch == 0, c_ref[...], x_ref[...])


def _proj_kernel(*refs, n_h, tiles_per_batch):
    h = _load_h(refs[:n_h], tiles_per_batch)
    (mod_ref, g_ref, wa_ref, wb_ref, wc_ref, wd_ref, wuq_ref, wk_ref, wv_ref,
     gaq_ref, gakv_ref, gdq_ref, gdk_ref, ropea_ref, rope64_ref,
     ka_ref, qat_ref, vat_ref, pb_ref, pc_ref, kd_ref, qdt_ref, vdt_ref) = refs[n_h:]
    sh1 = mod_ref[0:1, :]
    sc1 = mod_ref[1:2, :]
    u = (_rms(h, g_ref[...]) * (1.0 + sc1) + sh1).astype(BF16)

    rb = jnp.where(lax.broadcasted_iota(jnp.int32, (LANES, LANES), 0) < DH, 0, 1)
    cb = jnp.where(lax.broadcasted_iota(jnp.int32, (LANES, LANES), 1) < DH, 0, 1)
    bd = jnp.where(rb == cb, 1.0, 0.0).astype(BF16)

    ya = _dot(u, wa_ref[...])
    yd = _dot(u, wd_ref[...])
    cq = _rms(ya[:, :Q_LORA], gaq_ref[...]).astype(BF16)
    ckv = _rms(ya[:, Q_LORA:Q_LORA + KV_LORA], gakv_ref[...]).astype(BF16)
    kr = _rope(ya[:, Q_LORA + KV_LORA:], ropea_ref, ROPE_A // 4)
    yb = _dot(u, wb_ref[...])
    qa = _dot(cq, wuq_ref[...])
    ka = _dot(ckv, wk_ref[...])
    va = _dot(ckv, wv_ref[...])

    for c in range(HQ // LANES):
        lo, hi = c * LANES, (c + 1) * LANES
        qn = _head_rms(yd[:, lo:hi], gdq_ref[...], bd)
        qdt_ref[lo:hi, :] = (_rope(qn, rope64_ref, DH // 4) * (S_H * LOG2E)).T.astype(BF16)
    kn = _head_rms(yd[:, HQ:HQ + LANES], gdk_ref[...], bd)
    kd_ref[...] = _rope(kn, rope64_ref, DH // 4).astype(BF16)
    _store_vt(vdt_ref, 0, yd[:, HQ + LANES:])

    yc = _dot(u, wc_ref[...])

    for c in range(HQ // LANES):
        lo, hi = c * LANES, (c + 1) * LANES
        pb_ref[:, lo:hi] = (_rope(yb[:, lo:hi], rope64_ref, DH // 4) * (S_H * LOG2E)).astype(BF16)
    pb_ref[:, HQ:HQ + LANES] = _rope(yb[:, HQ:HQ + LANES], rope64_ref, DH // 4).astype(BF16)
    pb_ref[:, HQ + LANES:] = yb[:, HQ + LANES:].astype(BF16)

    for hh in range(N_HEADS):
        lo, hi = hh * HEAD_PAD_A, (hh + 1) * HEAD_PAD_A
        q = _rope(qa[:, lo:hi], ropea_ref, ROPE_A // 4) * (S_A * LOG2E)
        qat_ref[lo:hi, :] = q.T.astype(BF16)
        ka_ref[:, lo:hi] = (ka[:, lo:hi] + kr).astype(BF16)
    for c in range(HQ // LANES):
        lo, hi = c * LANES, (c + 1) * LANES
        _store_vt(vat_ref, c, va[:, lo:hi])

    pc_ref[:, :HQ] = (yc[:, :HQ] * (S_H * LOG2E)).astype(BF16)
    pc_ref[:, HQ:] = yc[:, HQ:].astype(BF16)


def _h_specs(h, d, tiles_per_batch):
    if not isinstance(h, tuple):
        return [pl.BlockSpec((TM, d), lambda i: (i, 0))]
    lat = tiles_per_batch - 1
    return [pl.BlockSpec((TM, d), lambda i: ((i // tiles_per_batch) * lat
                                             + jnp.maximum(i % tiles_per_batch - 1, 0), 0)),
            pl.BlockSpec((TM, d), lambda i: (i // tiles_per_batch, 0))]


def _project(h, mod, g_pre, w, tabs, n_batch, tiles_per_batch):
    hs = h if isinstance(h, tuple) else (h,)
    d = hs[0].shape[1]
    m = n_batch * tiles_per_batch * TM
    s = tiles_per_batch * TM
    wq_a = N_HEADS * HEAD_PAD_A
    row = lambda i: (i, 0)
    col = lambda i: (i // tiles_per_batch, 0, i % tiles_per_batch)
    tile_pos = lambda i: (0, i % tiles_per_batch, 0)
    in_specs = _h_specs(h, d, tiles_per_batch) + [
        _mod_spec(d, tiles_per_batch),
        _resident((1, d)),
        _resident(w["wa"].shape), _resident(w["wb"].shape), _resident(w["wc"].shape),
        _resident(w["wd"].shape), _resident(w["wuq"].shape),
        _resident(w["wk"].shape), _resident(w["wv"].shape),
        _resident((1, Q_LORA)), _resident((1, KV_LORA)), _resident((1, LANES)), _resident((1, LANES)),
        pl.BlockSpec((3, TM, LANES), tile_pos),
        pl.BlockSpec((3, TM, LANES), tile_pos),
    ]

    def rows_out(c):
        return pl.BlockSpec((TM, c), row), jax.ShapeDtypeStruct((m, c), BF16)

    def cols_out(r):
        return pl.BlockSpec((None, r, TM), col), jax.ShapeDtypeStruct((n_batch, r, s), BF16)

    outs = [rows_out(wq_a), cols_out(wq_a), cols_out(N_HEADS * VT_ROWS),
            rows_out(HQ + 2 * LANES), rows_out(3 * HQ),
            rows_out(LANES), cols_out(HQ), cols_out(KV_D * VT_ROWS)]
    return pl.pallas_call(
        functools.partial(_proj_kernel, n_h=len(hs), tiles_per_batch=tiles_per_batch),
        grid=(m // TM,),
        in_specs=in_specs,
        out_specs=[o[0] for o in outs],
        out_shape=[o[1] for o in outs],
        compiler_params=_cparams(1),
        name="proj",
    )(*hs, mod, g_pre, w["wa"], w["wb"], w["wc"], w["wd"], w["wuq"], w["wk"], w["wv"],
      w["gaq"], w["gakv"], w["gdq"], w["gdk"], tabs["rope_a"], tabs["rope_64"])


def _dense_attn_kernel(ka_ref, qat_ref, vat_ref, kd_ref, qdt_ref, vdt_ref, oa_ref, od_ref, ot_ref, st_ref,
                       *, n_ctx, q_first):
    qi = pl.program_id(1) + q_first
    group_d = N_HEADS // (LANES // DH)
    items = [(mixer, h) for mixer in range(2) for h in range(N_HEADS)]

    def attend(n_keys):
        n_chunks = -(-n_keys // KEY_CHUNK)
        bounds = [(i * n_keys // n_chunks, (i + 1) * n_keys // n_chunks) for i in range(n_chunks)]

        def q_operand(item):
            mixer, h = item
            if mixer == 0:
                return qat_ref[h * LANES:(h + 1) * LANES, :]
            qh = qdt_ref[h * DH:(h + 1) * DH, :]
            z = jnp.zeros_like(qh)
            return jnp.concatenate([qh, z] if h // group_d == 0 else [z, qh], axis=0)

        def scores(n, qt, i):
            mixer, h = items[n]
            lo, hi = bounds[i]
            k = ka_ref[lo:hi, h * LANES:(h + 1) * LANES] if mixer == 0 else kd_ref[lo:hi, :]
            st_ref[n % DENSE_SLOTS, lo:hi, :] = _dot(k, qt)

        def colmax(n):
            m = None
            for lo, hi in bounds:
                mi = jnp.max(st_ref[n % DENSE_SLOTS, lo:hi, :], axis=0, keepdims=True)
                m = mi if m is None else jnp.maximum(m, mi)
            return m

        ahead = DENSE_SLOTS - 1
        for n0 in range(ahead):
            qt = q_operand(items[n0])
            for i in range(n_chunks):
                scores(n0, qt, i)
        for n, (mixer, h) in enumerate(items):
            vt_ref, g = (vat_ref, h) if mixer == 0 else (vdt_ref, h // group_d)
            m = colmax(n)
            qt = q_operand(items[n + ahead]) if n + ahead < len(items) else None
            acc = None
            for i in range(n_chunks):
                if qt is not None:
                    scores(n + ahead, qt, i)
                lo, hi = bounds[i]
                p = jnp.exp2(st_ref[n % DENSE_SLOTS, lo:hi, :] - m).astype(BF16)
                ai = _dot(vt_ref[g * VT_ROWS:(g + 1) * VT_ROWS, lo:hi], p)
                acc = ai if acc is None else acc + ai
            ot_ref[mixer, h * DH:(h + 1) * DH, :] = acc[:DH] / acc[DH:DH + 1]
            if h == N_HEADS - 1:
                o_ref = oa_ref if mixer == 0 else od_ref
                o_ref[...] = ot_ref[mixer].T.astype(o_ref.dtype)

    @pl.when(qi < n_ctx // TM)
    def _():
        attend(n_ctx)

    @pl.when(qi >= n_ctx // TM)
    def _():
        attend(ka_ref.shape[0])


def _dense_attention(ka, qat, vat, kd, qdt, vdt, n_ctx, q_first):
    b, s, _ = ka.shape
    whole = lambda bi, qi: (bi, 0, 0)
    q_tile = lambda bi, qi: (bi, 0, qi + q_first)
    o_tile = lambda bi, qi: (bi, qi + q_first, 0)
    return pl.pallas_call(
        functools.partial(_dense_attn_kernel, n_ctx=n_ctx, q_first=q_first),
        grid=(b, s // TM - q_first),
        in_specs=[
            pl.BlockSpec((None, s, ka.shape[2]), whole),
            pl.BlockSpec((None, qat.shape[1], TM), q_tile),
            pl.BlockSpec((None, vat.shape[1], s), whole),
            pl.BlockSpec((None, s, kd.shape[2]), whole),
            pl.BlockSpec((None, qdt.shape[1], TM), q_tile),
            pl.BlockSpec((None, vdt.shape[1], s), whole),
        ],
        out_specs=[pl.BlockSpec((None, TM, HQ), o_tile)] * 2,
        out_shape=[jax.ShapeDtypeStruct((b, s, HQ), BF16)] * 2,
        scratch_shapes=[pltpu.VMEM((2, HQ, TM), F32), pltpu.VMEM((DENSE_SLOTS, s, TM), F32)],
        compiler_params=_cparams(2),
        name="attn_dense",
    )(ka, qat, vat, kd, qdt, vdt)


def _pair_attn_kernel(sink_ref, qb_ref, kb_ref, vb_ref, qc_ref, kc_ref, vc_ref, tb_ref, ob_ref, oc_ref, s_ref,
                      *, n_ctx, q_first):
    qi = pl.program_id(1) + q_first
    n_lat = kb_ref.shape[0] - n_ctx
    n_blk = HQ // LANES
    lanes = lambda j: slice(j * LANES, (j + 1) * LANES)

    def ctx_b():
        return kb_ref[0:n_ctx, :], lambda: vb_ref[0:n_ctx, :], None

    def ctx_c(j):
        return kc_ref[0:n_ctx, lanes(j)], lambda: vc_ref[0:n_ctx, lanes(j)], None

    def sinks(j):
        return sink_ref[GQA_HEAD_ORDER[2 * j]], sink_ref[GQA_HEAD_ORDER[2 * j + 1]]

    @pl.when(qi < n_ctx // TM)
    def _():
        def scores(item, slot):
            mixer, j = item
            if mixer == "b":
                return _pair_scores(qb_ref[:, lanes(j)], [ctx_b()], s_ref, slot)
            return _pair_scores(qc_ref[:, lanes(j)], [ctx_c(j)], s_ref, slot)

        def finish(item, handle, slot):
            mixer, j = item
            if mixer == "b":
                ob_ref[:, lanes(j)] = _pair_finish(handle, s_ref, slot, sinks(j)).astype(ob_ref.dtype)
            else:
                oc_ref[:, lanes(j)] = _pair_finish(handle, s_ref, slot).astype(oc_ref.dtype)

        _pipelined([(mixer, j) for mixer in "bc" for j in range(n_blk)], scores, finish)

    @pl.when(qi >= n_ctx // TM)
    def _():
        span_b = TM + 2 * WINDOW
        q0 = (qi - n_ctx // TM) * TM
        start = pl.multiple_of(jnp.clip(q0 - WINDOW, 0, n_lat - span_b), WINDOW)
        dist = (lax.broadcasted_iota(jnp.int32, (2 * TM, span_b), 0)
                - lax.broadcasted_iota(jnp.int32, (2 * TM, span_b), 1) + (q0 - start))
        dist = jnp.where(lax.broadcasted_iota(jnp.int32, (2 * TM, span_b), 0) >= TM, dist - TM, dist)
        bias_b = jnp.where(jnp.abs(dist) <= WINDOW, 0.0, NEG)
        rows_b = pl.ds(n_ctx + start, span_b)
        grid_rows = n_lat // GRID_W
        rows_per_tile = TM // GRID_W
        span_c = KH * GRID_W

        def scores(item, slot):
            if item[0] == "b":
                j = item[1]
                return _pair_scores(qb_ref[:, lanes(j)],
                                    [(kb_ref[rows_b, :], lambda: vb_ref[rows_b, :], bias_b), ctx_b()], s_ref, slot)
            _, rr, j = item
            r = (qi - n_ctx // TM) * rows_per_tile + rr
            r0 = jnp.clip(r - KH // 2, 0, grid_rows - KH)
            delta = r - r0
            krows = pl.ds(pl.multiple_of(n_ctx + r0 * GRID_W, GRID_W), span_c)
            bias = jnp.concatenate(
                [tb_ref[j, 2 * jj - delta + (KH - 1)] for jj in range(KH // 2)], axis=1)
            return _pair_scores(qc_ref[rr * GRID_W:(rr + 1) * GRID_W, lanes(j)],
                                [(kc_ref[krows, lanes(j)], lambda: vc_ref[krows, lanes(j)], bias), ctx_c(j)],
                                s_ref, slot)

        def finish(item, handle, slot):
            if item[0] == "b":
                j = item[1]
                ob_ref[:, lanes(j)] = _pair_finish(handle, s_ref, slot, sinks(j)).astype(ob_ref.dtype)
            else:
                _, rr, j = item
                oc_ref[rr * GRID_W:(rr + 1) * GRID_W, lanes(j)] = (
                    _pair_finish(handle, s_ref, slot).astype(oc_ref.dtype))

        items = [("b", j) for j in range(n_blk)]
        items += [("c", rr, j) for rr in range(rows_per_tile) for j in range(n_blk)]
        _pipelined(items, scores, finish)


def _pair_attention(pb, pc, sink, bias_tab, n_ctx, q_first):
    b, s, _ = pb.shape
    q_tile = lambda bi, qi: (bi, qi + q_first, 0)
    col_blk = lambda c: (lambda bi, qi: (bi, 0, c))
    return pl.pallas_call(
        functools.partial(_pair_attn_kernel, n_ctx=n_ctx, q_first=q_first),
        grid=(b, s // TM - q_first),
        in_specs=[
            pl.BlockSpec(memory_space=pltpu.SMEM),
            pl.BlockSpec((None, TM, HQ), q_tile),
            pl.BlockSpec((None, s, LANES), col_blk(HQ // LANES)),
            pl.BlockSpec((None, s, LANES), col_blk(HQ // LANES + 1)),
            pl.BlockSpec((None, TM, HQ), q_tile),
            pl.BlockSpec((None, s, HQ), col_blk(1)),
            pl.BlockSpec((None, s, HQ), col_blk(2)),
            _resident(bias_tab.shape),
        ],
        out_specs=[pl.BlockSpec((None, TM, HQ), q_tile)] * 2,
        out_shape=[jax.ShapeDtypeStruct((b, s, HQ), BF16)] * 2,
        scratch_shapes=[pltpu.VMEM((PAIR_SLOTS, 2 * TM, max(TM + 2 * WINDOW, KH * GRID_W) + n_ctx), F32)],
        compiler_params=_cparams(2),
        name="attn_pair",
    )(sink, pb, pb, pb, pc, pc, pc, bias_tab)


def _merge_kernel(*refs, n_h, tiles_per_batch):
    h = _load_h(refs[:n_h], tiles_per_batch)
    (oa_ref, ob_ref, oc_ref, od_ref, mod_ref, gpre_ref, gpost_ref,
     wg_ref, wbr_ref, wout_ref, o_ref, acc_ref) = refs[n_h:]
    d_model = h.shape[1]
    cw = 256
    sh1 = mod_ref[0:1, :]
    sc1 = mod_ref[1:2, :]
    ga1 = mod_ref[2:3, :]
    branches = (oa_ref, ob_ref, oc_ref, od_ref)

    def up_proj(n, c):
        return _dot(branches[n][...], wbr_ref[n, :, c * cw:(c + 1) * cw])

    up0 = [up_proj(n, 0) for n in range(N_BRANCH)]
    u = (_rms(h, gpre_ref[...]) * (1.0 + sc1) + sh1).astype(BF16)
    for c in range(d_model // cw):
        acc = None
        for n in range(N_BRANCH):
            lo = n * d_model + c * cw
            gate = _dot(u, wg_ref[:, lo:lo + cw])
            up = up0[n] if c == 0 else up_proj(n, c)
            term = _sigmoid(gate) * up
            acc = term if acc is None else acc + term
        acc_ref[:, c * cw:(c + 1) * cw] = acc.astype(BF16)
    y = _dot(acc_ref[...], wout_ref[...])
    o_ref[...] = h + ga1 * _rms(y, gpost_ref[...])


def _merge(o_a, o_b, o_c, o_d, h, mod, g_pre, g_post, wg, wbr, wout, tiles_per_batch, latent_only):
    split = isinstance(h, tuple)
    hs = h if split else (h,)
    m, d = o_a.shape[0], hs[0].shape[1]
    n, row_of, mod_of = _tile_visits(m, tiles_per_batch, latent_only)
    assert not (split and latent_only)
    row = lambda i: (row_of(i), 0)
    h_specs = _h_specs(h, d, tiles_per_batch) if split else [pl.BlockSpec((TM, d), row)]
    return pl.pallas_call(
        functools.partial(_merge_kernel, n_h=len(hs), tiles_per_batch=tiles_per_batch),
        grid=(n,),
        in_specs=h_specs + [pl.BlockSpec((TM, HQ), row)] * N_BRANCH + [
            pl.BlockSpec((None, None, MOD_CHUNKS, d), lambda i: (*mod_of(i), 0, 0)),
            _resident((1, d)), _resident((1, d)),
            _resident(wg.shape), _resident(wbr.shape), _resident(wout.shape),
        ],
        out_specs=pl.BlockSpec((TM, d), row),
        out_shape=jax.ShapeDtypeStruct((m, d), F32),
        scratch_shapes=[pltpu.VMEM((TM, d), BF16)],
        input_output_aliases={} if split else {0: 0},
        compiler_params=_cparams(1),
        name="merge",
    )(*hs, o_a, o_b, o_c, o_d, mod, g_pre, g_post, wg, wbr, wout)


def _ffn_kernel(hn_ref, hp_ref, modn_ref, modp_ref, gpre_ref, gpost_ref, w1_ref, w2_ref, o_ref,
                u_ref, y_ref, act_ref):
    i = pl.program_id(0)
    d_ff = w2_ref.shape[0]
    cw = 256
    cur = i % 2
    oth = 1 - cur

    def pre_norm(h_ref, mod_ref):
        return _rms(h_ref[...], gpre_ref[...]) * (1.0 + mod_ref[4:5, :]) + mod_ref[3:4, :]

    @pl.when(i == 0)
    def _():
        u_ref[0] = pre_norm(hp_ref, modp_ref).astype(BF16)
        y_ref[1] = jnp.zeros(y_ref.shape[1:], F32)

    out = hp_ref[...] + modp_ref[5:6, :] * _rms(y_ref[oth], gpost_ref[...])
    o_ref[...] = out
    u_next = pre_norm(hn_ref, modn_ref)
    for c in range(d_ff // cw):
        a = _dot(u_ref[cur], w1_ref[:, c * cw:(c + 1) * cw])
        b = _dot(u_ref[cur], w1_ref[:, d_ff + c * cw:d_ff + (c + 1) * cw])
        act_ref[:, c * cw:(c + 1) * cw] = (a * _sigmoid(a) * b).astype(BF16)
    u_ref[oth] = u_next.astype(BF16)
    y = _dot(act_ref[...], w2_ref[...])
    y_ref[cur] = y
    r0, c0 = y.shape[0] - F32_SUBLANES, y.shape[1] - LANES
    y_ref[cur, r0:, c0:] = y[r0:, c0:] + _zero_after(u_next, out)


def _ffn(h, mod, g_pre, g_post, w1, w2, tiles_per_batch, latent_only):
    m, d = h.shape
    n, row_of, mod_of = _tile_visits(m, tiles_per_batch, latent_only)
    p = _StepPipeline(n, row_of, mod_of, d)
    return pl.pallas_call(
        _ffn_kernel,
        grid=(n + 1,),
        in_specs=[p.rows_next, p.rows_prev, p.mod_next, p.mod_prev,
                  _resident((1, d)), _resident((1, d)), _resident(w1.shape), _resident(w2.shape)],
        out_specs=p.rows_out,
        out_shape=jax.ShapeDtypeStruct((n * TM, d), F32),
        scratch_shapes=[pltpu.VMEM((2, TM, d), BF16), pltpu.VMEM((2, TM, d), F32),
                        pltpu.VMEM((TM, w2.shape[0]), BF16)],
        compiler_params=_cparams(1),
        name="ffn",
    )(h, h, mod, mod, g_pre, g_post, w1, w2)


def _rope_tables(n_ctx, n_lat):
    t = jnp.arange(n_lat)
    rows_pos = (t // GRID_W).astype(F32)
    cols_pos = (t % GRID_W).astype(F32)

    def block(d):
        q = d // 4
        inv = ROPE_THETA ** (-jnp.arange(q, dtype=F32) * 2.0 / (d // 2))
        ar = rows_pos[:, None] * inv[None, :]
        ac = cols_pos[:, None] * inv[None, :]
        z = jnp.zeros_like(ar)
        cos = jnp.concatenate([jnp.cos(ar), jnp.cos(ar), jnp.cos(ac), jnp.cos(ac)], axis=1)
        up = jnp.concatenate([-jnp.sin(ar), z, -jnp.sin(ac), z], axis=1)
        dn = jnp.concatenate([z, jnp.sin(ar), z, jnp.sin(ac)], axis=1)
        return cos, up, dn

    def with_ctx(tab, fill):
        return jnp.concatenate([jnp.full((n_ctx, tab.shape[1]), fill, F32), tab], axis=0)

    c64, u64, d64 = block(DH)
    rope_64 = jnp.stack([with_ctx(jnp.tile(c64, (1, LANES // DH)), 1.0),
                         with_ctx(jnp.tile(u64, (1, LANES // DH)), 0.0),
                         with_ctx(jnp.tile(d64, (1, LANES // DH)), 0.0)])
    ca, ua, da = block(ROPE_A)
    pad_lo = NOPE_A
    pad_hi = HEAD_PAD_A - NOPE_A - ROPE_A

    def widen(tab, fill):
        return jnp.concatenate([jnp.full((n_lat, pad_lo), fill, F32), tab,
                                jnp.full((n_lat, pad_hi), fill, F32)], axis=1)

    rope_a = jnp.stack([with_ctx(widen(ca, 1.0), 1.0), with_ctx(widen(ua, 0.0), 0.0),
                        with_ctx(widen(da, 0.0), 0.0)])
    return {"rope_a": rope_a, "rope_64": rope_64}


def _permute_heads(w, axis, order):
    shape = w.shape
    w = w.reshape(shape[:axis] + (N_HEADS, DH) + shape[axis + 1:])
    w = jnp.take(w, jnp.asarray(order), axis=axis)
    return w.reshape(shape)


def _layer_weights(l, w_in, g_a_q, g_a_kv, w_a_uq, w_a_ukv, g_d_q, g_d_k, w_branch):
    d = w_in.shape[1]
    wl = w_in[l]
    o = 0
    a_cq = wl[:, o:o + Q_LORA]; o += Q_LORA
    a_ckv = wl[:, o:o + KV_LORA]; o += KV_LORA
    a_kr = wl[:, o:o + ROPE_A]; o += ROPE_A
    w_bq = wl[:, o:o + HQ]; o += HQ
    w_bkv = wl[:, o:o + 2 * KV_B * DH]; o += 2 * KV_B * DH
    w_c = wl[:, o:o + 3 * HQ]; o += 3 * HQ
    w_d = wl[:, o:o + HQ + 2 * KV_D * DH]; o += HQ + 2 * KV_D * DH
    w_g = wl[:, o:]
    zeros = lambda n: jnp.zeros((d, n), wl.dtype)
    wa = jnp.concatenate([a_cq, a_ckv, zeros(NOPE_A), a_kr, zeros(HEAD_PAD_A - NOPE_A - ROPE_A)], axis=1)
    w_b = jnp.concatenate([_permute_heads(w_bq, 1, GQA_HEAD_ORDER), w_bkv], axis=1)
    uq = w_a_uq[l].reshape(Q_LORA, N_HEADS, NOPE_A + ROPE_A)
    uq = jnp.pad(uq, ((0, 0), (0, 0), (0, HEAD_PAD_A - NOPE_A - ROPE_A))).reshape(Q_LORA, N_HEADS * HEAD_PAD_A)
    ukv = w_a_ukv[l].reshape(KV_LORA, N_HEADS, NOPE_A + DH)
    wk = jnp.pad(ukv[:, :, :NOPE_A], ((0, 0), (0, 0), (0, HEAD_PAD_A - NOPE_A))).reshape(KV_LORA, N_HEADS * HEAD_PAD_A)
    wv = ukv[:, :, NOPE_A:].reshape(KV_LORA, HQ)
    wbr = w_branch[l]
    wbr = jnp.stack([wbr[0], _permute_heads(wbr[1], 0, GQA_HEAD_ORDER), wbr[2], wbr[3]])
    cast = lambda x: x.astype(BF16)
    return {
        "wa": cast(wa), "wb": cast(w_b), "wc": cast(w_c), "wd": cast(w_d), "wg": cast(w_g),
        "wuq": cast(uq), "wk": cast(wk), "wv": cast(wv), "wbr": cast(wbr),
        "gaq": g_a_q[l][None, :], "gakv": g_a_kv[l][None, :],
        "gdq": jnp.tile(g_d_q[l], LANES // DH)[None, :], "gdk": jnp.tile(g_d_k[l], LANES // DH)[None, :],
    }


def kernel(x, c, ctx, c_ctx, w_mod, b_mod, g_pre_mix, g_post_mix, g_pre_ffn, g_post_ffn,
           w_in, g_a_q, g_a_kv, w_a_uq, w_a_ukv, sink_b, rpb_c, g_d_q, g_d_k,
           w_branch, w_out, w_ffn_in, w_ffn_out):
    b, n_lat, d = x.shape
    n_ctx = ctx.shape[1]
    s = n_ctx + n_lat
    depth = w_mod.shape[0]
    assert n_ctx == TM and n_lat % TM == 0 and n_lat >= TM + 2 * WINDOW
    assert n_lat % GRID_W == 0 and n_lat // GRID_W >= KH
    assert KV_B * DH == LANES and KV_D * DH == LANES
    tiles_per_batch = s // TM

    n_rows = -(-(b + 1) // F32_SUBLANES) * F32_SUBLANES
    cc = jnp.zeros((n_rows, d), F32).at[:b].set(c).at[b].set(c_ctx)
    mods = _modulation(cc, w_mod, b_mod)
    mod_x = mods[:, :b].reshape(depth, b, 1, MOD_CHUNKS, d)
    mod_c = jnp.broadcast_to(mods[:, b].reshape(depth, 1, 1, MOD_CHUNKS, d), mod_x.shape)
    mod_tab = jnp.concatenate([mod_c, mod_x], axis=2)

    tabs = _rope_tables(n_ctx, n_lat)
    h = (x.reshape(b * n_lat, d), ctx.reshape(b * n_ctx, d))
    if depth == 1:
        h = jnp.concatenate([ctx, x], axis=1).reshape(b * s, d)

    for l in range(depth):
        w = _layer_weights(l, w_in, g_a_q, g_a_kv, w_a_uq, w_a_ukv, g_d_q, g_d_k, w_branch)
        mod = mod_tab[l]
        g_pre = g_pre_mix[l][None, :]
        last = l == depth - 1
        q_first = n_ctx // TM if last else 0
        ka, qat, vat, pb, pc, kd, qdt, vdt = _project(h, mod, g_pre, w, tabs, b, tiles_per_batch)
        o_a, o_d = _dense_attention(ka.reshape(b, s, -1), qat, vat, kd.reshape(b, s, -1), qdt, vdt,
                                    n_ctx, q_first)
        o_b, o_c = _pair_attention(pb.reshape(b, s, -1), pc.reshape(b, s, -1), sink_b[l], _nbr_bias(rpb_c[l]),
                                   n_ctx, q_first)
        flat = lambda t: t.reshape(b * s, HQ)
        h = _merge(flat(o_a), flat(o_b), flat(o_c), flat(o_d), h, mod, g_pre, g_post_mix[l][None, :],
                   w["wg"], w["wbr"], w_out[l].astype(BF16), tiles_per_batch, last)
        h = _ffn(h, mod, g_pre_ffn[l][None, :], g_post_ffn[l][None, :],
                 w_ffn_in[l].astype(BF16), w_ffn_out[l].astype(BF16), tiles_per_batch, last)
    return h.reshape(b, n_lat, d)
```

```python
import functools

import jax
import jax.numpy as jnp
from jax import lax
from jax.experimental import pallas as pl
from jax.experimental.pallas import tpu as pltpu

F32 = jnp.float32
BF16 = jnp.bfloat16
UINT = jnp.uint32

GRID_W = 64
ROPE_THETA = 10000.0
EPS = 1e-6
NEG = -1e30
DH = 64
N_HEADS = 8
N_BRANCH = 4
Q_LORA = 256
KV_LORA = 128
NOPE_A = 64
ROPE_A = 32
KV_B = 2
WINDOW = 128
KH = 8
KW = 16
KV_D = 2
MOD_CHUNKS = 6

LANES = 128
F32_SUBLANES = 8
VMEM_LIMIT_BYTES = 56 * 1024 * 1024

TM = 256
HEAD_PAD_A = 128
HQ = N_HEADS * DH
S_A = (NOPE_A + ROPE_A) ** -0.5
S_H = DH ** -0.5
LOG2E = 1.4426950408889634
KEY_CHUNK = 1152
DENSE_SLOTS = 3
PAIR_SLOTS = 3
ONES_ROWS = 16
VT_ROWS = DH + ONES_ROWS
GQA_HEAD_ORDER = tuple(h for j in range(N_HEADS // 2) for h in (j, j + N_HEADS // 2))


def _cparams(n_axes):
    return pltpu.CompilerParams(
        dimension_semantics=("arbitrary",) * n_axes, vmem_limit_bytes=VMEM_LIMIT_BYTES)


def _resident(shape):
    nd = len(shape)
    return pl.BlockSpec(shape, lambda *_: (0,) * nd, pipeline_mode=pl.Buffered(1))


def _tile_visits(m, tiles_per_batch, latent_only):
    lat = tiles_per_batch - 1
    if latent_only:
        return (m // TM // tiles_per_batch * lat,
                lambda j: (j // lat) * tiles_per_batch + 1 + j % lat,
                lambda j: (j // lat, 1))
    return (m // TM, lambda j: j,
            lambda j: (j // tiles_per_batch, jnp.minimum(j % tiles_per_batch, 1)))


class _StepPipeline:
    def __init__(self, n, row_of, mod_of, d):
        nxt = lambda i: jnp.minimum(i + 1, n - 1)
        prv = lambda i: jnp.maximum(i - 1, 0)
        mod_block = (None, None, MOD_CHUNKS, d)
        self.rows_next = pl.BlockSpec((TM, d), lambda i: (row_of(nxt(i)), 0))
        self.rows_prev = pl.BlockSpec((TM, d), lambda i: (row_of(prv(i)), 0))
        self.mod_next = pl.BlockSpec(mod_block, lambda i: (*mod_of(nxt(i)), 0, 0))
        self.mod_prev = pl.BlockSpec(mod_block, lambda i: (*mod_of(prv(i)), 0, 0))
        self.rows_out = pl.BlockSpec((TM, d), lambda i: (prv(i), 0))


def _mod_spec(d, tiles_per_batch):
    return pl.BlockSpec((None, None, MOD_CHUNKS, d),
                        lambda i: (i // tiles_per_batch, jnp.minimum(i % tiles_per_batch, 1), 0, 0))


def _dot(a, b):
    return jnp.dot(a, b, preferred_element_type=F32)


def _dot_nt(a, b):
    return lax.dot_general(a, b, (((1,), (1,)), ((), ())), preferred_element_type=F32)


def _rms(x, g):
    ms = jnp.mean(x * x, axis=-1, keepdims=True)
    return x * lax.rsqrt(ms + EPS) * g


def _sigmoid(x):
    return 1.0 / (1.0 + jnp.exp(-x))


def _zero_after(*tiles):
    r = None
    for t in tiles:
        b = pltpu.bitcast(t, UINT)
        acc = b[:, 0:LANES]
        for c in range(1, b.shape[1] // LANES):
            acc = acc | b[:, c * LANES:(c + 1) * LANES]
        fold = acc[0:F32_SUBLANES]
        for k in range(1, acc.shape[0] // F32_SUBLANES):
            fold = fold | acc[k * F32_SUBLANES:(k + 1) * F32_SUBLANES]
        r = fold if r is None else r | fold
    half = jnp.iinfo(UINT).bits // 2
    z = lax.shift_right_logical(lax.shift_right_logical(r, jnp.array(half, UINT)), jnp.array(half, UINT))
    return pltpu.bitcast(z, F32)


def _rope(x, tab_ref, shift):
    n = x.shape[-1]
    return (x * tab_ref[0] + pltpu.roll(x, n - shift, 1) * tab_ref[1]
            + pltpu.roll(x, shift, 1) * tab_ref[2])


def _head_rms(x, g, bd):
    x2 = x * x
    hi = x2.astype(BF16)
    lo = (x2 - hi.astype(F32)).astype(BF16)
    ss = _dot(hi, bd) + _dot(lo, bd)
    return x * lax.rsqrt(ss * (1.0 / DH) + EPS) * g


def _pair_scores(q_blk, parts, s_ref, slot):
    m = q_blk.shape[0]
    lane = lax.broadcasted_iota(jnp.int32, (m, LANES), 1)
    zero = jnp.zeros_like(q_blk)
    qs = jnp.concatenate([jnp.where(lane < DH, q_blk, zero), jnp.where(lane >= DH, q_blk, zero)], axis=0)
    layout = []
    off = 0
    for k, load_v, bias in parts:
        n = k.shape[0]
        s = _dot_nt(qs, k)
        s_ref[slot, 0:2 * m, off:off + n] = s if bias is None else s + bias
        layout.append((off, n, load_v))
        off += n
    return m, layout


def _pair_finish(handle, s_ref, slot, sinks=None):
    m, layout = handle
    width = sum(n for _, n, _ in layout)
    s = s_ref[slot, 0:2 * m, 0:width]
    mx = jnp.max(s, axis=1, keepdims=True)
    sink = None
    if sinks is not None:
        row = lax.broadcasted_iota(jnp.int32, (2 * m, 1), 0)
        sink = jnp.where(row < m, sinks[0], sinks[1]) * LOG2E
        mx = jnp.maximum(mx, sink)
    p = jnp.exp2(s - mx)
    l = jnp.sum(p, axis=1, keepdims=True)
    if sink is not None:
        l = l + jnp.exp2(sink - mx)
    pb = p.astype(BF16)
    acc = None
    for off, n, load_v in layout:
        a = _dot(pb[:, off:off + n], load_v())
        acc = a if acc is None else acc + a
    o = acc / l
    lane = lax.broadcasted_iota(jnp.int32, (m, LANES), 1)
    return jnp.where(lane < DH, o[:m], o[m:])


def _pipelined(items, scores_fn, finish_fn):
    ahead = PAIR_SLOTS - 1
    handles = [scores_fn(items[n], n % PAIR_SLOTS) for n in range(min(ahead, len(items)))]
    for n, item in enumerate(items):
        if n + ahead < len(items):
            handles.append(scores_fn(items[n + ahead], (n + ahead) % PAIR_SLOTS))
        finish_fn(item, handles[n], n % PAIR_SLOTS)


def _mod_kernel(c_ref, w_ref, b_ref, o_ref):
    c = c_ref[...]
    sc = (c * _sigmoid(c)).astype(BF16)
    o_ref[...] = _dot(sc, w_ref[...].astype(BF16)) + b_ref[...]


def _modulation(cc, w_mod, b_mod):
    n_layers, d, n = w_mod.shape
    r = cc.shape[0]
    tn = n // 4
    return pl.pallas_call(
        _mod_kernel,
        grid=(n_layers, n // tn),
        in_specs=[
            pl.BlockSpec((r, d), lambda l, j: (0, 0)),
            pl.BlockSpec((None, d, tn), lambda l, j: (l, 0, j)),
            pl.BlockSpec((None, 1, tn), lambda l, j: (l, 0, j)),
        ],
        out_specs=pl.BlockSpec((None, r, tn), lambda l, j: (l, 0, j)),
        out_shape=jax.ShapeDtypeStruct((n_layers, r, n), F32),
        compiler_params=_cparams(2),
        name="modulation",
    )(cc, w_mod, b_mod.reshape(n_layers, 1, n))


def _nbr_bias_kernel(rpb_ref, o_ref):
    pair = pl.program_id(0)
    n_dr = 2 * KH - 1
    n_dc = 2 * KW - 1
    qc = lax.broadcasted_iota(jnp.int32, (GRID_W, 2 * GRID_W), 0)
    lane = lax.broadcasted_iota(jnp.int32, (GRID_W, 2 * GRID_W), 1)
    second = lane >= GRID_W
    kc = jnp.where(second, lane - GRID_W, lane)
    c0 = jnp.clip(qc - KW // 2, 0, GRID_W - KW)
    ok = (kc >= c0) & (kc < c0 + KW)
    dc = kc - qc + (KW - 1)

    for e in range(2):
        h = 2 * pair + e

        def d2_body(d2, carry, h=h, e=e):
            def d_body(d, acc):
                v0 = rpb_ref[(h * n_dr + d2) * n_dc + d]
                v1 = rpb_ref[(h * n_dr + d2 + 1) * n_dc + d]
                return jnp.where(dc == d, jnp.where(second, v1, v0), acc)

            acc = lax.fori_loop(0, n_dc, d_body, jnp.zeros((GRID_W, 2 * GRID_W), F32), unroll=True)
            o_ref[d2, e * GRID_W:(e + 1) * GRID_W, :] = jnp.where(ok, acc * LOG2E, NEG)
            return carry

        lax.fori_loop(0, n_dr - 1, d2_body, 0)


def _nbr_bias(rpb):
    h = rpb.shape[0]
    n_pairs = 2 * KH - 2
    return pl.pallas_call(
        _nbr_bias_kernel,
        grid=(h // 2,),
        in_specs=[pl.BlockSpec(memory_space=pltpu.SMEM)],
        out_specs=pl.BlockSpec((None, n_pairs, 2 * GRID_W, 2 * GRID_W), lambda i: (i, 0, 0, 0)),
        out_shape=jax.ShapeDtypeStruct((h // 2, n_pairs, 2 * GRID_W, 2 * GRID_W), F32),
        compiler_params=_cparams(1),
        name="nbr_bias",
    )(rpb.reshape(-1))


def _store_vt(vt_ref, c, v):
    t = v.T.astype(BF16)
    ones = jnp.ones((ONES_ROWS, v.shape[0]), BF16)
    for e in range(LANES // DH):
        base = (2 * c + e) * VT_ROWS
        vt_ref[base:base + DH, :] = t[e * DH:(e + 1) * DH, :]
        vt_ref[base + DH:base + VT_ROWS, :] = ones


def _load_h(refs, tiles_per_batch):
    if len(refs) == 1:
        return refs[0][...]
    x_ref, c_ref = refs
    return jnp.where(pl.program_id(0) % tiles_per_batch == 0, c_ref[...], x_ref[...])


def _proj_kernel(*refs, n_h, tiles_per_batch):
    h = _load_h(refs[:n_h], tiles_per_batch)
    (mod_ref, g_ref, wa_ref, wb_ref, wc_ref, wd_ref, wuq_ref, wk_ref, wv_ref,
     gaq_ref, gakv_ref, gdq_ref, gdk_ref, ropea_ref, rope64_ref,
     ka_ref, qat_ref, vat_ref, pb_ref, pc_ref, kd_ref, qdt_ref, vdt_ref) = refs[n_h:]
    sh1 = mod_ref[0:1, :]
    sc1 = mod_ref[1:2, :]
    u = (_rms(h, g_ref[...]) * (1.0 + sc1) + sh1).astype(BF16)

    rb = jnp.where(lax.broadcasted_iota(jnp.int32, (LANES, LANES), 0) < DH, 0, 1)
    cb = jnp.where(lax.broadcasted_iota(jnp.int32, (LANES, LANES), 1) < DH, 0, 1)
    bd = jnp.where(rb == cb, 1.0, 0.0).astype(BF16)

    ya = _dot(u, wa_ref[...])
    yd = _dot(u, wd_ref[...])
    cq = _rms(ya[:, :Q_LORA], gaq_ref[...]).astype(BF16)
    ckv = _rms(ya[:, Q_LORA:Q_LORA + KV_LORA], gakv_ref[...]).astype(BF16)
    kr = _rope(ya[:, Q_LORA + KV_LORA:], ropea_ref, ROPE_A // 4)
    yb = _dot(u, wb_ref[...])
    qa = _dot(cq, wuq_ref[...])
    ka = _dot(ckv, wk_ref[...])
    va = _dot(ckv, wv_ref[...])

    for c in range(HQ // LANES):
        lo, hi = c * LANES, (c + 1) * LANES
        qn = _head_rms(yd[:, lo:hi], gdq_ref[...], bd)
        qdt_ref[lo:hi, :] = (_rope(qn, rope64_ref, DH // 4) * (S_H * LOG2E)).T.astype(BF16)
    kn = _head_rms(yd[:, HQ:HQ + LANES], gdk_ref[...], bd)
    kd_ref[...] = _rope(kn, rope64_ref, DH // 4).astype(BF16)
    _store_vt(vdt_ref, 0, yd[:, HQ + LANES:])

    yc = _dot(u, wc_ref[...])

    for c in range(HQ // LANES):
        lo, hi = c * LANES, (c + 1) * LANES
        pb_ref[:, lo:hi] = (_rope(yb[:, lo:hi], rope64_ref, DH // 4) * (S_H * LOG2E)).astype(BF16)
    pb_ref[:, HQ:HQ + LANES] = _rope(yb[:, HQ:HQ + LANES], rope64_ref, DH // 4).astype(BF16)
    pb_ref[:, HQ + LANES:] = yb[:, HQ + LANES:].astype(BF16)

    for hh in range(N_HEADS):
        lo, hi = hh * HEAD_PAD_A, (hh + 1) * HEAD_PAD_A
        q = _rope(qa[:, lo:hi], ropea_ref, ROPE_A // 4) * (S_A * LOG2E)
        qat_ref[lo:hi, :] = q.T.astype(BF16)
        ka_ref[:, lo:hi] = (ka[:, lo:hi] + kr).astype(BF16)
    for c in range(HQ // LANES):
        lo, hi = c * LANES, (c + 1) * LANES
        _store_vt(vat_ref, c, va[:, lo:hi])

    pc_ref[:, :HQ] = (yc[:, :HQ] * (S_H * LOG2E)).astype(BF16)
    pc_ref[:, HQ:] = yc[:, HQ:].astype(BF16)


def _h_specs(h, d, tiles_per_batch):
    if not isinstance(h, tuple):
        return [pl.BlockSpec((TM, d), lambda i: (i, 0))]
    lat = tiles_per_batch - 1
    return [pl.BlockSpec((TM, d), lambda i: ((i // tiles_per_batch) * lat
                                             + jnp.maximum(i % tiles_per_batch - 1, 0), 0)),
            pl.BlockSpec((TM, d), lambda i: (i // tiles_per_batch, 0))]


def _project(h, mod, g_pre, w, tabs, n_batch, tiles_per_batch):
    hs = h if isinstance(h, tuple) else (h,)
    d = hs[0].shape[1]
    m = n_batch * tiles_per_batch * TM
    s = tiles_per_batch * TM
    wq_a = N_HEADS * HEAD_PAD_A
    row = lambda i: (i, 0)
    col = lambda i: (i // tiles_per_batch, 0, i % tiles_per_batch)
    tile_pos = lambda i: (0, i % tiles_per_batch, 0)
    in_specs = _h_specs(h, d, tiles_per_batch) + [
        _mod_spec(d, tiles_per_batch),
        _resident((1, d)),
        _resident(w["wa"].shape), _resident(w["wb"].shape), _resident(w["wc"].shape),
        _resident(w["wd"].shape), _resident(w["wuq"].shape),
        _resident(w["wk"].shape), _resident(w["wv"].shape),
        _resident((1, Q_LORA)), _resident((1, KV_LORA)), _resident((1, LANES)), _resident((1, LANES)),
        pl.BlockSpec((3, TM, LANES), tile_pos),
        pl.BlockSpec((3, TM, LANES), tile_pos),
    ]

    def rows_out(c):
        return pl.BlockSpec((TM, c), row), jax.ShapeDtypeStruct((m, c), BF16)

    def cols_out(r):
        return pl.BlockSpec((None, r, TM), col), jax.ShapeDtypeStruct((n_batch, r, s), BF16)

    outs = [rows_out(wq_a), cols_out(wq_a), cols_out(N_HEADS * VT_ROWS),
            rows_out(HQ + 2 * LANES), rows_out(3 * HQ),
            rows_out(LANES), cols_out(HQ), cols_out(KV_D * VT_ROWS)]
    return pl.pallas_call(
        functools.partial(_proj_kernel, n_h=len(hs), tiles_per_batch=tiles_per_batch),
        grid=(m // TM,),
        in_specs=in_specs,
        out_specs=[o[0] for o in outs],
        out_shape=[o[1] for o in outs],
        compiler_params=_cparams(1),
        name="proj",
    )(*hs, mod, g_pre, w["wa"], w["wb"], w["wc"], w["wd"], w["wuq"], w["wk"], w["wv"],
      w["gaq"], w["gakv"], w["gdq"], w["gdk"], tabs["rope_a"], tabs["rope_64"])


def _dense_attn_kernel(ka_ref, qat_ref, vat_ref, kd_ref, qdt_ref, vdt_ref, oa_ref, od_ref, ot_ref, st_ref,
                       *, n_ctx, q_first):
    qi = pl.program_id(1) + q_first
    group_d = N_HEADS // (LANES // DH)
    items = [(mixer, h) for mixer in range(2) for h in range(N_HEADS)]

    def attend(n_keys):
        n_chunks = -(-n_keys // KEY_CHUNK)
        bounds = [(i * n_keys // n_chunks, (i + 1) * n_keys // n_chunks) for i in range(n_chunks)]

        def q_operand(item):
            mixer, h = item
            if mixer == 0:
                return qat_ref[h * LANES:(h + 1) * LANES, :]
            qh = qdt_ref[h * DH:(h + 1) * DH, :]
            z = jnp.zeros_like(qh)
            return jnp.concatenate([qh, z] if h // group_d == 0 else [z, qh], axis=0)

        def scores(n, qt, i):
            mixer, h = items[n]
            lo, hi = bounds[i]
            k = ka_ref[lo:hi, h * LANES:(h + 1) * LANES] if mixer == 0 else kd_ref[lo:hi, :]
            st_ref[n % DENSE_SLOTS, lo:hi, :] = _dot(k, qt)

        def colmax(n):
            m = None
            for lo, hi in bounds:
                mi = jnp.max(st_ref[n % DENSE_SLOTS, lo:hi, :], axis=0, keepdims=True)
                m = mi if m is None else jnp.maximum(m, mi)
            return m

        ahead = DENSE_SLOTS - 1
        for n0 in range(ahead):
            qt = q_operand(items[n0])
            for i in range(n_chunks):
                scores(n0, qt, i)
        for n, (mixer, h) in enumerate(items):
            vt_ref, g = (vat_ref, h) if mixer == 0 else (vdt_ref, h // group_d)
            m = colmax(n)
            qt = q_operand(items[n + ahead]) if n + ahead < len(items) else None
            acc = None
            for i in range(n_chunks):
                if qt is not None:
                    scores(n + ahead, qt, i)
                lo, hi = bounds[i]
                p = jnp.exp2(st_ref[n % DENSE_SLOTS, lo:hi, :] - m).astype(BF16)
                ai = _dot(vt_ref[g * VT_ROWS:(g + 1) * VT_ROWS, lo:hi], p)
                acc = ai if acc is None else acc + ai
            ot_ref[mixer, h * DH:(h + 1) * DH, :] = acc[:DH] / acc[DH:DH + 1]
            if h == N_HEADS - 1:
                o_ref = oa_ref if mixer == 0 else od_ref
                o_ref[...] = ot_ref[mixer].T.astype(o_ref.dtype)

    @pl.when(qi < n_ctx // TM)
    def _():
        attend(n_ctx)

    @pl.when(qi >= n_ctx // TM)
    def _():
        attend(ka_ref.shape[0])


def _dense_attention(ka, qat, vat, kd, qdt, vdt, n_ctx, q_first):
    b, s, _ = ka.shape
    whole = lambda bi, qi: (bi, 0, 0)
    q_tile = lambda bi, qi: (bi, 0, qi + q_first)
    o_tile = lambda bi, qi: (bi, qi + q_first, 0)
    return pl.pallas_call(
        functools.partial(_dense_attn_kernel, n_ctx=n_ctx, q_first=q_first),
        grid=(b, s // TM - q_first),
        in_specs=[
            pl.BlockSpec((None, s, ka.shape[2]), whole),
            pl.BlockSpec((None, qat.shape[1], TM), q_tile),
            pl.BlockSpec((None, vat.shape[1], s), whole),
            pl.BlockSpec((None, s, kd.shape[2]), whole),
            pl.BlockSpec((None, qdt.shape[1], TM), q_tile),
            pl.BlockSpec((None, vdt.shape[1], s), whole),
        ],
        out_specs=[pl.BlockSpec((None, TM, HQ), o_tile)] * 2,
        out_shape=[jax.ShapeDtypeStruct((b, s, HQ), BF16)] * 2,
        scratch_shapes=[pltpu.VMEM((2, HQ, TM), F32), pltpu.VMEM((DENSE_SLOTS, s, TM), F32)],
        compiler_params=_cparams(2),
        name="attn_dense",
    )(ka, qat, vat, kd, qdt, vdt)


def _pair_attn_kernel(sink_ref, qb_ref, kb_ref, vb_ref, qc_ref, kc_ref, vc_ref, tb_ref, ob_ref, oc_ref, s_ref,
                      *, n_ctx, q_first):
    qi = pl.program_id(1) + q_first
    n_lat = kb_ref.shape[0] - n_ctx
    n_blk = HQ // LANES
    lanes = lambda j: slice(j * LANES, (j + 1) * LANES)

    def ctx_b():
        return kb_ref[0:n_ctx, :], lambda: vb_ref[0:n_ctx, :], None

    def ctx_c(j):
        return kc_ref[0:n_ctx, lanes(j)], lambda: vc_ref[0:n_ctx, lanes(j)], None

    def sinks(j):
        return sink_ref[GQA_HEAD_ORDER[2 * j]], sink_ref[GQA_HEAD_ORDER[2 * j + 1]]

    @pl.when(qi < n_ctx // TM)
    def _():
        def scores(item, slot):
            mixer, j = item
            if mixer == "b":
                return _pair_scores(qb_ref[:, lanes(j)], [ctx_b()], s_ref, slot)
            return _pair_scores(qc_ref[:, lanes(j)], [ctx_c(j)], s_ref, slot)

        def finish(item, handle, slot):
            mixer, j = item
            if mixer == "b":
                ob_ref[:, lanes(j)] = _pair_finish(handle, s_ref, slot, sinks(j)).astype(ob_ref.dtype)
            else:
                oc_ref[:, lanes(j)] = _pair_finish(handle, s_ref, slot).astype(oc_ref.dtype)

        _pipelined([(mixer, j) for mixer in "bc" for j in range(n_blk)], scores, finish)

    @pl.when(qi >= n_ctx // TM)
    def _():
        span_b = TM + 2 * WINDOW
        q0 = (qi - n_ctx // TM) * TM
        start = pl.multiple_of(jnp.clip(q0 - WINDOW, 0, n_lat - span_b), WINDOW)
        dist = (lax.broadcasted_iota(jnp.int32, (2 * TM, span_b), 0)
                - lax.broadcasted_iota(jnp.int32, (2 * TM, span_b), 1) + (q0 - start))
        dist = jnp.where(lax.broadcasted_iota(jnp.int32, (2 * TM, span_b), 0) >= TM, dist - TM, dist)
        bias_b = jnp.where(jnp.abs(dist) <= WINDOW, 0.0, NEG)
        rows_b = pl.ds(n_ctx + start, span_b)
        grid_rows = n_lat // GRID_W
        rows_per_tile = TM // GRID_W
        span_c = KH * GRID_W

        def scores(item, slot):
            if item[0] == "b":
                j = item[1]
                return _pair_scores(qb_ref[:, lanes(j)],
                                    [(kb_ref[rows_b, :], lambda: vb_ref[rows_b, :], bias_b), ctx_b()], s_ref, slot)
            _, rr, j = item
            r = (qi - n_ctx // TM) * rows_per_tile + rr
            r0 = jnp.clip(r - KH // 2, 0, grid_rows - KH)
            delta = r - r0
            krows = pl.ds(pl.multiple_of(n_ctx + r0 * GRID_W, GRID_W), span_c)
            bias = jnp.concatenate(
                [tb_ref[j, 2 * jj - delta + (KH - 1)] for jj in range(KH // 2)], axis=1)
            return _pair_scores(qc_ref[rr * GRID_W:(rr + 1) * GRID_W, lanes(j)],
                                [(kc_ref[krows, lanes(j)], lambda: vc_ref[krows, lanes(j)], bias), ctx_c(j)],
                                s_ref, slot)

        def finish(item, handle, slot):
            if item[0] == "b":
                j = item[1]
                ob_ref[:, lanes(j)] = _pair_finish(handle, s_ref, slot, sinks(j)).astype(ob_ref.dtype)
            else:
                _, rr, j = item
                oc_ref[rr * GRID_W:(rr + 1) * GRID_W, lanes(j)] = (
                    _pair_finish(handle, s_ref, slot).astype(oc_ref.dtype))

        items = [("b", j) for j in range(n_blk)]
        items += [("c", rr, j) for rr in range(rows_per_tile) for j in range(n_blk)]
        _pipelined(items, scores, finish)


def _pair_attention(pb, pc, sink, bias_tab, n_ctx, q_first):
    b, s, _ = pb.shape
    q_tile = lambda bi, qi: (bi, qi + q_first, 0)
    col_blk = lambda c: (lambda bi, qi: (bi, 0, c))
    return pl.pallas_call(
        functools.partial(_pair_attn_kernel, n_ctx=n_ctx, q_first=q_first),
        grid=(b, s // TM - q_first),
        in_specs=[
            pl.BlockSpec(memory_space=pltpu.SMEM),
            pl.BlockSpec((None, TM, HQ), q_tile),
            pl.BlockSpec((None, s, LANES), col_blk(HQ // LANES)),
            pl.BlockSpec((None, s, LANES), col_blk(HQ // LANES + 1)),
            pl.BlockSpec((None, TM, HQ), q_tile),
            pl.BlockSpec((None, s, HQ), col_blk(1)),
            pl.BlockSpec((None, s, HQ), col_blk(2)),
            _resident(bias_tab.shape),
        ],
        out_specs=[pl.BlockSpec((None, TM, HQ), q_tile)] * 2,
        out_shape=[jax.ShapeDtypeStruct((b, s, HQ), BF16)] * 2,
        scratch_shapes=[pltpu.VMEM((PAIR_SLOTS, 2 * TM, max(TM + 2 * WINDOW, KH * GRID_W) + n_ctx), F32)],
        compiler_params=_cparams(2),
        name="attn_pair",
    )(sink, pb, pb, pb, pc, pc, pc, bias_tab)


def _merge_kernel(*refs, n_h, tiles_per_batch):
    h = _load_h(refs[:n_h], tiles_per_batch)
    (oa_ref, ob_ref, oc_ref, od_ref, mod_ref, gpre_ref, gpost_ref,
     wg_ref, wbr_ref, wout_ref, o_ref, acc_ref) = refs[n_h:]
    d_model = h.shape[1]
    cw = 256
    sh1 = mod_ref[0:1, :]
    sc1 = mod_ref[1:2, :]
    ga1 = mod_ref[2:3, :]
    branches = (oa_ref, ob_ref, oc_ref, od_ref)

    def up_proj(n, c):
        return _dot(branches[n][...], wbr_ref[n, :, c * cw:(c + 1) * cw])

    up0 = [up_proj(n, 0) for n in range(N_BRANCH)]
    u = (_rms(h, gpre_ref[...]) * (1.0 + sc1) + sh1).astype(BF16)
    for c in range(d_model // cw):
        acc = None
        for n in range(N_BRANCH):
            lo = n * d_model + c * cw
            gate = _dot(u, wg_ref[:, lo:lo + cw])
            up = up0[n] if c == 0 else up_proj(n, c)
            term = _sigmoid(gate) * up
            acc = term if acc is None else acc + term
        acc_ref[:, c * cw:(c + 1) * cw] = acc.astype(BF16)
    y = _dot(acc_ref[...], wout_ref[...])
    o_ref[...] = h + ga1 * _rms(y, gpost_ref[...])


def _merge(o_a, o_b, o_c, o_d, h, mod, g_pre, g_post, wg, wbr, wout, tiles_per_batch, latent_only):
    split = isinstance(h, tuple)
    hs = h if split else (h,)
    m, d = o_a.shape[0], hs[0].shape[1]
    n, row_of, mod_of = _tile_visits(m, tiles_per_batch, latent_only)
    assert not (split and latent_only)
    row = lambda i: (row_of(i), 0)
    h_specs = _h_specs(h, d, tiles_per_batch) if split else [pl.BlockSpec((TM, d), row)]
    return pl.pallas_call(
        functools.partial(_merge_kernel, n_h=len(hs), tiles_per_batch=tiles_per_batch),
        grid=(n,),
        in_specs=h_specs + [pl.BlockSpec((TM, HQ), row)] * N_BRANCH + [
            pl.BlockSpec((None, None, MOD_CHUNKS, d), lambda i: (*mod_of(i), 0, 0)),
            _resident((1, d)), _resident((1, d)),
            _resident(wg.shape), _resident(wbr.shape), _resident(wout.shape),
        ],
        out_specs=pl.BlockSpec((TM, d), row),
        out_shape=jax.ShapeDtypeStruct((m, d), F32),
        scratch_shapes=[pltpu.VMEM((TM, d), BF16)],
        input_output_aliases={} if split else {0: 0},
        compiler_params=_cparams(1),
        name="merge",
    )(*hs, o_a, o_b, o_c, o_d, mod, g_pre, g_post, wg, wbr, wout)


def _ffn_kernel(hn_ref, hp_ref, modn_ref, modp_ref, gpre_ref, gpost_ref, w1_ref, w2_ref, o_ref,
                u_ref, y_ref, act_ref):
    i = pl.program_id(0)
    d_ff = w2_ref.shape[0]
    cw = 256
    cur = i % 2
    oth = 1 - cur

    def pre_norm(h_ref, mod_ref):
        return _rms(h_ref[...], gpre_ref[...]) * (1.0 + mod_ref[4:5, :]) + mod_ref[3:4, :]

    @pl.when(i == 0)
    def _():
        u_ref[0] = pre_norm(hp_ref, modp_ref).astype(BF16)
        y_ref[1] = jnp.zeros(y_ref.shape[1:], F32)

    out = hp_ref[...] + modp_ref[5:6, :] * _rms(y_ref[oth], gpost_ref[...])
    o_ref[...] = out
    u_next = pre_norm(hn_ref, modn_ref)
    for c in range(d_ff // cw):
        a = _dot(u_ref[cur], w1_ref[:, c * cw:(c + 1) * cw])
        b = _dot(u_ref[cur], w1_ref[:, d_ff + c * cw:d_ff + (c + 1) * cw])
        act_ref[:, c * cw:(c + 1) * cw] = (a * _sigmoid(a) * b).astype(BF16)
    u_ref[oth] = u_next.astype(BF16)
    y = _dot(act_ref[...], w2_ref[...])
    y_ref[cur] = y
    r0, c0 = y.shape[0] - F32_SUBLANES, y.shape[1] - LANES
    y_ref[cur, r0:, c0:] = y[r0:, c0:] + _zero_after(u_next, out)


def _ffn(h, mod, g_pre, g_post, w1, w2, tiles_per_batch, latent_only):
    m, d = h.shape
    n, row_of, mod_of = _tile_visits(m, tiles_per_batch, latent_only)
    p = _StepPipeline(n, row_of, mod_of, d)
    return pl.pallas_call(
        _ffn_kernel,
        grid=(n + 1,),
        in_specs=[p.rows_next, p.rows_prev, p.mod_next, p.mod_prev,
                  _resident((1, d)), _resident((1, d)), _resident(w1.shape), _resident(w2.shape)],
        out_specs=p.rows_out,
        out_shape=jax.ShapeDtypeStruct((n * TM, d), F32),
        scratch_shapes=[pltpu.VMEM((2, TM, d), BF16), pltpu.VMEM((2, TM, d), F32),
                        pltpu.VMEM((TM, w2.shape[0]), BF16)],
        compiler_params=_cparams(1),
        name="ffn",
    )(h, h, mod, mod, g_pre, g_post, w1, w2)


def _rope_tables(n_ctx, n_lat):
    t = jnp.arange(n_lat)
    rows_pos = (t // GRID_W).astype(F32)
    cols_pos = (t % GRID_W).astype(F32)

    def block(d):
        q = d // 4
        inv = ROPE_THETA ** (-jnp.arange(q, dtype=F32) * 2.0 / (d // 2))
        ar = rows_pos[:, None] * inv[None, :]
        ac = cols_pos[:, None] * inv[None, :]
        z = jnp.zeros_like(ar)
        cos = jnp.concatenate([jnp.cos(ar), jnp.cos(ar), jnp.cos(ac), jnp.cos(ac)], axis=1)
        up = jnp.concatenate([-jnp.sin(ar), z, -jnp.sin(ac), z], axis=1)
        dn = jnp.concatenate([z, jnp.sin(ar), z, jnp.sin(ac)], axis=1)
        return cos, up, dn

    def with_ctx(tab, fill):
        return jnp.concatenate([jnp.full((n_ctx, tab.shape[1]), fill, F32), tab], axis=0)

    c64, u64, d64 = block(DH)
    rope_64 = jnp.stack([with_ctx(jnp.tile(c64, (1, LANES // DH)), 1.0),
                         with_ctx(jnp.tile(u64, (1, LANES // DH)), 0.0),
                         with_ctx(jnp.tile(d64, (1, LANES // DH)), 0.0)])
    ca, ua, da = block(ROPE_A)
    pad_lo = NOPE_A
    pad_hi = HEAD_PAD_A - NOPE_A - ROPE_A

    def widen(tab, fill):
        return jnp.concatenate([jnp.full((n_lat, pad_lo), fill, F32), tab,
                                jnp.full((n_lat, pad_hi), fill, F32)], axis=1)

    rope_a = jnp.stack([with_ctx(widen(ca, 1.0), 1.0), with_ctx(widen(ua, 0.0), 0.0),
                        with_ctx(widen(da, 0.0), 0.0)])
    return {"rope_a": rope_a, "rope_64": rope_64}


def _permute_heads(w, axis, order):
    shape = w.shape
    w = w.reshape(shape[:axis] + (N_HEADS, DH) + shape[axis + 1:])
    w = jnp.take(w, jnp.asarray(order), axis=axis)
    return w.reshape(shape)


def _layer_weights(l, w_in, g_a_q, g_a_kv, w_a_uq, w_a_ukv, g_d_q, g_d_k, w_branch):
    d = w_in.shape[1]
    wl = w_in[l]
    o = 0
    a_cq = wl[:, o:o + Q_LORA]; o += Q_LORA
    a_ckv = wl[:, o:o + KV_LORA]; o += KV_LORA
    a_kr = wl[:, o:o + ROPE_A]; o += ROPE_A
    w_bq = wl[:, o:o + HQ]; o += HQ
    w_bkv = wl[:, o:o + 2 * KV_B * DH]; o += 2 * KV_B * DH
    w_c = wl[:, o:o + 3 * HQ]; o += 3 * HQ
    w_d = wl[:, o:o + HQ + 2 * KV_D * DH]; o += HQ + 2 * KV_D * DH
    w_g = wl[:, o:]
    zeros = lambda n: jnp.zeros((d, n), wl.dtype)
    wa = jnp.concatenate([a_cq, a_ckv, zeros(NOPE_A), a_kr, zeros(HEAD_PAD_A - NOPE_A - ROPE_A)], axis=1)
    w_b = jnp.concatenate([_permute_heads(w_bq, 1, GQA_HEAD_ORDER), w_bkv], axis=1)
    uq = w_a_uq[l].reshape(Q_LORA, N_HEADS, NOPE_A + ROPE_A)
    uq = jnp.pad(uq, ((0, 0), (0, 0), (0, HEAD_PAD_A - NOPE_A - ROPE_A))).reshape(Q_LORA, N_HEADS * HEAD_PAD_A)
    ukv = w_a_ukv[l].reshape(KV_LORA, N_HEADS, NOPE_A + DH)
    wk = jnp.pad(ukv[:, :, :NOPE_A], ((0, 0), (0, 0), (0, HEAD_PAD_A - NOPE_A))).reshape(KV_LORA, N_HEADS * HEAD_PAD_A)
    wv = ukv[:, :, NOPE_A:].reshape(KV_LORA, HQ)
    wbr = w_branch[l]
    wbr = jnp.stack([wbr[0], _permute_heads(wbr[1], 0, GQA_HEAD_ORDER), wbr[2], wbr[3]])
    cast = lambda x: x.astype(BF16)
    return {
        "wa": cast(wa), "wb": cast(w_b), "wc": cast(w_c), "wd": cast(w_d), "wg": cast(w_g),
        "wuq": cast(uq), "wk": cast(wk), "wv": cast(wv), "wbr": cast(wbr),
        "gaq": g_a_q[l][None, :], "gakv": g_a_kv[l][None, :],
        "gdq": jnp.tile(g_d_q[l], LANES // DH)[None, :], "gdk": jnp.tile(g_d_k[l], LANES // DH)[None, :],
    }


def kernel(x, c, ctx, c_ctx, w_mod, b_mod, g_pre_mix, g_post_mix, g_pre_ffn, g_post_ffn,
           w_in, g_a_q, g_a_kv, w_a_uq, w_a_ukv, sink_b, rpb_c, g_d_q, g_d_k,
           w_branch, w_out, w_ffn_in, w_ffn_out):
    b, n_lat, d = x.shape
    n_ctx = ctx.shape[1]
    s = n_ctx + n_lat
    depth = w_mod.shape[0]
    assert n_ctx == TM and n_lat % TM == 0 and n_lat >= TM + 2 * WINDOW
    assert n_lat % GRID_W == 0 and n_lat // GRID_W >= KH
    assert KV_B * DH == LANES and KV_D * DH == LANES
    tiles_per_batch = s // TM

    n_rows = -(-(b + 1) // F32_SUBLANES) * F32_SUBLANES
    cc = jnp.zeros((n_rows, d), F32).at[:b].set(c).at[b].set(c_ctx)
    mods = _modulation(cc, w_mod, b_mod)
    mod_x = mods[:, :b].reshape(depth, b, 1, MOD_CHUNKS, d)
    mod_c = jnp.broadcast_to(mods[:, b].reshape(depth, 1, 1, MOD_CHUNKS, d), mod_x.shape)
    mod_tab = jnp.concatenate([mod_c, mod_x], axis=2)

    tabs = _rope_tables(n_ctx, n_lat)
    n_rpb_heads = rpb_c.shape[1]
    nbr_tabs = _nbr_bias(rpb_c.reshape((depth * n_rpb_heads,) + rpb_c.shape[2:]))
    nbr_tabs = nbr_tabs.reshape((depth, n_rpb_heads // 2) + nbr_tabs.shape[1:])
    h = (x.reshape(b * n_lat, d), ctx.reshape(b * n_ctx, d))
    if depth == 1:
        h = jnp.concatenate([ctx, x], axis=1).reshape(b * s, d)

    for l in range(depth):
        w = _layer_weights(l, w_in, g_a_q, g_a_kv, w_a_uq, w_a_ukv, g_d_q, g_d_k, w_branch)
        mod = mod_tab[l]
        g_pre = g_pre_mix[l][None, :]
        last = l == depth - 1
        q_first = n_ctx // TM if last else 0
        ka, qat, vat, pb, pc, kd, qdt, vdt = _project(h, mod, g_pre, w, tabs, b, tiles_per_batch)
        o_a, o_d = _dense_attention(ka.reshape(b, s, -1), qat, vat, kd.reshape(b, s, -1), qdt, vdt,
                                    n_ctx, q_first)
        o_b, o_c = _pair_attention(pb.reshape(b, s, -1), pc.reshape(b, s, -1), sink_b[l], nbr_tabs[l],
                                   n_ctx, q_first)
        flat = lambda t: t.reshape(b * s, HQ)
        h = _merge(flat(o_a), flat(o_b), flat(o_c), flat(o_d), h, mod, g_pre, g_post_mix[l][None, :],
                   w["wg"], w["wbr"], w_out[l].astype(BF16), tiles_per_batch, last)
        h = _ffn(h, mod, g_pre_ffn[l][None, :], g_post_ffn[l][None, :],
                 w_ffn_in[l].astype(BF16), w_ffn_out[l].astype(BF16), tiles_per_batch, last)
    return h.reshape(b, n_lat, d)
```
